```python
import jax, jax.numpy as jnp
from jax import lax
import numpy as np

D_MODEL = 1024
BATCH = 8
SEQ = 4096
DEPTH = 1

MIX_WIDTH = D_MODEL
GMLP_WIDTH = MIX_WIDTH // 2
RWKV_WIDTH = MIX_WIDTH - GMLP_WIDTH
HEAD_DIM = 64
GMLP_HEADS = GMLP_WIDTH // HEAD_DIM
RWKV_HEADS = RWKV_WIDTH // HEAD_DIM
CHUNK = 128
DECAY_LORA = 32
ICLR_LORA = 32
GATE_LORA = 96
RWKV_PROJ = 3 * RWKV_WIDTH + DECAY_LORA + ICLR_LORA + GATE_LORA
IN_WIDTH = 2 * GMLP_WIDTH + RWKV_PROJ
N_GROUPS = 4
EXPERTS_PER_GROUP = 8
N_EXPERTS = N_GROUPS * EXPERTS_PER_GROUP
TOP_K = 2
EXPERT_HIDDEN = 512
MOE_BLOCK = 128
RMS_EPS = 1e-6
LN_EPS = 1e-5
GN_EPS = 64e-5

kernel_name = "hybrid_gmlp_rwkv7_hmoe_block"


def _rmsnorm(x, g):
    x32 = x.astype(jnp.float32)
    y = x32 * lax.rsqrt(jnp.mean(x32 * x32, axis=-1, keepdims=True) + RMS_EPS)
    return y.astype(x.dtype) * g


def _modulate(h, shift, scale):
    return h * (1 + scale[:, None, :]) + shift[:, None, :]


def _token_shift(z):
    return jnp.pad(z, ((0, 0), (1, 0), (0, 0)))[:, :-1]


def _gmlp_mixer(zu, zv, ln_w, ln_b, ws, bs):
    B, S, _ = zu.shape
    u = jax.nn.gelu(zu)
    v = jax.nn.gelu(zv)
    v32 = v.astype(jnp.float32)
    m = jnp.mean(v32, axis=-1, keepdims=True)
    var = jnp.mean(jnp.square(v32 - m), axis=-1, keepdims=True)
    v = ((v32 - m) * lax.rsqrt(var + LN_EPS)).astype(zu.dtype) * ln_w + ln_b
    vc = v.reshape(B, S // CHUNK, CHUNK, GMLP_HEADS, HEAD_DIM)
    w_causal = jnp.tril(ws)
    mixed = jnp.einsum('hts,bcshd->bcthd', w_causal, vc) + bs.T[None, None, :, :, None]
    return u * mixed.reshape(B, S, GMLP_WIDTH)


def _wkv7_scan(r, w, k, v, a, b):
    Bn, S, H, N = r.shape
    seq = tuple(jnp.moveaxis(t, 1, 0) for t in (r, w, k, v, a, b))

    def step(state, inp):
        r_t, w_t, k_t, v_t, a_t, b_t = inp
        sa = jnp.einsum('bhvk,bhk->bhv', state, a_t)
        state = (state * w_t[:, :, None, :] + sa[..., None] * b_t[:, :, None, :]
                 + v_t[..., None] * k_t[:, :, None, :])
        return state, jnp.einsum('bhvk,bhk->bhv', state, r_t)

    s0 = jnp.zeros((Bn, H, N, N), jnp.float32)
    _, y = lax.scan(step, s0, seq)
    return jnp.moveaxis(y, 0, 1)


def _rwkv7_mixer(z, mu, w0, w2, a0, a2, g2, k_k, k_a, r_k, gn_w, gn_b):
    B, S, _ = z.shape
    f32 = jnp.float32
    z = z + (_token_shift(z) - z) * mu
    W = RWKV_WIDTH
    r, k, v, xw, xa, xg = jnp.split(
        z, [W, 2 * W, 3 * W, 3 * W + DECAY_LORA, 3 * W + DECAY_LORA + ICLR_LORA], axis=-1)
    w_log = -jax.nn.softplus(-(w0 + jnp.tanh(xw) @ w2).astype(f32)) - 0.5
    decay = jnp.exp(-jnp.exp(w_log))
    a = jax.nn.sigmoid(a0 + xa @ a2)
    g = jax.nn.sigmoid(xg) @ g2
    hv = lambda t: t.reshape(B, S, RWKV_HEADS, HEAD_DIM)
    kk = hv(k * k_k).astype(f32)
    kk = kk / jnp.maximum(jnp.sqrt(jnp.sum(kk * kk, axis=-1, keepdims=True)), 1e-12)
    k = k * (1 + (a - 1) * k_a)
    r4, k4, v4, a4 = hv(r), hv(k), hv(v), hv(a)
    y = _wkv7_scan(r4.astype(f32), hv(decay), k4.astype(f32), v4.astype(f32),
                   -kk, kk * a4.astype(f32))
    m = jnp.mean(y, axis=-1, keepdims=True)
    var = jnp.mean(jnp.square(y - m), axis=-1, keepdims=True)
    y = ((y - m) * lax.rsqrt(var + GN_EPS)).reshape(B, S, W).astype(z.dtype) * gn_w + gn_b
    bonus = jnp.sum(r4 * k4 * r_k, axis=-1, keepdims=True) * v4
    return (y + bonus.reshape(B, S, W)) * g


def _hier_moe(h, rg_w, rg_b, re_w, re_b, w_gate, w_up, w_down):
    B, S, D = h.shape
    N = B * S
    f32 = jnp.float32
    ht = h.reshape(N, D)
    group_prob = jax.nn.softmax((ht @ rg_w + rg_b).astype(f32), axis=-1)
    g_p, g_idx = lax.top_k(group_prob, 1)
    exp_logits = (ht @ re_w + re_b).astype(f32).reshape(N, N_GROUPS, EXPERTS_PER_GROUP)
    sel = jnp.take_along_axis(exp_logits, g_idx[:, :, None], axis=1)[:, 0]
    e_p, e_idx = lax.top_k(jax.nn.softmax(sel, axis=-1), TOP_K)
    e_p = e_p / jnp.sum(e_p, axis=-1, keepdims=True)
    weights = g_p * e_p
    expert = g_idx * EXPERTS_PER_GROUP + e_idx
    A = N * TOP_K
    flat_e = expert.reshape(A)
    flat_w = weights.reshape(A)
    flat_tok = jnp.repeat(jnp.arange(N, dtype=jnp.int32), TOP_K)
    order = jnp.argsort(flat_e)
    se, stok, sw = flat_e[order], flat_tok[order], flat_w[order]
    counts = jnp.bincount(flat_e, length=N_EXPERTS)
    padded = (counts + MOE_BLOCK - 1) // MOE_BLOCK * MOE_BLOCK
    pad_end = jnp.cumsum(padded)
    pad_start = pad_end - padded
    start = jnp.cumsum(counts) - counts
    dest = pad_start[se] + jnp.arange(A) - start[se]
    n_blocks = -(-A // MOE_BLOCK) + N_EXPERTS
    P = n_blocks * MOE_BLOCK
    tok_buf = jnp.full((P,), N, jnp.int32).at[dest].set(stok)
    w_buf = jnp.zeros((P,), f32).at[dest].set(sw)
    block_e = jnp.minimum(
        jnp.searchsorted(pad_end, jnp.arange(n_blocks) * MOE_BLOCK, side='right'), N_EXPERTS - 1)
    x_pad = jnp.concatenate([ht, jnp.zeros((1, D), ht.dtype)], axis=0)
    xb = x_pad[tok_buf].reshape(n_blocks, MOE_BLOCK, D)

    def expert_block(args):
        xbk, e = args
        return (jax.nn.silu(xbk @ w_gate[e]) * (xbk @ w_up[e])) @ w_down[e]

    yb = lax.map(expert_block, (xb, block_e)).reshape(P, D)
    y = jnp.zeros((N + 1, D), yb.dtype).at[tok_buf].add(yb * w_buf[:, None].astype(yb.dtype))
    return y[:N].reshape(B, S, D)


def setup_inputs(seed: int = 0) -> dict:
    key = jax.random.key(seed)
    ks = jax.random.split(key, 40)
    L, D = DEPTH, D_MODEL
    nrm = lambda k, shape, s: jax.random.normal(k, shape, jnp.float32) * s
    uni = lambda k, shape, lo, hi: jax.random.uniform(k, shape, jnp.float32, lo, hi)
    return {
        "x": nrm(ks[0], (BATCH, SEQ, D), 1.0),
        "c": nrm(ks[1], (BATCH, D), 1.0),
        "ada_w": nrm(ks[2], (L, D, 6 * D), 0.3 * D ** -0.5),
        "ada_b": nrm(ks[3], (L, 6 * D), 0.01),
        "norm1_g": 1.0 + nrm(ks[4], (L, D), 0.05),
        "w_in": nrm(ks[5], (L, D, IN_WIDTH), D ** -0.5),
        "gmlp_ln_w": 1.0 + nrm(ks[6], (L, GMLP_WIDTH), 0.05),
        "gmlp_ln_b": nrm(ks[7], (L, GMLP_WIDTH), 0.02),
        "gmlp_ws": nrm(ks[8], (L, GMLP_HEADS, CHUNK, CHUNK), CHUNK ** -0.5),
        "gmlp_bs": 1.0 + nrm(ks[9], (L, GMLP_HEADS, CHUNK), 0.1),
        "rwkv_mu": uni(ks[10], (L, RWKV_PROJ), 0.0, 1.0),
        "rwkv_w0": uni(ks[11], (L, RWKV_WIDTH), -6.0, 0.0),
        "rwkv_w2": nrm(ks[12], (L, DECAY_LORA, RWKV_WIDTH), 0.1),
        "rwkv_a0": nrm(ks[13], (L, RWKV_WIDTH), 0.5),
        "rwkv_a2": nrm(ks[14], (L, ICLR_LORA, RWKV_WIDTH), ICLR_LORA ** -0.5),
        "rwkv_g2": nrm(ks[15], (L, GATE_LORA, RWKV_WIDTH), GATE_LORA ** -0.5),
        "rwkv_k_k": 0.85 + nrm(ks[16], (L, RWKV_WIDTH), 0.05),
        "rwkv_k_a": 1.0 + nrm(ks[17], (L, RWKV_WIDTH), 0.05),
        "rwkv_r_k": nrm(ks[18], (L, RWKV_HEADS, HEAD_DIM), 0.1),
        "rwkv_gn_w": 1.0 + nrm(ks[19], (L, RWKV_WIDTH), 0.05),
        "rwkv_gn_b": nrm(ks[20], (L, RWKV_WIDTH), 0.02),
        "w_out": nrm(ks[21], (L, MIX_WIDTH, D), MIX_WIDTH ** -0.5),
        "norm2_g": 1.0 + nrm(ks[22], (L, D), 0.05),
        "router_group_w": nrm(ks[23], (L, D, N_GROUPS), D ** -0.5),
        "router_group_b": nrm(ks[24], (L, N_GROUPS), 0.01),
        "router_expert_w": nrm(ks[25], (L, D, N_EXPERTS), D ** -0.5),
        "router_expert_b": nrm(ks[26], (L, N_EXPERTS), 0.01),
        "moe_w_gate": nrm(ks[27], (L, N_EXPERTS, D, EXPERT_HIDDEN), D ** -0.5),
        "moe_w_up": nrm(ks[28], (L, N_EXPERTS, D, EXPERT_HIDDEN), D ** -0.5),
        "moe_w_down": nrm(ks[29], (L, N_EXPERTS, EXPERT_HIDDEN, D), EXPERT_HIDDEN ** -0.5),
        "final_norm_g": 1.0 + nrm(ks[30], (D,), 0.05),
    }


def reference(x, c, ada_w, ada_b, norm1_g, w_in, gmlp_ln_w, gmlp_ln_b, gmlp_ws, gmlp_bs,
              rwkv_mu, rwkv_w0, rwkv_w2, rwkv_a0, rwkv_a2, rwkv_g2, rwkv_k_k, rwkv_k_a,
              rwkv_r_k, rwkv_gn_w, rwkv_gn_b, w_out, norm2_g, router_group_w, router_group_b,
              router_expert_w, router_expert_b, moe_w_gate, moe_w_up, moe_w_down,
              final_norm_g):
    for l in range(DEPTH):
        mod = c @ ada_w[l] + ada_b[l]
        shift1, scale1, gate1, shift2, scale2, gate2 = jnp.split(mod, 6, axis=-1)
        h = _modulate(_rmsnorm(x, norm1_g[l]), shift1, scale1)
        z = h @ w_in[l]
        zu, zv, zr = jnp.split(z, [GMLP_WIDTH, 2 * GMLP_WIDTH], axis=-1)
        y_a = _gmlp_mixer(zu, zv, gmlp_ln_w[l], gmlp_ln_b[l], gmlp_ws[l], gmlp_bs[l])
        y_b = _rwkv7_mixer(zr, rwkv_mu[l], rwkv_w0[l], rwkv_w2[l], rwkv_a0[l], rwkv_a2[l],
                           rwkv_g2[l], rwkv_k_k[l], rwkv_k_a[l], rwkv_r_k[l],
                           rwkv_gn_w[l], rwkv_gn_b[l])
        y = jnp.concatenate([y_a, y_b], axis=-1) @ w_out[l]
        x = x + gate1[:, None, :] * y
        h = _modulate(_rmsnorm(x, norm2_g[l]), shift2, scale2)
        y = _hier_moe(h, router_group_w[l], router_group_b[l], router_expert_w[l],
                      router_expert_b[l], moe_w_gate[l], moe_w_up[l], moe_w_down[l])
        x = x + gate2[:, None, :] * y
    return _rmsnorm(x, final_norm_g)
```

```python
import functools

import jax
import jax.numpy as jnp
from jax import lax
from jax.experimental import pallas as pl
from jax.experimental.pallas import tpu as pltpu

F32 = jnp.float32
BF16 = jnp.bfloat16
I32 = jnp.int32

HEAD_DIM = 64
GMLP_WIDTH = 512
RWKV_WIDTH = 512
GMLP_CHUNK = 128
DECAY_LORA = 32
ICLR_LORA = 32
GATE_LORA = 96
LORA_PAD = 256
N_GROUPS = 4
EXPERTS_PER_GROUP = 8
N_EXPERTS = 32
RMS_EPS = 1e-6
LN_EPS = 1e-5
GN_EPS = 64e-5

LANES = 128
MXU_DIM = 256
VMEM_LIMIT = 56 * 1024 * 1024

PRE_TM = 512
SCAN_TM = 256
SCAN_L = 64
HEADS_PER_GROUP = MXU_DIM // HEAD_DIM
ROUTE_T = 512
MOE_BM = 256
DISPATCH_T = 512
ROUTE_ROWS = 8


def _dot(a, b):
    return jnp.dot(a, b, preferred_element_type=F32)


def _dot_nt(a, b):
    return lax.dot_general(a, b, (((1,), (1,)), ((), ())), preferred_element_type=F32)


def _rms(x, g):
    return x * lax.rsqrt(jnp.mean(x * x, axis=-1, keepdims=True) + RMS_EPS) * g


def _mod_kernel(c_ref, w_ref, b_ref, o_ref):
    c = c_ref[...]
    w = w_ref[...]
    c_hi = c.astype(BF16)
    c_lo = (c - c_hi.astype(F32)).astype(BF16)
    w_hi = w.astype(BF16)
    w_lo = (w - w_hi.astype(F32)).astype(BF16)
    o_ref[...] = _dot(c_hi, w_hi) + _dot(c_lo, w_hi) + _dot(c_hi, w_lo) + b_ref[...]


def _mod_call(c, ada_w, ada_b):
    bsz, d = c.shape
    n = ada_w.shape[1]
    tn = 1024
    return pl.pallas_call(
        _mod_kernel,
        grid=(n // tn,),
        in_specs=[pl.BlockSpec((bsz, d), lambda j: (0, 0)),
                  pl.BlockSpec((d, tn), lambda j: (0, j)),
                  pl.BlockSpec((1, tn), lambda j: (0, j))],
        out_specs=pl.BlockSpec((bsz, tn), lambda j: (0, j)),
        out_shape=jax.ShapeDtypeStruct((bsz, n), F32),
        compiler_params=pltpu.CompilerParams(dimension_semantics=("arbitrary",),
                                             vmem_limit_bytes=VMEM_LIMIT),
        name="adaln_mod",
    )(c, ada_w, ada_b.reshape(1, n))


def _token_shift(z, carry_ref):
    tm = z.shape[0]
    prev = pltpu.roll(z, 1, axis=0)
    row = lax.broadcasted_iota(I32, z.shape, 0)
    prev = jnp.where(row == 0, carry_ref[0:1, :], prev)
    carry_ref[0:1, :] = z[tm - 1:tm, :]
    return prev


def _pre_kernel(x_ref, mod_ref, g1_ref, wuv_ref, wrkv_ref, wl_ref, lnw_ref, lnb_ref, wsp_ref,
                bsx_ref, mu_rkv_ref, mu_l_ref, lora_w_ref, w0_ref, a0_ref, kk_ref, ka_ref, ones_ref,
                ya_ref, r_ref, k2_ref, v_ref, kkn_ref, al_ref, g_ref, lw_ref,
                carry_rkv, carry_l):
    tm = x_ref.shape[1]

    @pl.when(pl.program_id(1) == 0)
    def _():
        carry_rkv[...] = jnp.zeros_like(carry_rkv)
        carry_l[...] = jnp.zeros_like(carry_l)

    x = x_ref[0]
    shift1 = mod_ref[0, 0:1, :]
    scale1 = mod_ref[0, 1:2, :]
    h = _rms(x, g1_ref[...]) * (1.0 + scale1) + shift1
    hb = h.astype(BF16)
    zuv = _dot(hb, wuv_ref[...])
    zr = _dot(hb, wrkv_ref[...])
    zl = _dot(hb, wl_ref[...])

    u = jax.nn.gelu(zuv[:, :GMLP_WIDTH])
    v = jax.nn.gelu(zuv[:, GMLP_WIDTH:])
    m = jnp.mean(v, axis=-1, keepdims=True)
    var = jnp.mean(jnp.square(v - m), axis=-1, keepdims=True)
    vn = ((v - m) * lax.rsqrt(var + LN_EPS) * lnw_ref[...] + lnb_ref[...]).astype(BF16)
    lane = lax.broadcasted_iota(I32, (GMLP_CHUNK, LANES), 1)
    left = lane < HEAD_DIM
    trow = lax.broadcasted_iota(I32, (GMLP_CHUNK, 2 * GMLP_CHUNK), 0)
    tcol = lax.broadcasted_iota(I32, (GMLP_CHUNK, 2 * GMLP_CHUNK), 1) % GMLP_CHUNK
    causal = tcol <= trow
    zero = jnp.zeros((), BF16)
    for p in range(GMLP_WIDTH // LANES):
        wpair = jnp.where(causal, wsp_ref[p], 0.0).astype(BF16)
        for c in range(tm // GMLP_CHUNK):
            rows = slice(c * GMLP_CHUNK, (c + 1) * GMLP_CHUNK)
            cols = slice(p * LANES, (p + 1) * LANES)
            vc = vn[rows, cols]
            rhs = jnp.concatenate([jnp.where(left, vc, zero), jnp.where(left, zero, vc)], axis=0)
            mixed = _dot(wpair, rhs) + bsx_ref[:, cols]
            ya_ref[0, rows, cols] = (u[rows, cols] * mixed).astype(ya_ref.dtype)

    zr = zr + (_token_shift(zr, carry_rkv) - zr) * mu_rkv_ref[...]
    zl = zl + (_token_shift(zl, carry_l) - zl) * mu_l_ref[...]
    r = zr[:, :RWKV_WIDTH]
    k = zr[:, RWKV_WIDTH:2 * RWKV_WIDTH]
    vv = zr[:, 2 * RWKV_WIDTH:]
    llane = lax.broadcasted_iota(I32, zl.shape, 1)
    f = jnp.where(llane < DECAY_LORA, jnp.tanh(zl),
                  jnp.where(llane < DECAY_LORA + ICLR_LORA, zl, jax.nn.sigmoid(zl)))
    lo = _dot(f.astype(BF16), lora_w_ref[...])
    wpre = w0_ref[...] + lo[:, :RWKV_WIDTH]
    w_log = -(jnp.maximum(-wpre, 0.0) + jnp.log1p(jnp.exp(-jnp.abs(wpre)))) - 0.5
    lw_ref[0] = -jnp.exp(w_log)
    alpha = jax.nn.sigmoid(a0_ref[...] + lo[:, RWKV_WIDTH:2 * RWKV_WIDTH])
    g_ref[0] = lo[:, 2 * RWKV_WIDTH:].astype(g_ref.dtype)
    kkr = k * kk_ref[...]
    ss = _dot((kkr * kkr).astype(BF16), ones_ref[...])
    kkn_ref[0] = (kkr / jnp.maximum(jnp.sqrt(ss), 1e-12)).astype(kkn_ref.dtype)
    k2_ref[0] = (k * (1.0 + (alpha - 1.0) * ka_ref[...])).astype(k2_ref.dtype)
    r_ref[0] = r.astype(r_ref.dtype)
    v_ref[0] = vv.astype(v_ref.dtype)
    al_ref[0] = alpha.astype(al_ref.dtype)


def _head_ones(width):
    hid = jnp.arange(width) // HEAD_DIM
    return (hid[:, None] == hid[None, :]).astype(BF16)


def _pre_call(x, mod3, p):
    bsz, seq, d = x.shape
    tm = min(PRE_TM, seq)
    w_in = p["w_in"]
    wuv = w_in[:, :2 * GMLP_WIDTH].astype(BF16)
    wrkv = w_in[:, 2 * GMLP_WIDTH:2 * GMLP_WIDTH + 3 * RWKV_WIDTH].astype(BF16)
    n_lora = DECAY_LORA + ICLR_LORA + GATE_LORA
    wl = jnp.pad(w_in[:, 2 * GMLP_WIDTH + 3 * RWKV_WIDTH:], ((0, 0), (0, LORA_PAD - n_lora))).astype(BF16)
    mu = p["rwkv_mu"]
    mu_rkv = mu[:3 * RWKV_WIDTH].reshape(1, -1)
    mu_l = jnp.pad(mu[3 * RWKV_WIDTH:], (0, LORA_PAD - n_lora)).reshape(1, -1)
    zpad = lambda a, lo_, hi_: jnp.pad(a, ((lo_, LORA_PAD - hi_), (0, 0)))
    lora_w = jnp.concatenate([
        zpad(p["rwkv_w2"], 0, DECAY_LORA),
        zpad(p["rwkv_a2"], DECAY_LORA, DECAY_LORA + ICLR_LORA),
        zpad(p["rwkv_g2"], DECAY_LORA + ICLR_LORA, n_lora)], axis=1).astype(BF16)
    ws = p["gmlp_ws"]
    wsp = jnp.concatenate([ws[0::2], ws[1::2]], axis=2)
    bsx = jnp.repeat(p["gmlp_bs"].T, HEAD_DIM, axis=1)
    row = lambda a: a.reshape(1, -1)
    full = lambda a: pl.BlockSpec(a.shape, lambda b, s: (0,) * a.ndim)
    args = [mod3, row(p["norm1_g"]), wuv, wrkv, wl, row(p["gmlp_ln_w"]), row(p["gmlp_ln_b"]), wsp, bsx,
            mu_rkv, mu_l, lora_w, row(p["rwkv_w0"]), row(p["rwkv_a0"]), row(p["rwkv_k_k"]),
            row(p["rwkv_k_a"]), _head_ones(RWKV_WIDTH)]
    in_specs = [pl.BlockSpec((1, tm, d), lambda b, s: (b, s, 0)),
                pl.BlockSpec((1,) + mod3.shape[1:], lambda b, s: (b, 0, 0))]
    in_specs += [full(a) for a in args[1:]]
    tok = lambda dt: jax.ShapeDtypeStruct((bsz, seq, RWKV_WIDTH), dt)
    out_spec = pl.BlockSpec((1, tm, RWKV_WIDTH), lambda b, s: (b, s, 0))
    outs = [tok(BF16)] * 7 + [tok(F32)]
    return pl.pallas_call(
        _pre_kernel,
        grid=(bsz, seq // tm),
        in_specs=in_specs,
        out_specs=[out_spec] * 8,
        out_shape=outs,
        scratch_shapes=[pltpu.VMEM((8, 3 * RWKV_WIDTH), F32), pltpu.VMEM((8, LORA_PAD), F32)],
        compiler_params=pltpu.CompilerParams(dimension_semantics=("arbitrary", "arbitrary"),
                                             vmem_limit_bytes=VMEM_LIMIT),
        name="pre_mix",
    )(x, *args)


def _blockdiag(xb, mask):
    t = jnp.concatenate([xb] * HEADS_PER_GROUP, axis=0)
    return jnp.where(mask, t, jnp.zeros((), xb.dtype))


def _diag_blocks(full, lane_head):
    out = jnp.zeros((HEAD_DIM, MXU_DIM), F32)
    for hh in range(HEADS_PER_GROUP):
        out = out + jnp.where(lane_head == hh, full[hh * HEAD_DIM:(hh + 1) * HEAD_DIM, :], 0.0)
    return out


def _scan_kernel(x_ref, mod_ref, ya_ref, r_ref, k2_ref, v_ref, kk_ref, al_ref, g_ref, lw_ref,
                 rk_ref, gnw_ref, gnb_ref, ones_ref, wout_ref, g2n_ref, wr_ref, br_ref,
                 x1_ref, h2_ref, route_ref, slab_ref, state_ref, y_scr):
    tm = x_ref.shape[1]
    L = SCAN_L
    n_chunks = tm // L
    n_groups = RWKV_WIDTH // MXU_DIM

    @pl.when(pl.program_id(1) == 0)
    def _():
        state_ref[...] = jnp.zeros_like(state_ref)

    r = r_ref[0].astype(F32)
    k2 = k2_ref[0].astype(F32)
    v = v_ref[0].astype(F32)
    kk = kk_ref[0].astype(F32)
    b = kk * al_ref[0].astype(F32)
    lw = lw_ref[0]
    lw_hi = lw.astype(BF16)
    lw_lo = (lw - lw_hi.astype(F32)).astype(BF16)

    trow = lax.broadcasted_iota(I32, (L, L), 0)
    tcol = lax.broadcasted_iota(I32, (L, L), 1)
    tri = (tcol <= trow).astype(BF16)
    row = lax.broadcasted_iota(I32, (L, MXU_DIM), 0)
    lane = lax.broadcasted_iota(I32, (L, MXU_DIM), 1)
    lane_j = lane % HEAD_DIM
    lane_head = lane // HEAD_DIM
    strict = lane_j < row
    incl = lane_j <= row
    eye = lane_j == row
    brow = lax.broadcasted_iota(I32, (MXU_DIM, MXU_DIM), 0) // HEAD_DIM
    bcol = lax.broadcasted_iota(I32, (MXU_DIM, MXU_DIM), 1) // HEAD_DIM
    bmask = brow == bcol

    def bd(xf):
        return _blockdiag(xf.astype(BF16), bmask)

    prep = {}
    for c in range(n_chunks):
        rows = slice(c * L, (c + 1) * L)
        lg = _dot(tri, lw_hi[rows]) + _dot(tri, lw_lo[rows])
        for g in range(n_groups):
            gs = slice(g * MXU_DIM, (g + 1) * MXU_DIM)
            lgc = lg[:, gs]
            glast = lgc[L - 1:L, :]
            eg = jnp.exp(lgc)
            egx = jnp.exp(lgc - lw[rows, gs])
            einv = jnp.exp(-lgc)
            ehat = jnp.exp(glast - lgc)
            at = -kk[rows, gs] * egx
            rt = r[rows, gs] * eg
            bt = b[rows, gs] * einv
            kt = k2[rows, gs] * einv
            bh = b[rows, gs] * ehat
            kh = k2[rows, gs] * ehat
            vc = v[rows, gs]
            lhs = jnp.concatenate([at, rt], axis=0).astype(BF16)
            pb = _dot_nt(lhs, bd(bt))
            pk = _dot_nt(lhs, bd(kt))
            a_ab = jnp.where(strict, pb[:L], 0.0)
            a_rb = jnp.where(incl, pb[L:], 0.0)
            a_ak = jnp.where(strict, pk[:L], 0.0)
            a_rk = jnp.where(incl, pk[L:], 0.0)
            ssum = jnp.where(eye, 1.0, 0.0) + a_ab
            xb = a_ab.astype(BF16)
            xp = _dot(xb, _blockdiag(xb, bmask))
            m = 2
            while m < L // 2:
                xb = xp.astype(BF16)
                res = _dot(jnp.concatenate([xb, ssum.astype(BF16)], axis=0), _blockdiag(xb, bmask))
                xp = res[:L]
                ssum = ssum + res[L:]
                m *= 2
            tmat = ssum + _dot(ssum.astype(BF16), bd(xp))
            res = _dot(jnp.concatenate([a_ak, a_rk], axis=0).astype(BF16), bd(vc))
            akv = res[:L]
            arkv = res[L:]
            tb = tmat.astype(BF16)
            wm = _dot(tb, bd(at))
            u0 = _dot(tb, bd(akv))
            arb = a_rb.astype(BF16)
            qp = rt + _dot(arb, bd(wm))
            y0 = _dot(arb, bd(u0)) + arkv
            bht = bh.T
            kht = kh.T
            full_m = _dot(bht.astype(BF16), wm.astype(BF16))
            full_n = _dot(jnp.concatenate([bht, kht], axis=1).astype(BF16),
                          jnp.concatenate([u0, vc], axis=0).astype(BF16))
            mt = _diag_blocks(full_m, lane_head) + jnp.where(eye, jnp.exp(glast), 0.0)
            nt = _diag_blocks(full_n, lane_head)
            prep[(c, g)] = (qp, y0, mt, nt)

    for g in range(n_groups):
        gs = slice(g * MXU_DIM, (g + 1) * MXU_DIM)
        st = state_ref[g]
        for c in range(n_chunks):
            qp, y0, mt, nt = prep[(c, g)]
            res = _dot(jnp.concatenate([qp, mt], axis=0).astype(BF16), bd(st))
            y_scr[c * L:(c + 1) * L, gs] = res[:L] + y0
            st = res[L:] + nt
        state_ref[g] = st

    y = y_scr[...]
    ones = ones_ref[...]
    inv_n = 1.0 / HEAD_DIM
    mean = _dot(y.astype(BF16), ones) * inv_n
    dlt = y - mean
    var = _dot((dlt * dlt).astype(BF16), ones) * inv_n
    yn = dlt * lax.rsqrt(var + GN_EPS) * gnw_ref[...] + gnb_ref[...]
    bonus = _dot((r * k2 * rk_ref[...]).astype(BF16), ones) * v
    yb = (yn + bonus) * g_ref[0].astype(F32)
    cat = jnp.concatenate([ya_ref[0], yb.astype(BF16)], axis=1)
    y2 = _dot(cat, wout_ref[...])
    gate1 = mod_ref[0, 2:3, :]
    shift2 = mod_ref[0, 3:4, :]
    scale2 = mod_ref[0, 4:5, :]
    x1 = x_ref[0] + gate1 * y2
    x1_ref[0] = x1
    h2 = _rms(x1, g2n_ref[...]) * (1.0 + scale2) + shift2
    h2_ref[0] = h2

    logits = _dot(h2.astype(BF16), wr_ref[...]) + br_ref[...]
    ln = lax.broadcasted_iota(I32, logits.shape, 1)
    neg = jnp.float32(-jnp.inf)
    is_g = ln < N_GROUPS
    gl = jnp.where(is_g, logits, neg)
    gmax = jnp.max(gl, axis=-1, keepdims=True)
    gidx = jnp.min(jnp.where(gl == gmax, ln, LANES), axis=-1, keepdims=True)
    g_p = 1.0 / jnp.sum(jnp.where(is_g, jnp.exp(logits - gmax), 0.0), axis=-1, keepdims=True)
    egrp = (ln - N_GROUPS) // EXPERTS_PER_GROUP
    sel = jnp.where(ln >= N_GROUPS, egrp, -1) == gidx
    el = jnp.where(sel, logits, neg)
    l1 = jnp.max(el, axis=-1, keepdims=True)
    i1 = jnp.min(jnp.where(el == l1, ln, LANES), axis=-1, keepdims=True)
    el2 = jnp.where(ln == i1, neg, el)
    l2 = jnp.max(el2, axis=-1, keepdims=True)
    i2 = jnp.min(jnp.where(el2 == l2, ln, LANES), axis=-1, keepdims=True)
    t = jnp.exp(l2 - l1)
    p1 = 1.0 / (1.0 + t)
    w1 = g_p * p1
    w2 = g_p * (t * p1)
    e1 = (i1 - N_GROUPS).astype(F32)
    e2 = (i2 - N_GROUPS).astype(F32)
    slab = jnp.where(ln == 0, e1, jnp.where(ln == 1, e2, jnp.where(ln == 2, w1, jnp.where(ln == 3, w2, 0.0))))
    slab_ref[...] = slab
    route_ref[...] = slab.T[:ROUTE_ROWS, :]


def _scan_call(x, mod3, pre_outs, p):
    bsz, seq, d = x.shape
    tm = min(SCAN_TM, seq)
    n_tok = bsz * seq
    steps = seq // tm
    n_route = N_GROUPS + N_EXPERTS
    wr = jnp.pad(jnp.concatenate([p["router_group_w"], p["router_expert_w"]], axis=1),
                 ((0, 0), (0, LANES - n_route))).astype(BF16)
    br = jnp.pad(jnp.concatenate([p["router_group_b"], p["router_expert_b"]]), (0, LANES - n_route)).reshape(1, -1)
    row = lambda a: a.reshape(1, -1)
    full = lambda a: pl.BlockSpec(a.shape, lambda b, s: (0,) * a.ndim)
    params = [row(p["rwkv_r_k"]), row(p["rwkv_gn_w"]), row(p["rwkv_gn_b"]), _head_ones(RWKV_WIDTH),
              p["w_out"].astype(BF16), row(p["norm2_g"]), wr, br]
    tok_spec = pl.BlockSpec((1, tm, RWKV_WIDTH), lambda b, s: (b, s, 0))
    x_spec = pl.BlockSpec((1, tm, d), lambda b, s: (b, s, 0))
    in_specs = [x_spec, pl.BlockSpec((1,) + mod3.shape[1:], lambda b, s: (b, 0, 0))]
    in_specs += [tok_spec] * 8 + [full(a) for a in params]
    return pl.pallas_call(
        _scan_kernel,
        grid=(bsz, steps),
        in_specs=in_specs,
        out_specs=[x_spec, x_spec, pl.BlockSpec((ROUTE_ROWS, tm), lambda b, s: (0, b * steps + s)),
                   pl.BlockSpec((tm, LANES), lambda b, s: (b * steps + s, 0))],
        out_shape=[jax.ShapeDtypeStruct((bsz, seq, d), F32), jax.ShapeDtypeStruct((bsz, seq, d), F32),
                   jax.ShapeDtypeStruct((ROUTE_ROWS, n_tok), F32), jax.ShapeDtypeStruct((n_tok, LANES), F32)],
        scratch_shapes=[pltpu.VMEM((RWKV_WIDTH // MXU_DIM, HEAD_DIM, MXU_DIM), F32),
                        pltpu.VMEM((tm, RWKV_WIDTH), F32)],
        compiler_params=pltpu.CompilerParams(dimension_semantics=("arbitrary", "arbitrary"),
                                             vmem_limit_bytes=VMEM_LIMIT),
        name="scan_post",
    )(x, mod3, *pre_outs, *params)


def _onehots(route_ref):
    t = route_ref.shape[1]
    erow = lax.broadcasted_iota(I32, (N_EXPERTS, t), 0)
    oh1 = erow == route_ref[0:1, :].astype(I32)
    oh2 = erow == route_ref[1:2, :].astype(I32)
    return oh1, oh2


def _count_kernel(route_ref, cnt_ref):
    @pl.when(pl.program_id(0) == 0)
    def _():
        cnt_ref[...] = jnp.zeros_like(cnt_ref)

    oh1, oh2 = _onehots(route_ref)
    cnt = oh1.astype(F32) + oh2.astype(F32)
    cnt_ref[...] += jnp.sum(cnt, axis=1, keepdims=True)


def _dest_kernel(route_ref, cnt_ref, dest_ref, meta_ref, base_ref):
    t = route_ref.shape[1]
    nbp = meta_ref.shape[1]

    @pl.when(pl.program_id(0) == 0)
    def _():
        cnt = cnt_ref[...]
        padded = jnp.floor((cnt + (MOE_BM - 1)) * (1.0 / MOE_BM)) * MOE_BM
        er = lax.broadcasted_iota(I32, (N_EXPERTS, N_EXPERTS), 0)
        ec = lax.broadcasted_iota(I32, (N_EXPERTS, N_EXPERTS), 1)
        lower = (ec < er).astype(BF16)
        p_hi = padded.astype(BF16)
        p_lo = (padded - p_hi.astype(F32)).astype(BF16)
        pstart = _dot(lower, p_hi) + _dot(lower, p_lo)
        base_ref[...] = pstart
        pend = (pstart + padded)[:, 0:1]
        blk0 = (lax.broadcasted_iota(I32, (N_EXPERTS, nbp), 1) * MOE_BM).astype(F32)
        be = jnp.sum((pend <= blk0).astype(I32), axis=0, keepdims=True)
        meta_ref[0:1, :] = jnp.minimum(be, N_EXPERTS - 1)
        used = (pend[N_EXPERTS - 1:N_EXPERTS, :] * (1.0 / MOE_BM)).astype(I32)
        meta_ref[1:2, :] = jnp.broadcast_to(used, (1, nbp))
        meta_ref[2:, :] = jnp.zeros((meta_ref.shape[0] - 2, nbp), I32)

    oh1, oh2 = _onehots(route_ref)
    cnt = oh1.astype(F32) + oh2.astype(F32)
    sr = lax.broadcasted_iota(I32, (t, t), 0)
    sc = lax.broadcasted_iota(I32, (t, t), 1)
    upper = (sr < sc).astype(BF16)
    pos = _dot(cnt.astype(BF16), upper) + base_ref[:, 0:1]
    dest_ref[0:1, :] = jnp.sum(jnp.where(oh1, pos, 0.0), axis=0, keepdims=True).astype(I32)
    dest_ref[1:2, :] = jnp.sum(jnp.where(oh2, pos, 0.0), axis=0, keepdims=True).astype(I32)
    base_ref[...] += jnp.sum(cnt, axis=1, keepdims=True)


def _n_blocks(n_tok):
    return -(-2 * n_tok // MOE_BM) + N_EXPERTS


def _route_call(route):
    n_tok = route.shape[1]
    t = min(ROUTE_T, n_tok)
    nbp = -(-_n_blocks(n_tok) // LANES) * LANES
    cparams = pltpu.CompilerParams(dimension_semantics=("arbitrary",), vmem_limit_bytes=VMEM_LIMIT)
    route_spec = pl.BlockSpec((ROUTE_ROWS, t), lambda j: (0, j))
    cnt_spec = pl.BlockSpec((N_EXPERTS, LANES), lambda j: (0, 0))
    counts = pl.pallas_call(
        _count_kernel, grid=(n_tok // t,), in_specs=[route_spec], out_specs=cnt_spec,
        out_shape=jax.ShapeDtypeStruct((N_EXPERTS, LANES), F32), compiler_params=cparams,
        name="route_count")(route)
    dest, meta = pl.pallas_call(
        _dest_kernel, grid=(n_tok // t,), in_specs=[route_spec, cnt_spec],
        out_specs=[pl.BlockSpec((2, t), lambda j: (0, j)), pl.BlockSpec((8, nbp), lambda j: (0, 0))],
        out_shape=[jax.ShapeDtypeStruct((2, n_tok), I32), jax.ShapeDtypeStruct((8, nbp), I32)],
        scratch_shapes=[pltpu.VMEM((N_EXPERTS, LANES), F32)], compiler_params=cparams,
        name="route_dest")(route, counts)
    return dest, meta


def _dispatch_kernel(dest_ref, h_hbm, xs_in, xs_hbm, sem):
    del xs_in
    n_tok = h_hbm.shape[0]
    t = DISPATCH_T if n_tok >= DISPATCH_T else n_tok
    base = pl.program_id(0) * t

    def copies(i):
        tok = base + i
        src = h_hbm.at[pl.ds(tok, 1), :]
        return (pltpu.make_async_copy(src, xs_hbm.at[pl.ds(dest_ref[tok], 1), :], sem),
                pltpu.make_async_copy(src, xs_hbm.at[pl.ds(dest_ref[n_tok + tok], 1), :], sem))

    def start(i, carry):
        for cp in copies(i):
            cp.start()
        return carry

    def wait(i, carry):
        for cp in copies(i):
            cp.wait()
        return carry

    lax.fori_loop(0, t, start, 0)
    lax.fori_loop(0, t, wait, 0)


def _dispatch_call(dest_flat, h_tok, n_slots):
    n_tok, d = h_tok.shape
    t = min(DISPATCH_T, n_tok)
    xs0 = jnp.zeros((n_slots, d), h_tok.dtype)
    return pl.pallas_call(
        _dispatch_kernel,
        grid_spec=pltpu.PrefetchScalarGridSpec(
            num_scalar_prefetch=1, grid=(n_tok // t,),
            in_specs=[pl.BlockSpec(memory_space=pl.ANY), pl.BlockSpec(memory_space=pl.ANY)],
            out_specs=pl.BlockSpec(memory_space=pl.ANY),
            scratch_shapes=[pltpu.SemaphoreType.DMA]),
        out_shape=jax.ShapeDtypeStruct((n_slots, d), h_tok.dtype),
        input_output_aliases={2: 0},
        compiler_params=pltpu.CompilerParams(dimension_semantics=("arbitrary",)),
        name="moe_dispatch",
    )(dest_flat, h_tok, xs0)


def _expert_kernel(meta_ref, xs_ref, wg_ref, wu_ref, wd_ref, yb_ref):
    j = pl.program_id(0)
    nbp = meta_ref.shape[0] // 8

    @pl.when(j < meta_ref[nbp])
    def _():
        x = xs_ref[...].astype(BF16)
        gate = _dot(x, wg_ref[0].astype(BF16))
        up = _dot(x, wu_ref[0].astype(BF16))
        hid = (jax.nn.silu(gate) * up).astype(BF16)
        yb_ref[...] = _dot(hid, wd_ref[0].astype(BF16))

    @pl.when(j >= meta_ref[nbp])
    def _():
        yb_ref[...] = jnp.zeros_like(yb_ref)


def _expert_call(meta_flat, xs, wg, wu, wd):
    n_slots, d = xs.shape
    hid = wg.shape[2]
    return pl.pallas_call(
        _expert_kernel,
        grid_spec=pltpu.PrefetchScalarGridSpec(
            num_scalar_prefetch=1, grid=(n_slots // MOE_BM,),
            in_specs=[pl.BlockSpec((MOE_BM, d), lambda j, m: (j, 0)),
                      pl.BlockSpec((1, d, hid), lambda j, m: (m[j], 0, 0)),
                      pl.BlockSpec((1, d, hid), lambda j, m: (m[j], 0, 0)),
                      pl.BlockSpec((1, hid, d), lambda j, m: (m[j], 0, 0))],
            out_specs=pl.BlockSpec((MOE_BM, d), lambda j, m: (j, 0))),
        out_shape=jax.ShapeDtypeStruct((n_slots, d), F32),
        compiler_params=pltpu.CompilerParams(dimension_semantics=("arbitrary",),
                                             vmem_limit_bytes=VMEM_LIMIT),
        name="moe_experts",
    )(meta_flat, xs, wg, wu, wd)


def _combine_kernel(dest_ref, yb_hbm, slab_ref, x1_ref, mod_ref, gf_ref, out_ref, gbuf, sem):
    t = x1_ref.shape[0]
    n_tok = dest_ref.shape[0] // 2
    base = pl.program_id(0) * t

    def copies(i):
        tok = base + i
        return (pltpu.make_async_copy(yb_hbm.at[pl.ds(dest_ref[tok], 1), :], gbuf.at[0, pl.ds(i, 1), :], sem),
                pltpu.make_async_copy(yb_hbm.at[pl.ds(dest_ref[n_tok + tok], 1), :], gbuf.at[1, pl.ds(i, 1), :], sem))

    def start(i, carry):
        for cp in copies(i):
            cp.start()
        return carry

    def wait(i, carry):
        for cp in copies(i):
            cp.wait()
        return carry

    lax.fori_loop(0, t, start, 0)
    lax.fori_loop(0, t, wait, 0)
    slab = slab_ref[...]
    y = slab[:, 2:3] * gbuf[0] + slab[:, 3:4] * gbuf[1]
    gate2 = mod_ref[0, 5:6, :]
    x2 = x1_ref[...] + gate2 * y
    out_ref[...] = _rms(x2, gf_ref[...])


def _combine_call(dest_flat, yb, slab, x1_tok, mod3, gf, seq):
    n_tok, d = x1_tok.shape
    t = min(DISPATCH_T, seq)
    return pl.pallas_call(
        _combine_kernel,
        grid_spec=pltpu.PrefetchScalarGridSpec(
            num_scalar_prefetch=1, grid=(n_tok // t,),
            in_specs=[pl.BlockSpec(memory_space=pl.ANY),
                      pl.BlockSpec((t, LANES), lambda i, dr: (i, 0)),
                      pl.BlockSpec((t, d), lambda i, dr: (i, 0)),
                      pl.BlockSpec((1,) + mod3.shape[1:], lambda i, dr: ((i * t) // seq, 0, 0)),
                      pl.BlockSpec((1, d), lambda i, dr: (0, 0))],
            out_specs=pl.BlockSpec((t, d), lambda i, dr: (i, 0)),
            scratch_shapes=[pltpu.VMEM((2, t, d), F32), pltpu.SemaphoreType.DMA]),
        out_shape=jax.ShapeDtypeStruct((n_tok, d), F32),
        compiler_params=pltpu.CompilerParams(dimension_semantics=("arbitrary",),
                                             vmem_limit_bytes=VMEM_LIMIT),
        name="moe_combine",
    )(dest_flat, yb, slab, x1_tok, mod3, gf.reshape(1, d))


def kernel(x, c, ada_w, ada_b, norm1_g, w_in, gmlp_ln_w, gmlp_ln_b, gmlp_ws, gmlp_bs, rwkv_mu, rwkv_w0,
           rwkv_w2, rwkv_a0, rwkv_a2, rwkv_g2, rwkv_k_k, rwkv_k_a, rwkv_r_k, rwkv_gn_w, rwkv_gn_b, w_out,
           norm2_g, router_group_w, router_group_b, router_expert_w, router_expert_b, moe_w_gate,
           moe_w_up, moe_w_down, final_norm_g):
    bsz, seq, d = x.shape
    n_tok = bsz * seq
    assert ada_w.shape[0] == 1, "the final rmsnorm is fused into the (single) layer's combine kernel"
    for l in range(1):
        p = dict(norm1_g=norm1_g[l], w_in=w_in[l], gmlp_ln_w=gmlp_ln_w[l], gmlp_ln_b=gmlp_ln_b[l],
                 gmlp_ws=gmlp_ws[l], gmlp_bs=gmlp_bs[l], rwkv_mu=rwkv_mu[l], rwkv_w0=rwkv_w0[l],
                 rwkv_w2=rwkv_w2[l], rwkv_a0=rwkv_a0[l], rwkv_a2=rwkv_a2[l], rwkv_g2=rwkv_g2[l],
                 rwkv_k_k=rwkv_k_k[l], rwkv_k_a=rwkv_k_a[l], rwkv_r_k=rwkv_r_k[l],
                 rwkv_gn_w=rwkv_gn_w[l], rwkv_gn_b=rwkv_gn_b[l], w_out=w_out[l], norm2_g=norm2_g[l],
                 router_group_w=router_group_w[l], router_group_b=router_group_b[l],
                 router_expert_w=router_expert_w[l], router_expert_b=router_expert_b[l])
        mod3 = _mod_call(c, ada_w[l], ada_b[l]).reshape(bsz, 6, d)
        pre = _pre_call(x, mod3, p)
        x1, h2, route, slab = _scan_call(x, mod3, pre, p)
        dest, meta = _route_call(route)
        dest_flat = dest.reshape(-1)
        n_slots = _n_blocks(n_tok) * MOE_BM
        xs = _dispatch_call(dest_flat, h2.reshape(n_tok, d), n_slots)
        yb = _expert_call(meta.reshape(-1), xs, moe_w_gate[l], moe_w_up[l], moe_w_down[l])
        x = _combine_call(dest_flat, yb, slab, x1.reshape(n_tok, d), mod3, final_norm_g, seq).reshape(bsz, seq, d)
    return x
```

```python
import functools

import jax
import jax.numpy as jnp
from jax import lax
from jax.experimental import pallas as pl
from jax.experimental.pallas import tpu as pltpu

F32 = jnp.float32
BF16 = jnp.bfloat16
I32 = jnp.int32

HEAD_DIM = 64
GMLP_WIDTH = 512
RWKV_WIDTH = 512
GMLP_CHUNK = 128
DECAY_LORA = 32
ICLR_LORA = 32
GATE_LORA = 96
LORA_PAD = 256
N_GROUPS = 4
EXPERTS_PER_GROUP = 8
N_EXPERTS = 32
RMS_EPS = 1e-6
LN_EPS = 1e-5
GN_EPS = 64e-5

LANES = 128
MXU_DIM = 256
VMEM_LIMIT = 56 * 1024 * 1024

PRE_TM = 512
SCAN_TM = 256
SCAN_L = 64
HEADS_PER_GROUP = MXU_DIM // HEAD_DIM
ROUTE_T = 512
MOE_BM = 256
DISPATCH_T = 512
ROUTE_ROWS = 8


def _dot(a, b):
    return jnp.dot(a, b, preferred_element_type=F32)


def _dot_nt(a, b):
    return lax.dot_general(a, b, (((1,), (1,)), ((), ())), preferred_element_type=F32)


def _rms(x, g):
    return x * lax.rsqrt(jnp.mean(x * x, axis=-1, keepdims=True) + RMS_EPS) * g


def _mod_kernel(c_ref, w_ref, b_ref, o_ref):
    c = c_ref[...]
    w = w_ref[...]
    c_hi = c.astype(BF16)
    c_lo = (c - c_hi.astype(F32)).astype(BF16)
    w_hi = w.astype(BF16)
    w_lo = (w - w_hi.astype(F32)).astype(BF16)
    o_ref[...] = _dot(c_hi, w_hi) + _dot(c_lo, w_hi) + _dot(c_hi, w_lo) + b_ref[...]


def _mod_call(c, ada_w, ada_b):
    bsz, d = c.shape
    n = ada_w.shape[1]
    tn = 1024
    return pl.pallas_call(
        _mod_kernel,
        grid=(n // tn,),
        in_specs=[pl.BlockSpec((bsz, d), lambda j: (0, 0)),
                  pl.BlockSpec((d, tn), lambda j: (0, j)),
                  pl.BlockSpec((1, tn), lambda j: (0, j))],
        out_specs=pl.BlockSpec((bsz, tn), lambda j: (0, j)),
        out_shape=jax.ShapeDtypeStruct((bsz, n), F32),
        compiler_params=pltpu.CompilerParams(dimension_semantics=("arbitrary",),
                                             vmem_limit_bytes=VMEM_LIMIT),
        name="adaln_mod",
    )(c, ada_w, ada_b.reshape(1, n))


def _token_shift(z, carry_ref):
    tm = z.shape[0]
    prev = pltpu.roll(z, 1, axis=0)
    row = lax.broadcasted_iota(I32, z.shape, 0)
    prev = jnp.where(row == 0, carry_ref[0:1, :], prev)
    carry_ref[0:1, :] = z[tm - 1:tm, :]
    return prev


def _pre_kernel(x_ref, mod_ref, g1_ref, wuv_ref, wrkv_ref, wl_ref, lnw_ref, lnb_ref, wsp_ref,
                bsx_ref, mu_rkv_ref, mu_l_ref, lora_w_ref, w0_ref, a0_ref, kk_ref, ka_ref, ones_ref,
                ya_ref, r_ref, k2_ref, v_ref, kkn_ref, al_ref, g_ref, lw_ref,
                carry_rkv, carry_l):
    tm = x_ref.shape[1]

    @pl.when(pl.program_id(1) == 0)
    def _():
        carry_rkv[...] = jnp.zeros_like(carry_rkv)
        carry_l[...] = jnp.zeros_like(carry_l)

    x = x_ref[0]
    shift1 = mod_ref[0, 0:1, :]
    scale1 = mod_ref[0, 1:2, :]
    h = _rms(x, g1_ref[...]) * (1.0 + scale1) + shift1
    hb = h.astype(BF16)
    zuv = _dot(hb, wuv_ref[...])
    zr = _dot(hb, wrkv_ref[...])
    zl = _dot(hb, wl_ref[...])

    u = jax.nn.gelu(zuv[:, :GMLP_WIDTH])
    v = jax.nn.gelu(zuv[:, GMLP_WIDTH:])
    m = jnp.mean(v, axis=-1, keepdims=True)
    var = jnp.mean(jnp.square(v - m), axis=-1, keepdims=True)
    vn = ((v - m) * lax.rsqrt(var + LN_EPS) * lnw_ref[...] + lnb_ref[...]).astype(BF16)
    lane = lax.broadcasted_iota(I32, (GMLP_CHUNK, LANES), 1)
    left = lane < HEAD_DIM
    trow = lax.broadcasted_iota(I32, (GMLP_CHUNK, 2 * GMLP_CHUNK), 0)
    tcol = lax.broadcasted_iota(I32, (GMLP_CHUNK, 2 * GMLP_CHUNK), 1) % GMLP_CHUNK
    causal = tcol <= trow
    zero = jnp.zeros((), BF16)
    for p in range(GMLP_WIDTH // LANES):
        wpair = jnp.where(causal, wsp_ref[p], 0.0).astype(BF16)
        for c in range(tm // GMLP_CHUNK):
            rows = slice(c * GMLP_CHUNK, (c + 1) * GMLP_CHUNK)
            cols = slice(p * LANES, (p + 1) * LANES)
            vc = vn[rows, cols]
            rhs = jnp.concatenate([jnp.where(left, vc, zero), jnp.where(left, zero, vc)], axis=0)
            mixed = _dot(wpair, rhs) + bsx_ref[:, cols]
            ya_ref[0, rows, cols] = (u[rows, cols] * mixed).astype(ya_ref.dtype)

    zr = zr + (_token_shift(zr, carry_rkv) - zr) * mu_rkv_ref[...]
    zl = zl + (_token_shift(zl, carry_l) - zl) * mu_l_ref[...]
    r = zr[:, :RWKV_WIDTH]
    k = zr[:, RWKV_WIDTH:2 * RWKV_WIDTH]
    vv = zr[:, 2 * RWKV_WIDTH:]
    llane = lax.broadcasted_iota(I32, zl.shape, 1)
    f = jnp.where(llane < DECAY_LORA, jnp.tanh(zl),
                  jnp.where(llane < DECAY_LORA + ICLR_LORA, zl, jax.nn.sigmoid(zl)))
    lo = _dot(f.astype(BF16), lora_w_ref[...])
    wpre = w0_ref[...] + lo[:, :RWKV_WIDTH]
    w_log = -(jnp.maximum(-wpre, 0.0) + jnp.log1p(jnp.exp(-jnp.abs(wpre)))) - 0.5
    lw_ref[0] = -jnp.exp(w_log)
    alpha = jax.nn.sigmoid(a0_ref[...] + lo[:, RWKV_WIDTH:2 * RWKV_WIDTH])
    g_ref[0] = lo[:, 2 * RWKV_WIDTH:].astype(g_ref.dtype)
    kkr = k * kk_ref[...]
    ss = _dot((kkr * kkr).astype(BF16), ones_ref[...])
    kkn_ref[0] = (kkr / jnp.maximum(jnp.sqrt(ss), 1e-12)).astype(kkn_ref.dtype)
    k2_ref[0] = (k * (1.0 + (alpha - 1.0) * ka_ref[...])).astype(k2_ref.dtype)
    r_ref[0] = r.astype(r_ref.dtype)
    v_ref[0] = vv.astype(v_ref.dtype)
    al_ref[0] = alpha.astype(al_ref.dtype)


def _head_ones(width):
    hid = jnp.arange(width) // HEAD_DIM
    return (hid[:, None] == hid[None, :]).astype(BF16)


def _pre_call(x, mod3, p):
    bsz, seq, d = x.shape
    tm = min(PRE_TM, seq)
    w_in = p["w_in"]
    wuv = w_in[:, :2 * GMLP_WIDTH].astype(BF16)
    wrkv = w_in[:, 2 * GMLP_WIDTH:2 * GMLP_WIDTH + 3 * RWKV_WIDTH].astype(BF16)
    n_lora = DECAY_LORA + ICLR_LORA + GATE_LORA
    wl = jnp.pad(w_in[:, 2 * GMLP_WIDTH + 3 * RWKV_WIDTH:], ((0, 0), (0, LORA_PAD - n_lora))).astype(BF16)
    mu = p["rwkv_mu"]
    mu_rkv = mu[:3 * RWKV_WIDTH].reshape(1, -1)
    mu_l = jnp.pad(mu[3 * RWKV_WIDTH:], (0, LORA_PAD - n_lora)).reshape(1, -1)
    zpad = lambda a, lo_, hi_: jnp.pad(a, ((lo_, LORA_PAD - hi_), (0, 0)))
    lora_w = jnp.concatenate([
        zpad(p["rwkv_w2"], 0, DECAY_LORA),
        zpad(p["rwkv_a2"], DECAY_LORA, DECAY_LORA + ICLR_LORA),
        zpad(p["rwkv_g2"], DECAY_LORA + ICLR_LORA, n_lora)], axis=1).astype(BF16)
    ws = p["gmlp_ws"]
    wsp = jnp.concatenate([ws[0::2], ws[1::2]], axis=2)
    bsx = jnp.repeat(p["gmlp_bs"].T, HEAD_DIM, axis=1)
    row = lambda a: a.reshape(1, -1)
    full = lambda a: pl.BlockSpec(a.shape, lambda b, s: (0,) * a.ndim)
    args = [mod3, row(p["norm1_g"]), wuv, wrkv, wl, row(p["gmlp_ln_w"]), row(p["gmlp_ln_b"]), wsp, bsx,
            mu_rkv, mu_l, lora_w, row(p["rwkv_w0"]), row(p["rwkv_a0"]), row(p["rwkv_k_k"]),
            row(p["rwkv_k_a"]), _head_ones(RWKV_WIDTH)]
    in_specs = [pl.BlockSpec((1, tm, d), lambda b, s: (b, s, 0)),
                pl.BlockSpec((1,) + mod3.shape[1:], lambda b, s: (b, 0, 0))]
    in_specs += [full(a) for a in args[1:]]
    tok = lambda dt: jax.ShapeDtypeStruct((bsz, seq, RWKV_WIDTH), dt)
    out_spec = pl.BlockSpec((1, tm, RWKV_WIDTH), lambda b, s: (b, s, 0))
    outs = [tok(BF16)] * 7 + [tok(F32)]
    return pl.pallas_call(
        _pre_kernel,
        grid=(bsz, seq // tm),
        in_specs=in_specs,
        out_specs=[out_spec] * 8,
        out_shape=outs,
        scratch_shapes=[pltpu.VMEM((8, 3 * RWKV_WIDTH), F32), pltpu.VMEM((8, LORA_PAD), F32)],
        compiler_params=pltpu.CompilerParams(dimension_semantics=("arbitrary", "arbitrary"),
                                             vmem_limit_bytes=VMEM_LIMIT),
        name="pre_mix",
    )(x, *args)


def _blockdiag(xb, mask):
    t = jnp.concatenate([xb] * HEADS_PER_GROUP, axis=0)
    return jnp.where(mask, t, jnp.zeros((), xb.dtype))


def _diag_blocks(full, lane_head):
    out = jnp.zeros((HEAD_DIM, MXU_DIM), F32)
    for hh in range(HEADS_PER_GROUP):
        out = out + jnp.where(lane_head == hh, full[hh * HEAD_DIM:(hh + 1) * HEAD_DIM, :], 0.0)
    return out


def _scan_kernel(x_ref, mod_ref, ya_ref, r_ref, k2_ref, v_ref, kk_ref, al_ref, g_ref, lw_ref,
                 rk_ref, gnw_ref, gnb_ref, ones_ref, wout_ref, g2n_ref, wr_ref, br_ref,
                 x1_ref, h2_ref, route_ref, slab_ref, state_ref, y_scr):
    tm = x_ref.shape[1]
    L = SCAN_L
    n_chunks = tm // L
    n_groups = RWKV_WIDTH // MXU_DIM

    @pl.when(pl.program_id(1) == 0)
    def _():
        state_ref[...] = jnp.zeros_like(state_ref)

    r = r_ref[0].astype(F32)
    k2 = k2_ref[0].astype(F32)
    v = v_ref[0].astype(F32)
    kk = kk_ref[0].astype(F32)
    b = kk * al_ref[0].astype(F32)
    lw = lw_ref[0]
    lw_hi = lw.astype(BF16)
    lw_lo = (lw - lw_hi.astype(F32)).astype(BF16)

    trow = lax.broadcasted_iota(I32, (L, L), 0)
    tcol = lax.broadcasted_iota(I32, (L, L), 1)
    tri = (tcol <= trow).astype(BF16)
    row = lax.broadcasted_iota(I32, (L, MXU_DIM), 0)
    lane = lax.broadcasted_iota(I32, (L, MXU_DIM), 1)
    lane_j = lane % HEAD_DIM
    lane_head = lane // HEAD_DIM
    strict = lane_j < row
    incl = lane_j <= row
    eye = lane_j == row
    brow = lax.broadcasted_iota(I32, (MXU_DIM, MXU_DIM), 0) // HEAD_DIM
    bcol = lax.broadcasted_iota(I32, (MXU_DIM, MXU_DIM), 1) // HEAD_DIM
    bmask = brow == bcol

    def bd(xf):
        return _blockdiag(xf.astype(BF16), bmask)

    probs = [(c, g) for c in range(n_chunks) for g in range(n_groups)]
    each = lambda fn, *lists: [fn(*a) for a in zip(*lists)]
    lgs = [_dot(tri, lw_hi[c * L:(c + 1) * L]) + _dot(tri, lw_lo[c * L:(c + 1) * L])
           for c in range(n_chunks)]

    def sl(arr, c, g):
        return arr[c * L:(c + 1) * L, g * MXU_DIM:(g + 1) * MXU_DIM]

    lgc = [lgs[c][:, g * MXU_DIM:(g + 1) * MXU_DIM] for c, g in probs]
    glast = [x[L - 1:L, :] for x in lgc]
    eg = [jnp.exp(x) for x in lgc]
    egx = [jnp.exp(x - sl(lw, c, g)) for x, (c, g) in zip(lgc, probs)]
    einv = [jnp.exp(-x) for x in lgc]
    ehat = each(lambda gl, x: jnp.exp(gl - x), glast, lgc)
    at = [-sl(kk, c, g) * e for e, (c, g) in zip(egx, probs)]
    rt = [sl(r, c, g) * e for e, (c, g) in zip(eg, probs)]
    bt = [sl(b, c, g) * e for e, (c, g) in zip(einv, probs)]
    kt = [sl(k2, c, g) * e for e, (c, g) in zip(einv, probs)]
    bh = [sl(b, c, g) * e for e, (c, g) in zip(ehat, probs)]
    kh = [sl(k2, c, g) * e for e, (c, g) in zip(ehat, probs)]
    vc = [sl(v, c, g) for c, g in probs]
    lhs = each(lambda a_, r_: jnp.concatenate([a_, r_], axis=0).astype(BF16), at, rt)
    pb = each(lambda l_, x: _dot_nt(l_, bd(x)), lhs, bt)
    pk = each(lambda l_, x: _dot_nt(l_, bd(x)), lhs, kt)
    a_ab = [jnp.where(strict, x[:L], 0.0) for x in pb]
    a_rb = [jnp.where(incl, x[L:], 0.0) for x in pb]
    a_ak = [jnp.where(strict, x[:L], 0.0) for x in pk]
    a_rk = [jnp.where(incl, x[L:], 0.0) for x in pk]
    ssum = [jnp.where(eye, 1.0, 0.0) + x for x in a_ab]
    xb = [x.astype(BF16) for x in a_ab]
    xp = [_dot(x, _blockdiag(x, bmask)) for x in xb]
    m = 2
    while m < L // 2:
        xb = [x.astype(BF16) for x in xp]
        res = each(lambda x, s_: _dot(jnp.concatenate([x, s_.astype(BF16)], axis=0), _blockdiag(x, bmask)),
                   xb, ssum)
        xp = [x[:L] for x in res]
        ssum = each(lambda s_, x: s_ + x[L:], ssum, res)
        m *= 2
    tmat = each(lambda s_, x: s_ + _dot(s_.astype(BF16), bd(x)), ssum, xp)
    res = each(lambda a1, a2, x: _dot(jnp.concatenate([a1, a2], axis=0).astype(BF16), bd(x)), a_ak, a_rk, vc)
    akv = [x[:L] for x in res]
    arkv = [x[L:] for x in res]
    tb = [x.astype(BF16) for x in tmat]
    wm = each(lambda t_, x: _dot(t_, bd(x)), tb, at)
    u0 = each(lambda t_, x: _dot(t_, bd(x)), tb, akv)
    arb = [x.astype(BF16) for x in a_rb]
    qp = each(lambda r_, a_, x: r_ + _dot(a_, bd(x)), rt, arb, wm)
    y0 = each(lambda a_, x, z: _dot(a_, bd(x)) + z, arb, u0, arkv)
    bht = [x.T for x in bh]
    kht = [x.T for x in kh]
    full_m = each(lambda b_, w_: _dot(b_.astype(BF16), w_.astype(BF16)), bht, wm)
    full_n = each(lambda b_, k_, u_, v_: _dot(jnp.concatenate([b_, k_], axis=1).astype(BF16),
                                              jnp.concatenate([u_, v_], axis=0).astype(BF16)),
                  bht, kht, u0, vc)
    mt = each(lambda f_, gl: _diag_blocks(f_, lane_head) + jnp.where(eye, jnp.exp(gl), 0.0), full_m, glast)
    nt = [_diag_blocks(f_, lane_head) for f_ in full_n]
    lhs2 = each(lambda q_, m_: jnp.concatenate([q_, m_], axis=0).astype(BF16), qp, mt)

    st = [state_ref[g] for g in range(n_groups)]
    for c in range(n_chunks):
        for g in range(n_groups):
            i = c * n_groups + g
            res_s = _dot(lhs2[i], bd(st[g]))
            y_scr[c * L:(c + 1) * L, g * MXU_DIM:(g + 1) * MXU_DIM] = res_s[:L] + y0[i]
            st[g] = res_s[L:] + nt[i]
    for g in range(n_groups):
        state_ref[g] = st[g]

    y = y_scr[...]
    ones = ones_ref[...]
    inv_n = 1.0 / HEAD_DIM
    mean = _dot(y.astype(BF16), ones) * inv_n
    dlt = y - mean
    var = _dot((dlt * dlt).astype(BF16), ones) * inv_n
    yn = dlt * lax.rsqrt(var + GN_EPS) * gnw_ref[...] + gnb_ref[...]
    bonus = _dot((r * k2 * rk_ref[...]).astype(BF16), ones) * v
    yb = (yn + bonus) * g_ref[0].astype(F32)
    cat = jnp.concatenate([ya_ref[0], yb.astype(BF16)], axis=1)
    y2 = _dot(cat, wout_ref[...])
    gate1 = mod_ref[0, 2:3, :]
    shift2 = mod_ref[0, 3:4, :]
    scale2 = mod_ref[0, 4:5, :]
    x1 = x_ref[0] + gate1 * y2
    x1_ref[0] = x1
    h2 = _rms(x1, g2n_ref[...]) * (1.0 + scale2) + shift2
    h2_ref[0] = h2

    logits = _dot(h2.astype(BF16), wr_ref[...]) + br_ref[...]
    ln = lax.broadcasted_iota(I32, logits.shape, 1)
    neg = jnp.float32(-jnp.inf)
    is_g = ln < N_GROUPS
    gl = jnp.where(is_g, logits, neg)
    gmax = jnp.max(gl, axis=-1, keepdims=True)
    gidx = jnp.min(jnp.where(gl == gmax, ln, LANES), axis=-1, keepdims=True)
    g_p = 1.0 / jnp.sum(jnp.where(is_g, jnp.exp(logits - gmax), 0.0), axis=-1, keepdims=True)
    egrp = (ln - N_GROUPS) // EXPERTS_PER_GROUP
    sel = jnp.where(ln >= N_GROUPS, egrp, -1) == gidx
    el = jnp.where(sel, logits, neg)
    l1 = jnp.max(el, axis=-1, keepdims=True)
    i1 = jnp.min(jnp.where(el == l1, ln, LANES), axis=-1, keepdims=True)
    el2 = jnp.where(ln == i1, neg, el)
    l2 = jnp.max(el2, axis=-1, keepdims=True)
    i2 = jnp.min(jnp.where(el2 == l2, ln, LANES), axis=-1, keepdims=True)
    t = jnp.exp(l2 - l1)
    p1 = 1.0 / (1.0 + t)
    w1 = g_p * p1
    w2 = g_p * (t * p1)
    e1 = (i1 - N_GROUPS).astype(F32)
    e2 = (i2 - N_GROUPS).astype(F32)
    slab = jnp.where(ln == 0, e1, jnp.where(ln == 1, e2, jnp.where(ln == 2, w1, jnp.where(ln == 3, w2, 0.0))))
    slab_ref[...] = slab
    route_ref[...] = slab.T[:ROUTE_ROWS, :]


def _scan_call(x, mod3, pre_outs, p):
    bsz, seq, d = x.shape
    tm = min(SCAN_TM, seq)
    n_tok = bsz * seq
    steps = seq // tm
    n_route = N_GROUPS + N_EXPERTS
    wr = jnp.pad(jnp.concatenate([p["router_group_w"], p["router_expert_w"]], axis=1),
                 ((0, 0), (0, LANES - n_route))).astype(BF16)
    br = jnp.pad(jnp.concatenate([p["router_group_b"], p["router_expert_b"]]), (0, LANES - n_route)).reshape(1, -1)
    row = lambda a: a.reshape(1, -1)
    full = lambda a: pl.BlockSpec(a.shape, lambda b, s: (0,) * a.ndim)
    params = [row(p["rwkv_r_k"]), row(p["rwkv_gn_w"]), row(p["rwkv_gn_b"]), _head_ones(RWKV_WIDTH),
              p["w_out"].astype(BF16), row(p["norm2_g"]), wr, br]
    tok_spec = pl.BlockSpec((1, tm, RWKV_WIDTH), lambda b, s: (b, s, 0))
    x_spec = pl.BlockSpec((1, tm, d), lambda b, s: (b, s, 0))
    in_specs = [x_spec, pl.BlockSpec((1,) + mod3.shape[1:], lambda b, s: (b, 0, 0))]
    in_specs += [tok_spec] * 8 + [full(a) for a in params]
    return pl.pallas_call(
        _scan_kernel,
        grid=(bsz, steps),
        in_specs=in_specs,
        out_specs=[x_spec, x_spec, pl.BlockSpec((ROUTE_ROWS, tm), lambda b, s: (0, b * steps + s)),
                   pl.BlockSpec((tm, LANES), lambda b, s: (b * steps + s, 0))],
        out_shape=[jax.ShapeDtypeStruct((bsz, seq, d), F32), jax.ShapeDtypeStruct((bsz, seq, d), F32),
                   jax.ShapeDtypeStruct((ROUTE_ROWS, n_tok), F32), jax.ShapeDtypeStruct((n_tok, LANES), F32)],
        scratch_shapes=[pltpu.VMEM((RWKV_WIDTH // MXU_DIM, HEAD_DIM, MXU_DIM), F32),
                        pltpu.VMEM((tm, RWKV_WIDTH), F32)],
        compiler_params=pltpu.CompilerParams(dimension_semantics=("arbitrary", "arbitrary"),
                                             vmem_limit_bytes=VMEM_LIMIT),
        name="scan_post",
    )(x, mod3, *pre_outs, *params)


def _onehots(route_ref):
    t = route_ref.shape[1]
    erow = lax.broadcasted_iota(I32, (N_EXPERTS, t), 0)
    oh1 = erow == route_ref[0:1, :].astype(I32)
    oh2 = erow == route_ref[1:2, :].astype(I32)
    return oh1, oh2


def _count_kernel(route_ref, cnt_ref):
    @pl.when(pl.program_id(0) == 0)
    def _():
        cnt_ref[...] = jnp.zeros_like(cnt_ref)

    oh1, oh2 = _onehots(route_ref)
    cnt = oh1.astype(F32) + oh2.astype(F32)
    cnt_ref[...] += jnp.sum(cnt, axis=1, keepdims=True)


def _dest_kernel(route_ref, cnt_ref, dest_ref, meta_ref, base_ref):
    t = route_ref.shape[1]
    nbp = meta_ref.shape[1]

    @pl.when(pl.program_id(0) == 0)
    def _():
        cnt = cnt_ref[...]
        padded = jnp.floor((cnt + (MOE_BM - 1)) * (1.0 / MOE_BM)) * MOE_BM
        er = lax.broadcasted_iota(I32, (N_EXPERTS, N_EXPERTS), 0)
        ec = lax.broadcasted_iota(I32, (N_EXPERTS, N_EXPERTS), 1)
        lower = (ec < er).astype(BF16)
        p_hi = padded.astype(BF16)
        p_lo = (padded - p_hi.astype(F32)).astype(BF16)
        pstart = _dot(lower, p_hi) + _dot(lower, p_lo)
        base_ref[...] = pstart
        pend = (pstart + padded)[:, 0:1]
        blk0 = (lax.broadcasted_iota(I32, (N_EXPERTS, nbp), 1) * MOE_BM).astype(F32)
        be = jnp.sum((pend <= blk0).astype(I32), axis=0, keepdims=True)
        meta_ref[0:1, :] = jnp.minimum(be, N_EXPERTS - 1)
        used = (pend[N_EXPERTS - 1:N_EXPERTS, :] * (1.0 / MOE_BM)).astype(I32)
        meta_ref[1:2, :] = jnp.broadcast_to(used, (1, nbp))
        meta_ref[2:, :] = jnp.zeros((meta_ref.shape[0] - 2, nbp), I32)

    oh1, oh2 = _onehots(route_ref)
    cnt = oh1.astype(F32) + oh2.astype(F32)
    sr = lax.broadcasted_iota(I32, (t, t), 0)
    sc = lax.broadcasted_iota(I32, (t, t), 1)
    upper = (sr < sc).astype(BF16)
    pos = _dot(cnt.astype(BF16), upper) + base_ref[:, 0:1]
    dest_ref[0:1, :] = jnp.sum(jnp.where(oh1, pos, 0.0), axis=0, keepdims=True).astype(I32)
    dest_ref[1:2, :] = jnp.sum(jnp.where(oh2, pos, 0.0), axis=0, keepdims=True).astype(I32)
    base_ref[...] += jnp.sum(cnt, axis=1, keepdims=True)


def _n_blocks(n_tok):
    return -(-2 * n_tok // MOE_BM) + N_EXPERTS


def _route_call(route):
    n_tok = route.shape[1]
    t = min(ROUTE_T, n_tok)
    nbp = -(-_n_blocks(n_tok) // LANES) * LANES
    cparams = pltpu.CompilerParams(dimension_semantics=("arbitrary",), vmem_limit_bytes=VMEM_LIMIT)
    route_spec = pl.BlockSpec((ROUTE_ROWS, t), lambda j: (0, j))
    cnt_spec = pl.BlockSpec((N_EXPERTS, LANES), lambda j: (0, 0))
    counts = pl.pallas_call(
        _count_kernel, grid=(n_tok // t,), in_specs=[route_spec], out_specs=cnt_spec,
        out_shape=jax.ShapeDtypeStruct((N_EXPERTS, LANES), F32), compiler_params=cparams,
        name="route_count")(route)
    dest, meta = pl.pallas_call(
        _dest_kernel, grid=(n_tok // t,), in_specs=[route_spec, cnt_spec],
        out_specs=[pl.BlockSpec((2, t), lambda j: (0, j)), pl.BlockSpec((8, nbp), lambda j: (0, 0))],
        out_shape=[jax.ShapeDtypeStruct((2, n_tok), I32), jax.ShapeDtypeStruct((8, nbp), I32)],
        scratch_shapes=[pltpu.VMEM((N_EXPERTS, LANES), F32)], compiler_params=cparams,
        name="route_dest")(route, counts)
    return dest, meta


def _dispatch_kernel(dest_ref, h_ref, xs_in, xs_hbm, sem):
    del xs_in
    n_tok = dest_ref.shape[0] // 2
    t = h_ref.shape[0]
    base = pl.program_id(0) * t

    def copies(i):
        tok = base + i
        src = h_ref.at[pl.ds(i, 1), :]
        return (pltpu.make_async_copy(src, xs_hbm.at[pl.ds(dest_ref[tok], 1), :], sem),
                pltpu.make_async_copy(src, xs_hbm.at[pl.ds(dest_ref[n_tok + tok], 1), :], sem))

    def start(i, carry):
        for cp in copies(i):
            cp.start()
        return carry

    def wait(i, carry):
        for cp in copies(i):
            cp.wait()
        return carry

    lax.fori_loop(0, t, start, 0)
    lax.fori_loop(0, t, wait, 0)


def _dispatch_call(dest_flat, h_tok, n_slots):
    n_tok, d = h_tok.shape
    t = min(DISPATCH_T, n_tok)
    xs0 = jnp.zeros((n_slots, d), h_tok.dtype)
    return pl.pallas_call(
        _dispatch_kernel,
        grid_spec=pltpu.PrefetchScalarGridSpec(
            num_scalar_prefetch=1, grid=(n_tok // t,),
            in_specs=[pl.BlockSpec((t, d), lambda i, dr: (i, 0)), pl.BlockSpec(memory_space=pl.ANY)],
            out_specs=pl.BlockSpec(memory_space=pl.ANY),
            scratch_shapes=[pltpu.SemaphoreType.DMA]),
        out_shape=jax.ShapeDtypeStruct((n_slots, d), h_tok.dtype),
        input_output_aliases={2: 0},
        compiler_params=pltpu.CompilerParams(dimension_semantics=("arbitrary",)),
        name="moe_dispatch",
    )(dest_flat, h_tok, xs0)


def _expert_kernel(meta_ref, xs_ref, wg_ref, wu_ref, wd_ref, yb_ref, wgb, wub, wdb):
    j = pl.program_id(0)
    nbp = meta_ref.shape[0] // 8
    used = j < meta_ref[nbp]
    new_expert = jnp.logical_or(j == 0, meta_ref[j] != meta_ref[jnp.maximum(j - 1, 0)])

    @pl.when(jnp.logical_and(used, new_expert))
    def _():
        wgb[...] = wg_ref[0].astype(BF16)
        wub[...] = wu_ref[0].astype(BF16)
        wdb[...] = wd_ref[0].astype(BF16)

    @pl.when(used)
    def _():
        x = xs_ref[...].astype(BF16)
        gate = _dot(x, wgb[...])
        up = _dot(x, wub[...])
        hid = (jax.nn.silu(gate) * up).astype(BF16)
        yb_ref[...] = _dot(hid, wdb[...])

    @pl.when(jnp.logical_not(used))
    def _():
        yb_ref[...] = jnp.zeros_like(yb_ref)


def _expert_call(meta_flat, xs, wg, wu, wd):
    n_slots, d = xs.shape
    hid = wg.shape[2]
    return pl.pallas_call(
        _expert_kernel,
        grid_spec=pltpu.PrefetchScalarGridSpec(
            num_scalar_prefetch=1, grid=(n_slots // MOE_BM,),
            in_specs=[pl.BlockSpec((MOE_BM, d), lambda j, m: (j, 0)),
                      pl.BlockSpec((1, d, hid), lambda j, m: (m[j], 0, 0)),
                      pl.BlockSpec((1, d, hid), lambda j, m: (m[j], 0, 0)),
                      pl.BlockSpec((1, hid, d), lambda j, m: (m[j], 0, 0))],
            out_specs=pl.BlockSpec((MOE_BM, d), lambda j, m: (j, 0)),
            scratch_shapes=[pltpu.VMEM((d, hid), BF16), pltpu.VMEM((d, hid), BF16),
                            pltpu.VMEM((hid, d), BF16)]),
        out_shape=jax.ShapeDtypeStruct((n_slots, d), F32),
        compiler_params=pltpu.CompilerParams(dimension_semantics=("arbitrary",),
                                             vmem_limit_bytes=VMEM_LIMIT),
        name="moe_experts",
    )(meta_flat, xs, wg, wu, wd)


def _combine_kernel(dest_ref, yb_hbm, slab_ref, x1_ref, mod_ref, gf_ref, out_ref, gbuf, sem):
    t = x1_ref.shape[0]
    n_tok = dest_ref.shape[0] // 2
    base = pl.program_id(0) * t

    def copies(i):
        tok = base + i
        return (pltpu.make_async_copy(yb_hbm.at[pl.ds(dest_ref[tok], 1), :], gbuf.at[0, pl.ds(i, 1), :], sem),
                pltpu.make_async_copy(yb_hbm.at[pl.ds(dest_ref[n_tok + tok], 1), :], gbuf.at[1, pl.ds(i, 1), :], sem))

    def start(i, carry):
        for cp in copies(i):
            cp.start()
        return carry

    def wait(i, carry):
        for cp in copies(i):
            cp.wait()
        return carry

    lax.fori_loop(0, t, start, 0)
    lax.fori_loop(0, t, wait, 0)
    slab = slab_ref[...]
    y = slab[:, 2:3] * gbuf[0] + slab[:, 3:4] * gbuf[1]
    gate2 = mod_ref[0, 5:6, :]
    x2 = x1_ref[...] + gate2 * y
    out_ref[...] = _rms(x2, gf_ref[...])


def _combine_call(dest_flat, yb, slab, x1_tok, mod3, gf, seq):
    n_tok, d = x1_tok.shape
    t = min(DISPATCH_T, seq)
    return pl.pallas_call(
        _combine_kernel,
        grid_spec=pltpu.PrefetchScalarGridSpec(
            num_scalar_prefetch=1, grid=(n_tok // t,),
            in_specs=[pl.BlockSpec(memory_space=pl.ANY),
                      pl.BlockSpec((t, LANES), lambda i, dr: (i, 0)),
                      pl.BlockSpec((t, d), lambda i, dr: (i, 0)),
                      pl.BlockSpec((1,) + mod3.shape[1:], lambda i, dr: ((i * t) // seq, 0, 0)),
                      pl.BlockSpec((1, d), lambda i, dr: (0, 0))],
            out_specs=pl.BlockSpec((t, d), lambda i, dr: (i, 0)),
            scratch_shapes=[pltpu.VMEM((2, t, d), F32), pltpu.SemaphoreType.DMA]),
        out_shape=jax.ShapeDtypeStruct((n_tok, d), F32),
        compiler_params=pltpu.CompilerParams(dimension_semantics=("arbitrary",),
                                             vmem_limit_bytes=VMEM_LIMIT),
        name="moe_combine",
    )(dest_flat, yb, slab, x1_tok, mod3, gf.reshape(1, d))


def kernel(x, c, ada_w, ada_b, norm1_g, w_in, gmlp_ln_w, gmlp_ln_b, gmlp_ws, gmlp_bs, rwkv_mu, rwkv_w0,
           rwkv_w2, rwkv_a0, rwkv_a2, rwkv_g2, rwkv_k_k, rwkv_k_a, rwkv_r_k, rwkv_gn_w, rwkv_gn_b, w_out,
           norm2_g, router_group_w, router_group_b, router_expert_w, router_expert_b, moe_w_gate,
           moe_w_up, moe_w_down, final_norm_g):
    bsz, seq, d = x.shape
    n_tok = bsz * seq
    assert ada_w.shape[0] == 1, "the final rmsnorm is fused into the (single) layer's combine kernel"
    for l in range(1):
        p = dict(norm1_g=norm1_g[l], w_in=w_in[l], gmlp_ln_w=gmlp_ln_w[l], gmlp_ln_b=gmlp_ln_b[l],
                 gmlp_ws=gmlp_ws[l], gmlp_bs=gmlp_bs[l], rwkv_mu=rwkv_mu[l], rwkv_w0=rwkv_w0[l],
                 rwkv_w2=rwkv_w2[l], rwkv_a0=rwkv_a0[l], rwkv_a2=rwkv_a2[l], rwkv_g2=rwkv_g2[l],
                 rwkv_k_k=rwkv_k_k[l], rwkv_k_a=rwkv_k_a[l], rwkv_r_k=rwkv_r_k[l],
                 rwkv_gn_w=rwkv_gn_w[l], rwkv_gn_b=rwkv_gn_b[l], w_out=w_out[l], norm2_g=norm2_g[l],
                 router_group_w=router_group_w[l], router_group_b=router_group_b[l],
                 router_expert_w=router_expert_w[l], router_expert_b=router_expert_b[l])
        mod3 = _mod_call(c, ada_w[l], ada_b[l]).reshape(bsz, 6, d)
        pre = _pre_call(x, mod3, p)
        x1, h2, route, slab = _scan_call(x, mod3, pre, p)
        dest, meta = _route_call(route)
        dest_flat = dest.reshape(-1)
        n_slots = _n_blocks(n_tok) * MOE_BM
        xs = _dispatch_call(dest_flat, h2.reshape(n_tok, d), n_slots)
        yb = _expert_call(meta.reshape(-1), xs, moe_w_gate[l], moe_w_up[l], moe_w_down[l])
        x = _combine_call(dest_flat, yb, slab, x1.reshape(n_tok, d), mod3, final_norm_g, seq).reshape(bsz, seq, d)
    return x
```

```python
import functools

import jax
import jax.numpy as jnp
from jax import lax
from jax.experimental import pallas as pl
from jax.experimental.pallas import tpu as pltpu

F32 = jnp.float32
BF16 = jnp.bfloat16
I32 = jnp.int32

HEAD_DIM = 64
GMLP_WIDTH = 512
RWKV_WIDTH = 512
GMLP_CHUNK = 128
DECAY_LORA = 32
ICLR_LORA = 32
GATE_LORA = 96
LORA_PAD = 256
N_GROUPS = 4
EXPERTS_PER_GROUP = 8
N_EXPERTS = 32
RMS_EPS = 1e-6
LN_EPS = 1e-5
GN_EPS = 64e-5

LANES = 128
MXU_DIM = 256
VMEM_LIMIT = 56 * 1024 * 1024

PRE_TM = 512
SCAN_TM = 256
SCAN_L = 64
HEADS_PER_GROUP = MXU_DIM // HEAD_DIM
ROUTE_T = 512
MOE_BM = 512
DISPATCH_T = 1024
COMBINE_T = 512
ROUTE_ROWS = 8
DMA_UNROLL = 8


def _dot(a, b):
    return jnp.dot(a, b, preferred_element_type=F32)


def _dot_nt(a, b):
    return lax.dot_general(a, b, (((1,), (1,)), ((), ())), preferred_element_type=F32)


def _rms(x, g):
    return x * lax.rsqrt(jnp.mean(x * x, axis=-1, keepdims=True) + RMS_EPS) * g


ROW_SUB = 8


def _store_row_tiles(ref, val):
    m = val.shape[0]
    for s in range(ROW_SUB):
        ref[pl.ds(s, m, stride=ROW_SUB), :] = val[:, s * LANES:(s + 1) * LANES]


def _load_row_tiles(ref):
    m = ref.shape[0] // ROW_SUB
    return jnp.concatenate([ref[pl.ds(s, m, stride=ROW_SUB), :] for s in range(ROW_SUB)], axis=1)


def _row_tile(ref, row):
    return ref.at[pl.ds(pl.multiple_of(row * ROW_SUB, ROW_SUB), ROW_SUB), :]


def _mod_kernel(c_ref, w_ref, b_ref, o_ref):
    c = c_ref[...]
    w = w_ref[...]
    c_hi = c.astype(BF16)
    c_lo = (c - c_hi.astype(F32)).astype(BF16)
    w_hi = w.astype(BF16)
    w_lo = (w - w_hi.astype(F32)).astype(BF16)
    o_ref[...] = _dot(c_hi, w_hi) + _dot(c_lo, w_hi) + _dot(c_hi, w_lo) + b_ref[...]


def _mod_call(c, ada_w, ada_b):
    bsz, d = c.shape
    n = ada_w.shape[1]
    tn = 1024
    return pl.pallas_call(
        _mod_kernel,
        grid=(n // tn,),
        in_specs=[pl.BlockSpec((bsz, d), lambda j: (0, 0)),
                  pl.BlockSpec((d, tn), lambda j: (0, j)),
                  pl.BlockSpec((1, tn), lambda j: (0, j))],
        out_specs=pl.BlockSpec((bsz, tn), lambda j: (0, j)),
        out_shape=jax.ShapeDtypeStruct((bsz, n), F32),
        compiler_params=pltpu.CompilerParams(dimension_semantics=("arbitrary",),
                                             vmem_limit_bytes=VMEM_LIMIT),
        name="adaln_mod",
    )(c, ada_w, ada_b.reshape(1, n))


def _token_shift(z, carry_ref):
    tm = z.shape[0]
    prev = pltpu.roll(z, 1, axis=0)
    row = lax.broadcasted_iota(I32, z.shape, 0)
    prev = jnp.where(row == 0, carry_ref[0:1, :], prev)
    carry_ref[0:1, :] = z[tm - 1:tm, :]
    return prev


def _pre_kernel(x_ref, mod_ref, g1_ref, wuv_ref, wrkv_ref, wl_ref, lnw_ref, lnb_ref, wsp_ref,
                bsx_ref, mu_rkv_ref, mu_l_ref, lora_w_ref, w0_ref, a0_ref, kk_ref, ka_ref, ones_ref,
                ya_ref, r_ref, k2_ref, v_ref, kkn_ref, al_ref, g_ref, lw_ref,
                carry_rkv, carry_l):
    tm = x_ref.shape[1]

    @pl.when(pl.program_id(1) == 0)
    def _():
        carry_rkv[...] = jnp.zeros_like(carry_rkv)
        carry_l[...] = jnp.zeros_like(carry_l)

    x = x_ref[0]
    shift1 = mod_ref[0, 0:1, :]
    scale1 = mod_ref[0, 1:2, :]
    h = _rms(x, g1_ref[...]) * (1.0 + scale1) + shift1
    hb = h.astype(BF16)
    zuv = _dot(hb, wuv_ref[...])
    zr = _dot(hb, wrkv_ref[...])
    zl = _dot(hb, wl_ref[...])

    u = jax.nn.gelu(zuv[:, :GMLP_WIDTH])
    v = jax.nn.gelu(zuv[:, GMLP_WIDTH:])
    m = jnp.mean(v, axis=-1, keepdims=True)
    var = jnp.mean(jnp.square(v - m), axis=-1, keepdims=True)
    vn = ((v - m) * lax.rsqrt(var + LN_EPS) * lnw_ref[...] + lnb_ref[...]).astype(BF16)
    lane = lax.broadcasted_iota(I32, (GMLP_CHUNK, LANES), 1)
    left = lane < HEAD_DIM
    trow = lax.broadcasted_iota(I32, (GMLP_CHUNK, 2 * GMLP_CHUNK), 0)
    tcol = lax.broadcasted_iota(I32, (GMLP_CHUNK, 2 * GMLP_CHUNK), 1) % GMLP_CHUNK
    causal = tcol <= trow
    zero = jnp.zeros((), BF16)
    for p in range(GMLP_WIDTH // LANES):
        wpair = jnp.where(causal, wsp_ref[p], 0.0).astype(BF16)
        for c in range(tm // GMLP_CHUNK):
            rows = slice(c * GMLP_CHUNK, (c + 1) * GMLP_CHUNK)
            cols = slice(p * LANES, (p + 1) * LANES)
            vc = vn[rows, cols]
            rhs = jnp.concatenate([jnp.where(left, vc, zero), jnp.where(left, zero, vc)], axis=0)
            mixed = _dot(wpair, rhs) + bsx_ref[:, cols]
            ya_ref[0, rows, cols] = (u[rows, cols] * mixed).astype(ya_ref.dtype)

    zr = zr + (_token_shift(zr, carry_rkv) - zr) * mu_rkv_ref[...]
    zl = zl + (_token_shift(zl, carry_l) - zl) * mu_l_ref[...]
    r = zr[:, :RWKV_WIDTH]
    k = zr[:, RWKV_WIDTH:2 * RWKV_WIDTH]
    vv = zr[:, 2 * RWKV_WIDTH:]
    llane = lax.broadcasted_iota(I32, zl.shape, 1)
    f = jnp.where(llane < DECAY_LORA, jnp.tanh(zl),
                  jnp.where(llane < DECAY_LORA + ICLR_LORA, zl, jax.nn.sigmoid(zl)))
    lo = _dot(f.astype(BF16), lora_w_ref[...])
    wpre = w0_ref[...] + lo[:, :RWKV_WIDTH]
    w_log = -(jnp.maximum(-wpre, 0.0) + jnp.log1p(jnp.exp(-jnp.abs(wpre)))) - 0.5
    lw_ref[0] = -jnp.exp(w_log)
    alpha = jax.nn.sigmoid(a0_ref[...] + lo[:, RWKV_WIDTH:2 * RWKV_WIDTH])
    g_ref[0] = lo[:, 2 * RWKV_WIDTH:].astype(g_ref.dtype)
    kkr = k * kk_ref[...]
    ss = _dot((kkr * kkr).astype(BF16), ones_ref[...])
    kkn_ref[0] = (kkr / jnp.maximum(jnp.sqrt(ss), 1e-12)).astype(kkn_ref.dtype)
    k2_ref[0] = (k * (1.0 + (alpha - 1.0) * ka_ref[...])).astype(k2_ref.dtype)
    r_ref[0] = r.astype(r_ref.dtype)
    v_ref[0] = vv.astype(v_ref.dtype)
    al_ref[0] = alpha.astype(al_ref.dtype)


def _head_ones(width):
    hid = jnp.arange(width) // HEAD_DIM
    return (hid[:, None] == hid[None, :]).astype(BF16)


def _pre_call(x, mod3, p):
    bsz, seq, d = x.shape
    tm = min(PRE_TM, seq)
    w_in = p["w_in"]
    wuv = w_in[:, :2 * GMLP_WIDTH].astype(BF16)
    wrkv = w_in[:, 2 * GMLP_WIDTH:2 * GMLP_WIDTH + 3 * RWKV_WIDTH].astype(BF16)
    n_lora = DECAY_LORA + ICLR_LORA + GATE_LORA
    wl = jnp.pad(w_in[:, 2 * GMLP_WIDTH + 3 * RWKV_WIDTH:], ((0, 0), (0, LORA_PAD - n_lora))).astype(BF16)
    mu = p["rwkv_mu"]
    mu_rkv = mu[:3 * RWKV_WIDTH].reshape(1, -1)
    mu_l = jnp.pad(mu[3 * RWKV_WIDTH:], (0, LORA_PAD - n_lora)).reshape(1, -1)
    zpad = lambda a, lo_, hi_: jnp.pad(a, ((lo_, LORA_PAD - hi_), (0, 0)))
    lora_w = jnp.concatenate([
        zpad(p["rwkv_w2"], 0, DECAY_LORA),
        zpad(p["rwkv_a2"], DECAY_LORA, DECAY_LORA + ICLR_LORA),
        zpad(p["rwkv_g2"], DECAY_LORA + ICLR_LORA, n_lora)], axis=1).astype(BF16)
    ws = p["gmlp_ws"]
    wsp = jnp.concatenate([ws[0::2], ws[1::2]], axis=2)
    bsx = jnp.repeat(p["gmlp_bs"].T, HEAD_DIM, axis=1)
    row = lambda a: a.reshape(1, -1)
    full = lambda a: pl.BlockSpec(a.shape, lambda b, s: (0,) * a.ndim)
    args = [mod3, row(p["norm1_g"]), wuv, wrkv, wl, row(p["gmlp_ln_w"]), row(p["gmlp_ln_b"]), wsp, bsx,
            mu_rkv, mu_l, lora_w, row(p["rwkv_w0"]), row(p["rwkv_a0"]), row(p["rwkv_k_k"]),
            row(p["rwkv_k_a"]), _head_ones(RWKV_WIDTH)]
    in_specs = [pl.BlockSpec((1, tm, d), lambda b, s: (b, s, 0)),
                pl.BlockSpec((1,) + mod3.shape[1:], lambda b, s: (b, 0, 0))]
    in_specs += [full(a) for a in args[1:]]
    tok = lambda dt: jax.ShapeDtypeStruct((bsz, seq, RWKV_WIDTH), dt)
    out_spec = pl.BlockSpec((1, tm, RWKV_WIDTH), lambda b, s: (b, s, 0))
    outs = [tok(BF16)] * 7 + [tok(F32)]
    return pl.pallas_call(
        _pre_kernel,
        grid=(bsz, seq // tm),
        in_specs=in_specs,
        out_specs=[out_spec] * 8,
        out_shape=outs,
        scratch_shapes=[pltpu.VMEM((8, 3 * RWKV_WIDTH), F32), pltpu.VMEM((8, LORA_PAD), F32)],
        compiler_params=pltpu.CompilerParams(dimension_semantics=("arbitrary", "arbitrary"),
                                             vmem_limit_bytes=VMEM_LIMIT),
        name="pre_mix",
    )(x, *args)


def _blockdiag(xb, mask):
    t = jnp.concatenate([xb] * HEADS_PER_GROUP, axis=0)
    return jnp.where(mask, t, jnp.zeros((), xb.dtype))


def _diag_blocks(full, lane_head):
    out = jnp.zeros((HEAD_DIM, MXU_DIM), F32)
    for hh in range(HEADS_PER_GROUP):
        out = out + jnp.where(lane_head == hh, full[hh * HEAD_DIM:(hh + 1) * HEAD_DIM, :], 0.0)
    return out


def _scan_kernel(x_ref, mod_ref, ya_ref, r_ref, k2_ref, v_ref, kk_ref, al_ref, g_ref, lw_ref,
                 rk_ref, gnw_ref, gnb_ref, ones_ref, wout_ref, g2n_ref, wr_ref, br_ref,
                 x1_ref, h2_ref, route_ref, slab_ref, state_ref, y_scr):
    tm = x_ref.shape[1]
    L = SCAN_L
    n_chunks = tm // L
    n_groups = RWKV_WIDTH // MXU_DIM

    @pl.when(pl.program_id(1) == 0)
    def _():
        state_ref[...] = jnp.zeros_like(state_ref)

    r = r_ref[0].astype(F32)
    k2 = k2_ref[0].astype(F32)
    v = v_ref[0].astype(F32)
    kk = kk_ref[0].astype(F32)
    b = kk * al_ref[0].astype(F32)
    lw = lw_ref[0]
    lw_hi = lw.astype(BF16)
    lw_lo = (lw - lw_hi.astype(F32)).astype(BF16)

    trow = lax.broadcasted_iota(I32, (L, L), 0)
    tcol = lax.broadcasted_iota(I32, (L, L), 1)
    tri = (tcol <= trow).astype(BF16)
    row = lax.broadcasted_iota(I32, (L, MXU_DIM), 0)
    lane = lax.broadcasted_iota(I32, (L, MXU_DIM), 1)
    lane_j = lane % HEAD_DIM
    lane_head = lane // HEAD_DIM
    strict = lane_j < row
    incl = lane_j <= row
    eye = lane_j == row
    brow = lax.broadcasted_iota(I32, (MXU_DIM, MXU_DIM), 0) // HEAD_DIM
    bcol = lax.broadcasted_iota(I32, (MXU_DIM, MXU_DIM), 1) // HEAD_DIM
    bmask = brow == bcol

    def bd(xf):
        return _blockdiag(xf.astype(BF16), bmask)

    probs = [(c, g) for c in range(n_chunks) for g in range(n_groups)]
    each = lambda fn, *lists: [fn(*a) for a in zip(*lists)]
    lgs = [_dot(tri, lw_hi[c * L:(c + 1) * L]) + _dot(tri, lw_lo[c * L:(c + 1) * L])
           for c in range(n_chunks)]

    def sl(arr, c, g):
        return arr[c * L:(c + 1) * L, g * MXU_DIM:(g + 1) * MXU_DIM]

    lgc = [lgs[c][:, g * MXU_DIM:(g + 1) * MXU_DIM] for c, g in probs]
    glast = [x[L - 1:L, :] for x in lgc]
    eg = [jnp.exp(x) for x in lgc]
    egx = [jnp.exp(x - sl(lw, c, g)) for x, (c, g) in zip(lgc, probs)]
    einv = [jnp.exp(-x) for x in lgc]
    ehat = each(lambda gl, x: jnp.exp(gl - x), glast, lgc)
    at = [-sl(kk, c, g) * e for e, (c, g) in zip(egx, probs)]
    rt = [sl(r, c, g) * e for e, (c, g) in zip(eg, probs)]
    bt = [sl(b, c, g) * e for e, (c, g) in zip(einv, probs)]
    kt = [sl(k2, c, g) * e for e, (c, g) in zip(einv, probs)]
    bh = [sl(b, c, g) * e for e, (c, g) in zip(ehat, probs)]
    kh = [sl(k2, c, g) * e for e, (c, g) in zip(ehat, probs)]
    vc = [sl(v, c, g) for c, g in probs]
    lhs = each(lambda a_, r_: jnp.concatenate([a_, r_], axis=0).astype(BF16), at, rt)
    pb = each(lambda l_, x: _dot_nt(l_, bd(x)), lhs, bt)
    pk = each(lambda l_, x: _dot_nt(l_, bd(x)), lhs, kt)
    a_ab = [jnp.where(strict, x[:L], 0.0) for x in pb]
    a_rb = [jnp.where(incl, x[L:], 0.0) for x in pb]
    a_ak = [jnp.where(strict, x[:L], 0.0) for x in pk]
    a_rk = [jnp.where(incl, x[L:], 0.0) for x in pk]
    ssum = [jnp.where(eye, 1.0, 0.0) + x for x in a_ab]
    xb = [x.astype(BF16) for x in a_ab]
    xp = [_dot(x, _blockdiag(x, bmask)) for x in xb]
    m = 2
    while m < L // 2:
        xb = [x.astype(BF16) for x in xp]
        res = each(lambda x, s_: _dot(jnp.concatenate([x, s_.astype(BF16)], axis=0), _blockdiag(x, bmask)),
                   xb, ssum)
        xp = [x[:L] for x in res]
        ssum = each(lambda s_, x: s_ + x[L:], ssum, res)
        m *= 2
    tmat = each(lambda s_, x: s_ + _dot(s_.astype(BF16), bd(x)), ssum, xp)
    res = each(lambda a1, a2, x: _dot(jnp.concatenate([a1, a2], axis=0).astype(BF16), bd(x)), a_ak, a_rk, vc)
    akv = [x[:L] for x in res]
    arkv = [x[L:] for x in res]
    tb = [x.astype(BF16) for x in tmat]
    wm = each(lambda t_, x: _dot(t_, bd(x)), tb, at)
    u0 = each(lambda t_, x: _dot(t_, bd(x)), tb, akv)
    arb = [x.astype(BF16) for x in a_rb]
    qp = each(lambda r_, a_, x: r_ + _dot(a_, bd(x)), rt, arb, wm)
    y0 = each(lambda a_, x, z: _dot(a_, bd(x)) + z, arb, u0, arkv)
    bht = [x.T for x in bh]
    kht = [x.T for x in kh]
    full_m = each(lambda b_, w_: _dot(b_.astype(BF16), w_.astype(BF16)), bht, wm)
    full_n = each(lambda b_, k_, u_, v_: _dot(jnp.concatenate([b_, k_], axis=1).astype(BF16),
                                              jnp.concatenate([u_, v_], axis=0).astype(BF16)),
                  bht, kht, u0, vc)
    mt = each(lambda f_, gl: _diag_blocks(f_, lane_head) + jnp.where(eye, jnp.exp(gl), 0.0), full_m, glast)
    nt = [_diag_blocks(f_, lane_head) for f_ in full_n]
    lhs2 = each(lambda q_, m_: jnp.concatenate([q_, m_], axis=0).astype(BF16), qp, mt)

    st = [state_ref[g] for g in range(n_groups)]
    for c in range(n_chunks):
        for g in range(n_groups):
            i = c * n_groups + g
            res_s = _dot(lhs2[i], bd(st[g]))
            y_scr[c * L:(c + 1) * L, g * MXU_DIM:(g + 1) * MXU_DIM] = res_s[:L] + y0[i]
            st[g] = res_s[L:] + nt[i]
    for g in range(n_groups):
        state_ref[g] = st[g]

    y = y_scr[...]
    ones = ones_ref[...]
    inv_n = 1.0 / HEAD_DIM
    mean = _dot(y.astype(BF16), ones) * inv_n
    dlt = y - mean
    var = _dot((dlt * dlt).astype(BF16), ones) * inv_n
    yn = dlt * lax.rsqrt(var + GN_EPS) * gnw_ref[...] + gnb_ref[...]
    bonus = _dot((r * k2 * rk_ref[...]).astype(BF16), ones) * v
    yb = (yn + bonus) * g_ref[0].astype(F32)
    cat = jnp.concatenate([ya_ref[0], yb.astype(BF16)], axis=1)
    y2 = _dot(cat, wout_ref[...])
    gate1 = mod_ref[0, 2:3, :]
    shift2 = mod_ref[0, 3:4, :]
    scale2 = mod_ref[0, 4:5, :]
    x1 = x_ref[0] + gate1 * y2
    x1_ref[0] = x1
    h2 = _rms(x1, g2n_ref[...]) * (1.0 + scale2) + shift2
    _store_row_tiles(h2_ref.at[0], h2)

    logits = _dot(h2.astype(BF16), wr_ref[...]) + br_ref[...]
    ln = lax.broadcasted_iota(I32, logits.shape, 1)
    neg = jnp.float32(-jnp.inf)
    is_g = ln < N_GROUPS
    gl = jnp.where(is_g, logits, neg)
    gmax = jnp.max(gl, axis=-1, keepdims=True)
    gidx = jnp.min(jnp.where(gl == gmax, ln, LANES), axis=-1, keepdims=True)
    g_p = 1.0 / jnp.sum(jnp.where(is_g, jnp.exp(logits - gmax), 0.0), axis=-1, keepdims=True)
    egrp = (ln - N_GROUPS) // EXPERTS_PER_GROUP
    sel = jnp.where(ln >= N_GROUPS, egrp, -1) == gidx
    el = jnp.where(sel, logits, neg)
    l1 = jnp.max(el, axis=-1, keepdims=True)
    i1 = jnp.min(jnp.where(el == l1, ln, LANES), axis=-1, keepdims=True)
    el2 = jnp.where(ln == i1, neg, el)
    l2 = jnp.max(el2, axis=-1, keepdims=True)
    i2 = jnp.min(jnp.where(el2 == l2, ln, LANES), axis=-1, keepdims=True)
    t = jnp.exp(l2 - l1)
    p1 = 1.0 / (1.0 + t)
    w1 = g_p * p1
    w2 = g_p * (t * p1)
    e1 = (i1 - N_GROUPS).astype(F32)
    e2 = (i2 - N_GROUPS).astype(F32)
    slab = jnp.where(ln == 0, e1, jnp.where(ln == 1, e2, jnp.where(ln == 2, w1, jnp.where(ln == 3, w2, 0.0))))
    slab_ref[...] = slab
    route_ref[...] = slab.T[:ROUTE_ROWS, :]


def _scan_call(x, mod3, pre_outs, p):
    bsz, seq, d = x.shape
    tm = min(SCAN_TM, seq)
    n_tok = bsz * seq
    steps = seq // tm
    n_route = N_GROUPS + N_EXPERTS
    wr = jnp.pad(jnp.concatenate([p["router_group_w"], p["router_expert_w"]], axis=1),
                 ((0, 0), (0, LANES - n_route))).astype(BF16)
    br = jnp.pad(jnp.concatenate([p["router_group_b"], p["router_expert_b"]]), (0, LANES - n_route)).reshape(1, -1)
    row = lambda a: a.reshape(1, -1)
    full = lambda a: pl.BlockSpec(a.shape, lambda b, s: (0,) * a.ndim)
    params = [row(p["rwkv_r_k"]), row(p["rwkv_gn_w"]), row(p["rwkv_gn_b"]), _head_ones(RWKV_WIDTH),
              p["w_out"].astype(BF16), row(p["norm2_g"]), wr, br]
    tok_spec = pl.BlockSpec((1, tm, RWKV_WIDTH), lambda b, s: (b, s, 0))
    x_spec = pl.BlockSpec((1, tm, d), lambda b, s: (b, s, 0))
    in_specs = [x_spec, pl.BlockSpec((1,) + mod3.shape[1:], lambda b, s: (b, 0, 0))]
    in_specs += [tok_spec] * 8 + [full(a) for a in params]
    return pl.pallas_call(
        _scan_kernel,
        grid=(bsz, steps),
        in_specs=in_specs,
        out_specs=[x_spec, pl.BlockSpec((1, tm * ROW_SUB, LANES), lambda b, s: (b, s, 0)),
                   pl.BlockSpec((ROUTE_ROWS, tm), lambda b, s: (0, b * steps + s)),
                   pl.BlockSpec((tm, LANES), lambda b, s: (b * steps + s, 0))],
        out_shape=[jax.ShapeDtypeStruct((bsz, seq, d), F32),
                   jax.ShapeDtypeStruct((bsz, seq * ROW_SUB, LANES), F32),
                   jax.ShapeDtypeStruct((ROUTE_ROWS, n_tok), F32), jax.ShapeDtypeStruct((n_tok, LANES), F32)],
        scratch_shapes=[pltpu.VMEM((RWKV_WIDTH // MXU_DIM, HEAD_DIM, MXU_DIM), F32),
                        pltpu.VMEM((tm, RWKV_WIDTH), F32)],
        compiler_params=pltpu.CompilerParams(dimension_semantics=("arbitrary", "arbitrary"),
                                             vmem_limit_bytes=VMEM_LIMIT),
        name="scan_post",
    )(x, mod3, *pre_outs, *params)


def _onehots(route_ref):
    t = route_ref.shape[1]
    erow = lax.broadcasted_iota(I32, (N_EXPERTS, t), 0)
    oh1 = erow == route_ref[0:1, :].astype(I32)
    oh2 = erow == route_ref[1:2, :].astype(I32)
    return oh1, oh2


def _count_kernel(route_ref, cnt_ref):
    @pl.when(pl.program_id(0) == 0)
    def _():
        cnt_ref[...] = jnp.zeros_like(cnt_ref)

    oh1, oh2 = _onehots(route_ref)
    cnt = oh1.astype(F32) + oh2.astype(F32)
    cnt_ref[...] += jnp.sum(cnt, axis=1, keepdims=True)


def _dest_kernel(route_ref, cnt_ref, dest_ref, meta_ref, base_ref):
    t = route_ref.shape[1]
    nbp = meta_ref.shape[1]

    @pl.when(pl.program_id(0) == 0)
    def _():
        cnt = cnt_ref[...]
        padded = jnp.floor((cnt + (MOE_BM - 1)) * (1.0 / MOE_BM)) * MOE_BM
        er = lax.broadcasted_iota(I32, (N_EXPERTS, N_EXPERTS), 0)
        ec = lax.broadcasted_iota(I32, (N_EXPERTS, N_EXPERTS), 1)
        lower = (ec < er).astype(BF16)
        p_hi = padded.astype(BF16)
        p_lo = (padded - p_hi.astype(F32)).astype(BF16)
        pstart = _dot(lower, p_hi) + _dot(lower, p_lo)
        base_ref[...] = pstart
        pend = (pstart + padded)[:, 0:1]
        blk0 = (lax.broadcasted_iota(I32, (N_EXPERTS, nbp), 1) * MOE_BM).astype(F32)
        be = jnp.sum((pend <= blk0).astype(I32), axis=0, keepdims=True)
        meta_ref[0:1, :] = jnp.minimum(be, N_EXPERTS - 1)
        used = (pend[N_EXPERTS - 1:N_EXPERTS, :] * (1.0 / MOE_BM)).astype(I32)
        meta_ref[1:2, :] = jnp.broadcast_to(used, (1, nbp))
        meta_ref[2:, :] = jnp.zeros((meta_ref.shape[0] - 2, nbp), I32)

    oh1, oh2 = _onehots(route_ref)
    cnt = oh1.astype(F32) + oh2.astype(F32)
    sr = lax.broadcasted_iota(I32, (t, t), 0)
    sc = lax.broadcasted_iota(I32, (t, t), 1)
    upper = (sr < sc).astype(BF16)
    pos = _dot(cnt.astype(BF16), upper) + base_ref[:, 0:1]
    dest_ref[0:1, :] = jnp.sum(jnp.where(oh1, pos, 0.0), axis=0, keepdims=True).astype(I32)
    dest_ref[1:2, :] = jnp.sum(jnp.where(oh2, pos, 0.0), axis=0, keepdims=True).astype(I32)
    base_ref[...] += jnp.sum(cnt, axis=1, keepdims=True)


def _n_blocks(n_tok):
    return -(-2 * n_tok // MOE_BM) + N_EXPERTS


def _route_call(route):
    n_tok = route.shape[1]
    t = min(ROUTE_T, n_tok)
    nbp = -(-_n_blocks(n_tok) // LANES) * LANES
    cparams = pltpu.CompilerParams(dimension_semantics=("arbitrary",), vmem_limit_bytes=VMEM_LIMIT)
    route_spec = pl.BlockSpec((ROUTE_ROWS, t), lambda j: (0, j))
    cnt_spec = pl.BlockSpec((N_EXPERTS, LANES), lambda j: (0, 0))
    counts = pl.pallas_call(
        _count_kernel, grid=(n_tok // t,), in_specs=[route_spec], out_specs=cnt_spec,
        out_shape=jax.ShapeDtypeStruct((N_EXPERTS, LANES), F32), compiler_params=cparams,
        name="route_count")(route)
    dest, meta = pl.pallas_call(
        _dest_kernel, grid=(n_tok // t,), in_specs=[route_spec, cnt_spec],
        out_specs=[pl.BlockSpec((2, t), lambda j: (0, j)), pl.BlockSpec((8, nbp), lambda j: (0, 0))],
        out_shape=[jax.ShapeDtypeStruct((2, n_tok), I32), jax.ShapeDtypeStruct((8, nbp), I32)],
        scratch_shapes=[pltpu.VMEM((N_EXPERTS, LANES), F32)], compiler_params=cparams,
        name="route_dest")(route, counts)
    return dest, meta


def _dispatch_kernel(dest_ref, h_ref, xs_in, xs_hbm, sem):
    del xs_in
    n_tok = dest_ref.shape[0] // 2
    t = h_ref.shape[0] // ROW_SUB
    base = pl.program_id(0) * t

    def start(i, carry):
        tok = base + i
        src = _row_tile(h_ref, i)
        pltpu.make_async_copy(src, _row_tile(xs_hbm, dest_ref[tok]), sem).start(priority=0)
        pltpu.make_async_copy(src, _row_tile(xs_hbm, dest_ref[n_tok + tok]), sem).start(priority=1)
        return carry

    lax.fori_loop(0, t, start, 0, unroll=DMA_UNROLL)
    for _ in range(2):
        pltpu.make_async_copy(h_ref, xs_hbm.at[pl.ds(0, t * ROW_SUB), :], sem).wait()


def _dispatch_call(dest_flat, h_tok, n_slots):
    n_tok = h_tok.shape[0] // ROW_SUB
    t = min(DISPATCH_T, n_tok)
    xs0 = jnp.zeros((n_slots * ROW_SUB, LANES), h_tok.dtype)
    return pl.pallas_call(
        _dispatch_kernel,
        grid_spec=pltpu.PrefetchScalarGridSpec(
            num_scalar_prefetch=1, grid=(n_tok // t,),
            in_specs=[pl.BlockSpec((t * ROW_SUB, LANES), lambda i, dr: (i, 0)),
                      pl.BlockSpec(memory_space=pl.ANY)],
            out_specs=pl.BlockSpec(memory_space=pl.ANY),
            scratch_shapes=[pltpu.SemaphoreType.DMA]),
        out_shape=jax.ShapeDtypeStruct((n_slots * ROW_SUB, LANES), h_tok.dtype),
        input_output_aliases={2: 0},
        compiler_params=pltpu.CompilerParams(dimension_semantics=("arbitrary",)),
        name="moe_dispatch",
    )(dest_flat, h_tok, xs0)


def _expert_kernel(meta_ref, xs_ref, wg_ref, wu_ref, wd_ref, yb_ref, wgb, wub, wdb):
    j = pl.program_id(0)
    nbp = meta_ref.shape[0] // 8
    used = j < meta_ref[nbp]
    new_expert = jnp.logical_or(j == 0, meta_ref[j] != meta_ref[jnp.maximum(j - 1, 0)])

    @pl.when(jnp.logical_and(used, new_expert))
    def _():
        wgb[...] = wg_ref[0].astype(BF16)
        wub[...] = wu_ref[0].astype(BF16)
        wdb[...] = wd_ref[0].astype(BF16)

    @pl.when(used)
    def _():
        x = _load_row_tiles(xs_ref).astype(BF16)
        gate = _dot(x, wgb[...])
        up = _dot(x, wub[...])
        hid = (jax.nn.silu(gate) * up).astype(BF16)
        _store_row_tiles(yb_ref, _dot(hid, wdb[...]))

    @pl.when(jnp.logical_not(used))
    def _():
        yb_ref[...] = jnp.zeros_like(yb_ref)


def _expert_call(meta_flat, xs, wg, wu, wd):
    n_slots = xs.shape[0] // ROW_SUB
    d, hid = wg.shape[1:]
    blk = pl.BlockSpec((MOE_BM * ROW_SUB, LANES), lambda j, m: (j, 0))
    return pl.pallas_call(
        _expert_kernel,
        grid_spec=pltpu.PrefetchScalarGridSpec(
            num_scalar_prefetch=1, grid=(n_slots // MOE_BM,),
            in_specs=[blk,
                      pl.BlockSpec((1, d, hid), lambda j, m: (m[j], 0, 0)),
                      pl.BlockSpec((1, d, hid), lambda j, m: (m[j], 0, 0)),
                      pl.BlockSpec((1, hid, d), lambda j, m: (m[j], 0, 0))],
            out_specs=blk,
            scratch_shapes=[pltpu.VMEM((d, hid), BF16), pltpu.VMEM((d, hid), BF16),
                            pltpu.VMEM((hid, d), BF16)]),
        out_shape=jax.ShapeDtypeStruct((n_slots * ROW_SUB, LANES), F32),
        compiler_params=pltpu.CompilerParams(dimension_semantics=("arbitrary",),
                                             vmem_limit_bytes=VMEM_LIMIT),
        name="moe_experts",
    )(meta_flat, xs, wg, wu, wd)


def _combine_kernel(dest_ref, yb_hbm, slab_ref, x1_ref, mod_ref, gf_ref, out_ref, gbuf, sem):
    t = x1_ref.shape[0]
    n_tok = dest_ref.shape[0] // 2
    step = pl.program_id(0)
    slot = step % 2

    def issue(tile, buf):
        def body(r, carry):
            tok = tile * t + r
            pltpu.make_async_copy(_row_tile(yb_hbm, dest_ref[tok]), _row_tile(gbuf.at[buf, 0], r),
                                  sem.at[buf]).start(priority=0)
            pltpu.make_async_copy(_row_tile(yb_hbm, dest_ref[n_tok + tok]), _row_tile(gbuf.at[buf, 1], r),
                                  sem.at[buf]).start(priority=1)
            return carry
        lax.fori_loop(0, t, body, 0, unroll=DMA_UNROLL)

    @pl.when(step == 0)
    def _():
        issue(step, slot)

    @pl.when(step + 1 < pl.num_programs(0))
    def _():
        issue(step + 1, 1 - slot)

    for kk in range(2):
        pltpu.make_async_copy(yb_hbm.at[pl.ds(0, t * ROW_SUB), :], gbuf.at[slot, kk], sem.at[slot]).wait()
    slab = slab_ref[...]
    y = (slab[:, 2:3] * _load_row_tiles(gbuf.at[slot, 0])
         + slab[:, 3:4] * _load_row_tiles(gbuf.at[slot, 1]))
    gate2 = mod_ref[0, 5:6, :]
    x2 = x1_ref[...] + gate2 * y
    out_ref[...] = _rms(x2, gf_ref[...])


def _combine_call(dest_flat, yb, slab, x1_tok, mod3, gf, seq):
    n_tok, d = x1_tok.shape
    t = min(COMBINE_T, seq)
    return pl.pallas_call(
        _combine_kernel,
        grid_spec=pltpu.PrefetchScalarGridSpec(
            num_scalar_prefetch=1, grid=(n_tok // t,),
            in_specs=[pl.BlockSpec(memory_space=pl.ANY),
                      pl.BlockSpec((t, LANES), lambda i, dr: (i, 0)),
                      pl.BlockSpec((t, d), lambda i, dr: (i, 0)),
                      pl.BlockSpec((1,) + mod3.shape[1:], lambda i, dr: ((i * t) // seq, 0, 0)),
                      pl.BlockSpec((1, d), lambda i, dr: (0, 0))],
            out_specs=pl.BlockSpec((t, d), lambda i, dr: (i, 0)),
            scratch_shapes=[pltpu.VMEM((2, 2, t * ROW_SUB, LANES), F32), pltpu.SemaphoreType.DMA((2,))]),
        out_shape=jax.ShapeDtypeStruct((n_tok, d), F32),
        compiler_params=pltpu.CompilerParams(dimension_semantics=("arbitrary",),
                                             vmem_limit_bytes=VMEM_LIMIT),
        name="moe_combine",
    )(dest_flat, yb, slab, x1_tok, mod3, gf.reshape(1, d))


def kernel(x, c, ada_w, ada_b, norm1_g, w_in, gmlp_ln_w, gmlp_ln_b, gmlp_ws, gmlp_bs, rwkv_mu, rwkv_w0,
           rwkv_w2, rwkv_a0, rwkv_a2, rwkv_g2, rwkv_k_k, rwkv_k_a, rwkv_r_k, rwkv_gn_w, rwkv_gn_b, w_out,
           norm2_g, router_group_w, router_group_b, router_expert_w, router_expert_b, moe_w_gate,
           moe_w_up, moe_w_down, final_norm_g):
    bsz, seq, d = x.shape
    n_tok = bsz * seq
    assert ada_w.shape[0] == 1, "the final rmsnorm is fused into the (single) layer's combine kernel"
    for l in range(1):
        p = dict(norm1_g=norm1_g[l], w_in=w_in[l], gmlp_ln_w=gmlp_ln_w[l], gmlp_ln_b=gmlp_ln_b[l],
                 gmlp_ws=gmlp_ws[l], gmlp_bs=gmlp_bs[l], rwkv_mu=rwkv_mu[l], rwkv_w0=rwkv_w0[l],
                 rwkv_w2=rwkv_w2[l], rwkv_a0=rwkv_a0[l], rwkv_a2=rwkv_a2[l], rwkv_g2=rwkv_g2[l],
                 rwkv_k_k=rwkv_k_k[l], rwkv_k_a=rwkv_k_a[l], rwkv_r_k=rwkv_r_k[l],
                 rwkv_gn_w=rwkv_gn_w[l], rwkv_gn_b=rwkv_gn_b[l], w_out=w_out[l], norm2_g=norm2_g[l],
                 router_group_w=router_group_w[l], router_group_b=router_group_b[l],
                 router_expert_w=router_expert_w[l], router_expert_b=router_expert_b[l])
        mod3 = _mod_call(c, ada_w[l], ada_b[l]).reshape(bsz, 6, d)
        pre = _pre_call(x, mod3, p)
        x1, h2, route, slab = _scan_call(x, mod3, pre, p)
        dest, meta = _route_call(route)
        dest_flat = dest.reshape(-1)
        n_slots = _n_blocks(n_tok) * MOE_BM
        xs = _dispatch_call(dest_flat, h2.reshape(n_tok * ROW_SUB, LANES), n_slots)
        yb = _expert_call(meta.reshape(-1), xs, moe_w_gate[l], moe_w_up[l], moe_w_down[l])
        x = _combine_call(dest_flat, yb, slab, x1.reshape(n_tok, d), mod3, final_norm_g, seq).reshape(bsz, seq, d)
    return x
```

```python
import functools

import jax
import jax.numpy as jnp
from jax import lax
from jax.experimental import pallas as pl
from jax.experimental.pallas import tpu as pltpu

F32 = jnp.float32
BF16 = jnp.bfloat16
I32 = jnp.int32

HEAD_DIM = 64
GMLP_WIDTH = 512
RWKV_WIDTH = 512
GMLP_CHUNK = 128
DECAY_LORA = 32
ICLR_LORA = 32
GATE_LORA = 96
LORA_PAD = 256
N_GROUPS = 4
EXPERTS_PER_GROUP = 8
N_EXPERTS = 32
RMS_EPS = 1e-6
LN_EPS = 1e-5
GN_EPS = 64e-5

LANES = 128
MXU_DIM = 256
VMEM_LIMIT = 56 * 1024 * 1024

PRE_TM = 512
SCAN_TM = 512
SCAN_L = 64
HEADS_PER_GROUP = MXU_DIM // HEAD_DIM
ROUTE_T = 512
MOE_BM = 512
DISPATCH_T = 1024
COMBINE_T = 512
ROUTE_ROWS = 8
DMA_UNROLL = 8


def _dot(a, b):
    return jnp.dot(a, b, preferred_element_type=F32)


def _dot_nt(a, b):
    return lax.dot_general(a, b, (((1,), (1,)), ((), ())), preferred_element_type=F32)


def _rms(x, g):
    return x * lax.rsqrt(jnp.mean(x * x, axis=-1, keepdims=True) + RMS_EPS) * g


ROW_SUB = 8


def _store_row_tiles(ref, val):
    m = val.shape[0]
    for s in range(ROW_SUB):
        ref[pl.ds(s, m, stride=ROW_SUB), :] = val[:, s * LANES:(s + 1) * LANES]


def _load_row_tiles(ref):
    m = ref.shape[0] // ROW_SUB
    return jnp.concatenate([ref[pl.ds(s, m, stride=ROW_SUB), :] for s in range(ROW_SUB)], axis=1)


def _row_tile(ref, row):
    return ref.at[pl.ds(pl.multiple_of(row * ROW_SUB, ROW_SUB), ROW_SUB), :]


def _mod_kernel(c_ref, w_ref, b_ref, o_ref):
    c = c_ref[...]
    w = w_ref[...]
    c_hi = c.astype(BF16)
    c_lo = (c - c_hi.astype(F32)).astype(BF16)
    w_hi = w.astype(BF16)
    w_lo = (w - w_hi.astype(F32)).astype(BF16)
    o_ref[...] = _dot(c_hi, w_hi) + _dot(c_lo, w_hi) + _dot(c_hi, w_lo) + b_ref[...]


def _mod_call(c, ada_w, ada_b):
    bsz, d = c.shape
    n = ada_w.shape[1]
    tn = 1024
    return pl.pallas_call(
        _mod_kernel,
        grid=(n // tn,),
        in_specs=[pl.BlockSpec((bsz, d), lambda j: (0, 0)),
                  pl.BlockSpec((d, tn), lambda j: (0, j)),
                  pl.BlockSpec((1, tn), lambda j: (0, j))],
        out_specs=pl.BlockSpec((bsz, tn), lambda j: (0, j)),
        out_shape=jax.ShapeDtypeStruct((bsz, n), F32),
        compiler_params=pltpu.CompilerParams(dimension_semantics=("arbitrary",),
                                             vmem_limit_bytes=VMEM_LIMIT),
        name="adaln_mod",
    )(c, ada_w, ada_b.reshape(1, n))


def _token_shift(z, carry_ref):
    tm = z.shape[0]
    prev = pltpu.roll(z, 1, axis=0)
    row = lax.broadcasted_iota(I32, z.shape, 0)
    prev = jnp.where(row == 0, carry_ref[0:1, :], prev)
    carry_ref[0:1, :] = z[tm - 1:tm, :]
    return prev


def _pre_kernel(x_ref, mod_ref, g1_ref, wuv_ref, wrkv_ref, wl_ref, lnw_ref, lnb_ref, wsp_ref,
                bsx_ref, mu_rkv_ref, mu_l_ref, lora_w_ref, w0_ref, a0_ref, kk_ref, ka_ref, ones_ref,
                ya_ref, r_ref, k2_ref, v_ref, kkn_ref, al_ref, g_ref, lw_ref,
                carry_rkv, carry_l):
    tm = x_ref.shape[1]

    @pl.when(pl.program_id(1) == 0)
    def _():
        carry_rkv[...] = jnp.zeros_like(carry_rkv)
        carry_l[...] = jnp.zeros_like(carry_l)

    x = x_ref[0]
    shift1 = mod_ref[0, 0:1, :]
    scale1 = mod_ref[0, 1:2, :]
    h = _rms(x, g1_ref[...]) * (1.0 + scale1) + shift1
    hb = h.astype(BF16)
    zuv = _dot(hb, wuv_ref[...])
    zr = _dot(hb, wrkv_ref[...])
    zl = _dot(hb, wl_ref[...])

    u = jax.nn.gelu(zuv[:, :GMLP_WIDTH])
    v = jax.nn.gelu(zuv[:, GMLP_WIDTH:])
    m = jnp.mean(v, axis=-1, keepdims=True)
    var = jnp.mean(jnp.square(v - m), axis=-1, keepdims=True)
    vn = ((v - m) * lax.rsqrt(var + LN_EPS) * lnw_ref[...] + lnb_ref[...]).astype(BF16)
    lane = lax.broadcasted_iota(I32, (GMLP_CHUNK, LANES), 1)
    left = lane < HEAD_DIM
    trow = lax.broadcasted_iota(I32, (GMLP_CHUNK, 2 * GMLP_CHUNK), 0)
    tcol = lax.broadcasted_iota(I32, (GMLP_CHUNK, 2 * GMLP_CHUNK), 1) % GMLP_CHUNK
    causal = tcol <= trow
    zero = jnp.zeros((), BF16)
    for p in range(GMLP_WIDTH // LANES):
        wpair = jnp.where(causal, wsp_ref[p], 0.0).astype(BF16)
        for c in range(tm // GMLP_CHUNK):
            rows = slice(c * GMLP_CHUNK, (c + 1) * GMLP_CHUNK)
            cols = slice(p * LANES, (p + 1) * LANES)
            vc = vn[rows, cols]
            rhs = jnp.concatenate([jnp.where(left, vc, zero), jnp.where(left, zero, vc)], axis=0)
            mixed = _dot(wpair, rhs) + bsx_ref[:, cols]
            ya_ref[0, rows, cols] = (u[rows, cols] * mixed).astype(ya_ref.dtype)

    zr = zr + (_token_shift(zr, carry_rkv) - zr) * mu_rkv_ref[...]
    zl = zl + (_token_shift(zl, carry_l) - zl) * mu_l_ref[...]
    r = zr[:, :RWKV_WIDTH]
    k = zr[:, RWKV_WIDTH:2 * RWKV_WIDTH]
    vv = zr[:, 2 * RWKV_WIDTH:]
    llane = lax.broadcasted_iota(I32, zl.shape, 1)
    f = jnp.where(llane < DECAY_LORA, jnp.tanh(zl),
                  jnp.where(llane < DECAY_LORA + ICLR_LORA, zl, jax.nn.sigmoid(zl)))
    lo = _dot(f.astype(BF16), lora_w_ref[...])
    wpre = w0_ref[...] + lo[:, :RWKV_WIDTH]
    w_log = -(jnp.maximum(-wpre, 0.0) + jnp.log1p(jnp.exp(-jnp.abs(wpre)))) - 0.5
    lw_ref[0] = -jnp.exp(w_log)
    alpha = jax.nn.sigmoid(a0_ref[...] + lo[:, RWKV_WIDTH:2 * RWKV_WIDTH])
    g_ref[0] = lo[:, 2 * RWKV_WIDTH:].astype(g_ref.dtype)
    kkr = k * kk_ref[...]
    ss = _dot((kkr * kkr).astype(BF16), ones_ref[...])
    kkn_ref[0] = (kkr / jnp.maximum(jnp.sqrt(ss), 1e-12)).astype(kkn_ref.dtype)
    k2_ref[0] = (k * (1.0 + (alpha - 1.0) * ka_ref[...])).astype(k2_ref.dtype)
    r_ref[0] = r.astype(r_ref.dtype)
    v_ref[0] = vv.astype(v_ref.dtype)
    al_ref[0] = alpha.astype(al_ref.dtype)


def _head_ones(width):
    hid = jnp.arange(width) // HEAD_DIM
    return (hid[:, None] == hid[None, :]).astype(BF16)


def _pre_call(x, mod3, p):
    bsz, seq, d = x.shape
    tm = min(PRE_TM, seq)
    w_in = p["w_in"]
    wuv = w_in[:, :2 * GMLP_WIDTH].astype(BF16)
    wrkv = w_in[:, 2 * GMLP_WIDTH:2 * GMLP_WIDTH + 3 * RWKV_WIDTH].astype(BF16)
    n_lora = DECAY_LORA + ICLR_LORA + GATE_LORA
    wl = jnp.pad(w_in[:, 2 * GMLP_WIDTH + 3 * RWKV_WIDTH:], ((0, 0), (0, LORA_PAD - n_lora))).astype(BF16)
    mu = p["rwkv_mu"]
    mu_rkv = mu[:3 * RWKV_WIDTH].reshape(1, -1)
    mu_l = jnp.pad(mu[3 * RWKV_WIDTH:], (0, LORA_PAD - n_lora)).reshape(1, -1)
    zpad = lambda a, lo_, hi_: jnp.pad(a, ((lo_, LORA_PAD - hi_), (0, 0)))
    lora_w = jnp.concatenate([
        zpad(p["rwkv_w2"], 0, DECAY_LORA),
        zpad(p["rwkv_a2"], DECAY_LORA, DECAY_LORA + ICLR_LORA),
        zpad(p["rwkv_g2"], DECAY_LORA + ICLR_LORA, n_lora)], axis=1).astype(BF16)
    ws = p["gmlp_ws"]
    wsp = jnp.concatenate([ws[0::2], ws[1::2]], axis=2)
    bsx = jnp.repeat(p["gmlp_bs"].T, HEAD_DIM, axis=1)
    row = lambda a: a.reshape(1, -1)
    full = lambda a: pl.BlockSpec(a.shape, lambda b, s: (0,) * a.ndim)
    args = [mod3, row(p["norm1_g"]), wuv, wrkv, wl, row(p["gmlp_ln_w"]), row(p["gmlp_ln_b"]), wsp, bsx,
            mu_rkv, mu_l, lora_w, row(p["rwkv_w0"]), row(p["rwkv_a0"]), row(p["rwkv_k_k"]),
            row(p["rwkv_k_a"]), _head_ones(RWKV_WIDTH)]
    in_specs = [pl.BlockSpec((1, tm, d), lambda b, s: (b, s, 0)),
                pl.BlockSpec((1,) + mod3.shape[1:], lambda b, s: (b, 0, 0))]
    in_specs += [full(a) for a in args[1:]]
    tok = lambda dt: jax.ShapeDtypeStruct((bsz, seq, RWKV_WIDTH), dt)
    out_spec = pl.BlockSpec((1, tm, RWKV_WIDTH), lambda b, s: (b, s, 0))
    outs = [tok(BF16)] * 7 + [tok(F32)]
    return pl.pallas_call(
        _pre_kernel,
        grid=(bsz, seq // tm),
        in_specs=in_specs,
        out_specs=[out_spec] * 8,
        out_shape=outs,
        scratch_shapes=[pltpu.VMEM((8, 3 * RWKV_WIDTH), F32), pltpu.VMEM((8, LORA_PAD), F32)],
        compiler_params=pltpu.CompilerParams(dimension_semantics=("arbitrary", "arbitrary"),
                                             vmem_limit_bytes=VMEM_LIMIT),
        name="pre_mix",
    )(x, *args)


def _blockdiag(xb, mask):
    t = jnp.concatenate([xb] * HEADS_PER_GROUP, axis=0)
    return jnp.where(mask, t, jnp.zeros((), xb.dtype))


def _diag_blocks(full, lane_head):
    out = jnp.zeros((HEAD_DIM, MXU_DIM), F32)
    for hh in range(HEADS_PER_GROUP):
        out = out + jnp.where(lane_head == hh, full[hh * HEAD_DIM:(hh + 1) * HEAD_DIM, :], 0.0)
    return out


def _scan_kernel(x_ref, mod_ref, ya_ref, r_ref, k2_ref, v_ref, kk_ref, al_ref, g_ref, lw_ref,
                 rk_ref, gnw_ref, gnb_ref, ones_ref, wout_ref, g2n_ref, wr_ref, br_ref,
                 x1_ref, h2_ref, route_ref, slab_ref, state_ref, y_scr):
    tm = x_ref.shape[1]
    L = SCAN_L
    n_chunks = tm // L
    n_groups = RWKV_WIDTH // MXU_DIM

    @pl.when(pl.program_id(1) == 0)
    def _():
        state_ref[...] = jnp.zeros_like(state_ref)

    r = r_ref[0].astype(F32)
    k2 = k2_ref[0].astype(F32)
    v = v_ref[0].astype(F32)
    kk = kk_ref[0].astype(F32)
    b = kk * al_ref[0].astype(F32)
    lw = lw_ref[0]
    lw_hi = lw.astype(BF16)
    lw_lo = (lw - lw_hi.astype(F32)).astype(BF16)

    trow = lax.broadcasted_iota(I32, (L, L), 0)
    tcol = lax.broadcasted_iota(I32, (L, L), 1)
    tri = (tcol <= trow).astype(BF16)
    row = lax.broadcasted_iota(I32, (L, MXU_DIM), 0)
    lane = lax.broadcasted_iota(I32, (L, MXU_DIM), 1)
    lane_j = lane % HEAD_DIM
    lane_head = lane // HEAD_DIM
    strict = lane_j < row
    incl = lane_j <= row
    eye = lane_j == row
    brow = lax.broadcasted_iota(I32, (MXU_DIM, MXU_DIM), 0) // HEAD_DIM
    bcol = lax.broadcasted_iota(I32, (MXU_DIM, MXU_DIM), 1) // HEAD_DIM
    bmask = brow == bcol

    def bd(xf):
        return _blockdiag(xf.astype(BF16), bmask)

    probs = [(c, g) for c in range(n_chunks) for g in range(n_groups)]
    each = lambda fn, *lists: [fn(*a) for a in zip(*lists)]
    lgs = [_dot(tri, lw_hi[c * L:(c + 1) * L]) + _dot(tri, lw_lo[c * L:(c + 1) * L])
           for c in range(n_chunks)]

    def sl(arr, c, g):
        return arr[c * L:(c + 1) * L, g * MXU_DIM:(g + 1) * MXU_DIM]

    lgc = [lgs[c][:, g * MXU_DIM:(g + 1) * MXU_DIM] for c, g in probs]
    glast = [x[L - 1:L, :] for x in lgc]
    eg = [jnp.exp(x) for x in lgc]
    egx = [jnp.exp(x - sl(lw, c, g)) for x, (c, g) in zip(lgc, probs)]
    einv = [jnp.exp(-x) for x in lgc]
    ehat = each(lambda gl, x: jnp.exp(gl - x), glast, lgc)
    at = [-sl(kk, c, g) * e for e, (c, g) in zip(egx, probs)]
    rt = [sl(r, c, g) * e for e, (c, g) in zip(eg, probs)]
    bt = [sl(b, c, g) * e for e, (c, g) in zip(einv, probs)]
    kt = [sl(k2, c, g) * e for e, (c, g) in zip(einv, probs)]
    bh = [sl(b, c, g) * e for e, (c, g) in zip(ehat, probs)]
    kh = [sl(k2, c, g) * e for e, (c, g) in zip(ehat, probs)]
    vc = [sl(v, c, g) for c, g in probs]
    lhs = each(lambda a_, r_: jnp.concatenate([a_, r_], axis=0).astype(BF16), at, rt)
    pb = each(lambda l_, x: _dot_nt(l_, bd(x)), lhs, bt)
    pk = each(lambda l_, x: _dot_nt(l_, bd(x)), lhs, kt)
    a_ab = [jnp.where(strict, x[:L], 0.0) for x in pb]
    a_rb = [jnp.where(incl, x[L:], 0.0) for x in pb]
    a_ak = [jnp.where(strict, x[:L], 0.0) for x in pk]
    a_rk = [jnp.where(incl, x[L:], 0.0) for x in pk]
    ssum = [jnp.where(eye, 1.0, 0.0) + x for x in a_ab]
    xb = [x.astype(BF16) for x in a_ab]
    xp = [_dot(x, _blockdiag(x, bmask)) for x in xb]
    m = 2
    while m < L // 2:
        xb = [x.astype(BF16) for x in xp]
        res = each(lambda x, s_: _dot(jnp.concatenate([x, s_.astype(BF16)], axis=0), _blockdiag(x, bmask)),
                   xb, ssum)
        xp = [x[:L] for x in res]
        ssum = each(lambda s_, x: s_ + x[L:], ssum, res)
        m *= 2
    tmat = each(lambda s_, x: s_ + _dot(s_.astype(BF16), bd(x)), ssum, xp)
    res = each(lambda a1, a2, x: _dot(jnp.concatenate([a1, a2], axis=0).astype(BF16), bd(x)), a_ak, a_rk, vc)
    akv = [x[:L] for x in res]
    arkv = [x[L:] for x in res]
    tb = [x.astype(BF16) for x in tmat]
    wm = each(lambda t_, x: _dot(t_, bd(x)), tb, at)
    u0 = each(lambda t_, x: _dot(t_, bd(x)), tb, akv)
    arb = [x.astype(BF16) for x in a_rb]
    qp = each(lambda r_, a_, x: r_ + _dot(a_, bd(x)), rt, arb, wm)
    y0 = each(lambda a_, x, z: _dot(a_, bd(x)) + z, arb, u0, arkv)
    bht = [x.T for x in bh]
    kht = [x.T for x in kh]
    full_m = each(lambda b_, w_: _dot(b_.astype(BF16), w_.astype(BF16)), bht, wm)
    full_n = each(lambda b_, k_, u_, v_: _dot(jnp.concatenate([b_, k_], axis=1).astype(BF16),
                                              jnp.concatenate([u_, v_], axis=0).astype(BF16)),
                  bht, kht, u0, vc)
    mt = each(lambda f_, gl: _diag_blocks(f_, lane_head) + jnp.where(eye, jnp.exp(gl), 0.0), full_m, glast)
    nt = [_diag_blocks(f_, lane_head) for f_ in full_n]
    lhs2 = each(lambda q_, m_: jnp.concatenate([q_, m_], axis=0).astype(BF16), qp, mt)

    st = [state_ref[g] for g in range(n_groups)]
    for c in range(n_chunks):
        for g in range(n_groups):
            i = c * n_groups + g
            res_s = _dot(lhs2[i], bd(st[g]))
            y_scr[c * L:(c + 1) * L, g * MXU_DIM:(g + 1) * MXU_DIM] = res_s[:L] + y0[i]
            st[g] = res_s[L:] + nt[i]
    for g in range(n_groups):
        state_ref[g] = st[g]

    y = y_scr[...]
    ones = ones_ref[...]
    inv_n = 1.0 / HEAD_DIM
    mean = _dot(y.astype(BF16), ones) * inv_n
    dlt = y - mean
    var = _dot((dlt * dlt).astype(BF16), ones) * inv_n
    yn = dlt * lax.rsqrt(var + GN_EPS) * gnw_ref[...] + gnb_ref[...]
    bonus = _dot((r * k2 * rk_ref[...]).astype(BF16), ones) * v
    yb = (yn + bonus) * g_ref[0].astype(F32)
    cat = jnp.concatenate([ya_ref[0], yb.astype(BF16)], axis=1)
    y2 = _dot(cat, wout_ref[...])
    gate1 = mod_ref[0, 2:3, :]
    shift2 = mod_ref[0, 3:4, :]
    scale2 = mod_ref[0, 4:5, :]
    x1 = x_ref[0] + gate1 * y2
    x1_ref[0] = x1
    h2 = _rms(x1, g2n_ref[...]) * (1.0 + scale2) + shift2
    _store_row_tiles(h2_ref.at[0], h2)

    logits = _dot(h2.astype(BF16), wr_ref[...]) + br_ref[...]
    ln = lax.broadcasted_iota(I32, logits.shape, 1)
    neg = jnp.float32(-jnp.inf)
    is_g = ln < N_GROUPS
    gl = jnp.where(is_g, logits, neg)
    gmax = jnp.max(gl, axis=-1, keepdims=True)
    gidx = jnp.min(jnp.where(gl == gmax, ln, LANES), axis=-1, keepdims=True)
    g_p = 1.0 / jnp.sum(jnp.where(is_g, jnp.exp(logits - gmax), 0.0), axis=-1, keepdims=True)
    egrp = (ln - N_GROUPS) // EXPERTS_PER_GROUP
    sel = jnp.where(ln >= N_GROUPS, egrp, -1) == gidx
    el = jnp.where(sel, logits, neg)
    l1 = jnp.max(el, axis=-1, keepdims=True)
    i1 = jnp.min(jnp.where(el == l1, ln, LANES), axis=-1, keepdims=True)
    el2 = jnp.where(ln == i1, neg, el)
    l2 = jnp.max(el2, axis=-1, keepdims=True)
    i2 = jnp.min(jnp.where(el2 == l2, ln, LANES), axis=-1, keepdims=True)
    t = jnp.exp(l2 - l1)
    p1 = 1.0 / (1.0 + t)
    w1 = g_p * p1
    w2 = g_p * (t * p1)
    e1 = (i1 - N_GROUPS).astype(F32)
    e2 = (i2 - N_GROUPS).astype(F32)
    slab = jnp.where(ln == 0, e1, jnp.where(ln == 1, e2, jnp.where(ln == 2, w1, jnp.where(ln == 3, w2, 0.0))))
    slab_ref[...] = slab
    route_ref[...] = slab.T[:ROUTE_ROWS, :]


def _scan_call(x, mod3, pre_outs, p):
    bsz, seq, d = x.shape
    tm = min(SCAN_TM, seq)
    n_tok = bsz * seq
    steps = seq // tm
    n_route = N_GROUPS + N_EXPERTS
    wr = jnp.pad(jnp.concatenate([p["router_group_w"], p["router_expert_w"]], axis=1),
                 ((0, 0), (0, LANES - n_route))).astype(BF16)
    br = jnp.pad(jnp.concatenate([p["router_group_b"], p["router_expert_b"]]), (0, LANES - n_route)).reshape(1, -1)
    row = lambda a: a.reshape(1, -1)
    full = lambda a: pl.BlockSpec(a.shape, lambda b, s: (0,) * a.ndim)
    params = [row(p["rwkv_r_k"]), row(p["rwkv_gn_w"]), row(p["rwkv_gn_b"]), _head_ones(RWKV_WIDTH),
              p["w_out"].astype(BF16), row(p["norm2_g"]), wr, br]
    tok_spec = pl.BlockSpec((1, tm, RWKV_WIDTH), lambda b, s: (b, s, 0))
    x_spec = pl.BlockSpec((1, tm, d), lambda b, s: (b, s, 0))
    in_specs = [x_spec, pl.BlockSpec((1,) + mod3.shape[1:], lambda b, s: (b, 0, 0))]
    in_specs += [tok_spec] * 8 + [full(a) for a in params]
    return pl.pallas_call(
        _scan_kernel,
        grid=(bsz, steps),
        in_specs=in_specs,
        out_specs=[x_spec, pl.BlockSpec((1, tm * ROW_SUB, LANES), lambda b, s: (b, s, 0)),
                   pl.BlockSpec((ROUTE_ROWS, tm), lambda b, s: (0, b * steps + s)),
                   pl.BlockSpec((tm, LANES), lambda b, s: (b * steps + s, 0))],
        out_shape=[jax.ShapeDtypeStruct((bsz, seq, d), F32),
                   jax.ShapeDtypeStruct((bsz, seq * ROW_SUB, LANES), F32),
                   jax.ShapeDtypeStruct((ROUTE_ROWS, n_tok), F32), jax.ShapeDtypeStruct((n_tok, LANES), F32)],
        scratch_shapes=[pltpu.VMEM((RWKV_WIDTH // MXU_DIM, HEAD_DIM, MXU_DIM), F32),
                        pltpu.VMEM((tm, RWKV_WIDTH), F32)],
        compiler_params=pltpu.CompilerParams(dimension_semantics=("arbitrary", "arbitrary"),
                                             vmem_limit_bytes=VMEM_LIMIT),
        name="scan_post",
    )(x, mod3, *pre_outs, *params)


def _mix_kernel(x_ref, mod_ref, ya_ref, g_ref, r_ref, k2_ref, v_ref, kk_ref, al_ref, lw_ref,
                rk_ref, gnw_ref, gnb_ref, ones_ref, wout_ref, g2n_ref, wr_ref, br_ref,
                x1_ref, h2_ref, route_ref, slab_ref,
                state_ref, y_scr, lhs_scr, y0_scr, nt_scr, bonus_scr):
    tm = x_ref.shape[1]
    L = SCAN_L
    n_chunks = tm // L
    n_groups = RWKV_WIDTH // MXU_DIM
    n_p = n_chunks * n_groups
    s = pl.program_id(1)
    cur = (s % 2) * n_p
    prev = n_p - cur

    @pl.when(jnp.logical_and(pl.program_id(0) == 0, s == 0))
    def _():
        lhs_scr[...] = jnp.zeros_like(lhs_scr)
        y0_scr[...] = jnp.zeros_like(y0_scr)
        nt_scr[...] = jnp.zeros_like(nt_scr)
        bonus_scr[...] = jnp.zeros_like(bonus_scr)
        state_ref[...] = jnp.zeros_like(state_ref)

    @pl.when(s == 1)
    def _():
        state_ref[...] = jnp.zeros_like(state_ref)

    r = r_ref[0].astype(F32)
    k2 = k2_ref[0].astype(F32)
    v = v_ref[0].astype(F32)
    kk = kk_ref[0].astype(F32)
    b = kk * al_ref[0].astype(F32)
    lw = lw_ref[0]
    lw_hi = lw.astype(BF16)
    lw_lo = (lw - lw_hi.astype(F32)).astype(BF16)
    ones = ones_ref[...]

    trow = lax.broadcasted_iota(I32, (L, L), 0)
    tcol = lax.broadcasted_iota(I32, (L, L), 1)
    tri = (tcol <= trow).astype(BF16)
    row = lax.broadcasted_iota(I32, (L, MXU_DIM), 0)
    lane = lax.broadcasted_iota(I32, (L, MXU_DIM), 1)
    lane_j = lane % HEAD_DIM
    lane_head = lane // HEAD_DIM
    strict = lane_j < row
    incl = lane_j <= row
    eye = lane_j == row
    brow = lax.broadcasted_iota(I32, (MXU_DIM, MXU_DIM), 0) // HEAD_DIM
    bcol = lax.broadcasted_iota(I32, (MXU_DIM, MXU_DIM), 1) // HEAD_DIM
    bmask = brow == bcol

    def bd(xf):
        return _blockdiag(xf.astype(BF16), bmask)

    st = [state_ref[g] for g in range(n_groups)]

    def chain_step(c):
        for g in range(n_groups):
            i = prev + c * n_groups + g
            res = _dot(lhs_scr[i], bd(st[g]))
            y_scr[c * L:(c + 1) * L, g * MXU_DIM:(g + 1) * MXU_DIM] = res[:L] + y0_scr[i]
            st[g] = res[L:] + nt_scr[i]

    chain_at = {}
    for c in range(n_chunks):
        chain_at[2 * c + 1] = c

    def after_stage(k):
        if k in chain_at:
            chain_step(chain_at[k])

    probs = [(c, g) for c in range(n_chunks) for g in range(n_groups)]
    each = lambda fn, *lists: [fn(*a) for a in zip(*lists)]

    def sl(arr, c, g):
        return arr[c * L:(c + 1) * L, g * MXU_DIM:(g + 1) * MXU_DIM]

    lgs = [_dot(tri, lw_hi[c * L:(c + 1) * L]) + _dot(tri, lw_lo[c * L:(c + 1) * L])
           for c in range(n_chunks)]
    lgc = [lgs[c][:, g * MXU_DIM:(g + 1) * MXU_DIM] for c, g in probs]
    glast = [x_[L - 1:L, :] for x_ in lgc]
    eg = [jnp.exp(x_) for x_ in lgc]
    egx = [jnp.exp(x_ - sl(lw, c, g)) for x_, (c, g) in zip(lgc, probs)]
    einv = [jnp.exp(-x_) for x_ in lgc]
    ehat = each(lambda gl, x_: jnp.exp(gl - x_), glast, lgc)
    at = [-sl(kk, c, g) * e for e, (c, g) in zip(egx, probs)]
    rt = [sl(r, c, g) * e for e, (c, g) in zip(eg, probs)]
    bt = [sl(b, c, g) * e for e, (c, g) in zip(einv, probs)]
    kt = [sl(k2, c, g) * e for e, (c, g) in zip(einv, probs)]
    bh = [sl(b, c, g) * e for e, (c, g) in zip(ehat, probs)]
    kh = [sl(k2, c, g) * e for e, (c, g) in zip(ehat, probs)]
    vc = [sl(v, c, g) for c, g in probs]
    lhs = each(lambda a_, r_: jnp.concatenate([a_, r_], axis=0).astype(BF16), at, rt)
    pb = each(lambda l_, x_: _dot_nt(l_, bd(x_)), lhs, bt)
    after_stage(1)
    pk = each(lambda l_, x_: _dot_nt(l_, bd(x_)), lhs, kt)
    a_ab = [jnp.where(strict, x_[:L], 0.0) for x_ in pb]
    a_rb = [jnp.where(incl, x_[L:], 0.0) for x_ in pb]
    a_ak = [jnp.where(strict, x_[:L], 0.0) for x_ in pk]
    a_rk = [jnp.where(incl, x_[L:], 0.0) for x_ in pk]
    ssum = [jnp.where(eye, 1.0, 0.0) + x_ for x_ in a_ab]
    xb = [x_.astype(BF16) for x_ in a_ab]
    xp = [_dot(x_, _blockdiag(x_, bmask)) for x_ in xb]
    stage = 3
    after_stage(stage)
    m = 2
    while m < L // 2:
        xb = [x_.astype(BF16) for x_ in xp]
        res = each(lambda x_, s_: _dot(jnp.concatenate([x_, s_.astype(BF16)], axis=0), _blockdiag(x_, bmask)),
                   xb, ssum)
        xp = [x_[:L] for x_ in res]
        ssum = each(lambda s_, x_: s_ + x_[L:], ssum, res)
        m *= 2
        stage += 1
        after_stage(stage)
    tmat = each(lambda s_, x_: s_ + _dot(s_.astype(BF16), bd(x_)), ssum, xp)
    for c in range(n_chunks):
        if 2 * c + 1 > stage:
            chain_step(c)
    for g in range(n_groups):
        state_ref[g] = st[g]

    y = y_scr[...]
    inv_n = 1.0 / HEAD_DIM
    mean = _dot(y.astype(BF16), ones) * inv_n
    res = each(lambda a1, a2, x_: _dot(jnp.concatenate([a1, a2], axis=0).astype(BF16), bd(x_)), a_ak, a_rk, vc)
    akv = [x_[:L] for x_ in res]
    arkv = [x_[L:] for x_ in res]
    dlt = y - mean
    var = _dot((dlt * dlt).astype(BF16), ones) * inv_n
    tb = [x_.astype(BF16) for x_ in tmat]
    wm = each(lambda t_, x_: _dot(t_, bd(x_)), tb, at)
    u0 = each(lambda t_, x_: _dot(t_, bd(x_)), tb, akv)

    yn = dlt * lax.rsqrt(var + GN_EPS) * gnw_ref[...] + gnb_ref[...]
    yb = (yn + bonus_scr[(s + 1) % 2]) * g_ref[0].astype(F32)
    cat = jnp.concatenate([ya_ref[0], yb.astype(BF16)], axis=1)
    y2 = _dot(cat, wout_ref[...])
    gate1 = mod_ref[0, 2:3, :]
    shift2 = mod_ref[0, 3:4, :]
    scale2 = mod_ref[0, 4:5, :]
    x1 = x_ref[0] + gate1 * y2
    x1_ref[0] = x1
    h2 = _rms(x1, g2n_ref[...]) * (1.0 + scale2) + shift2
    _store_row_tiles(h2_ref.at[0], h2)

    arb = [x_.astype(BF16) for x_ in a_rb]
    qp = each(lambda r_, a_, x_: r_ + _dot(a_, bd(x_)), rt, arb, wm)
    y0 = each(lambda a_, x_, z_: _dot(a_, bd(x_)) + z_, arb, u0, arkv)

    logits = _dot(h2.astype(BF16), wr_ref[...]) + br_ref[...]
    slab = _route_slab(logits)
    slab_ref[...] = slab
    route_ref[...] = slab.T[:ROUTE_ROWS, :]

    bht = [x_.T for x_ in bh]
    kht = [x_.T for x_ in kh]
    full_m = each(lambda b_, w_: _dot(b_.astype(BF16), w_.astype(BF16)), bht, wm)
    full_n = each(lambda b_, k_, u_, v_: _dot(jnp.concatenate([b_, k_], axis=1).astype(BF16),
                                              jnp.concatenate([u_, v_], axis=0).astype(BF16)),
                  bht, kht, u0, vc)
    mt = each(lambda f_, gl: _diag_blocks(f_, lane_head) + jnp.where(eye, jnp.exp(gl), 0.0), full_m, glast)
    nt = [_diag_blocks(f_, lane_head) for f_ in full_n]
    bonus_scr[s % 2] = _dot((r * k2 * rk_ref[...]).astype(BF16), ones) * v
    for i in range(n_p):
        lhs_scr[cur + i] = jnp.concatenate([qp[i], mt[i]], axis=0).astype(BF16)
        y0_scr[cur + i] = y0[i]
        nt_scr[cur + i] = nt[i]


def _route_slab(logits):
    ln = lax.broadcasted_iota(I32, logits.shape, 1)
    neg = jnp.float32(-jnp.inf)
    is_g = ln < N_GROUPS
    gl = jnp.where(is_g, logits, neg)
    gmax = jnp.max(gl, axis=-1, keepdims=True)
    gidx = jnp.min(jnp.where(gl == gmax, ln, LANES), axis=-1, keepdims=True)
    g_p = 1.0 / jnp.sum(jnp.where(is_g, jnp.exp(logits - gmax), 0.0), axis=-1, keepdims=True)
    egrp = (ln - N_GROUPS) // EXPERTS_PER_GROUP
    sel = jnp.where(ln >= N_GROUPS, egrp, -1) == gidx
    el = jnp.where(sel, logits, neg)
    l1 = jnp.max(el, axis=-1, keepdims=True)
    i1 = jnp.min(jnp.where(el == l1, ln, LANES), axis=-1, keepdims=True)
    el2 = jnp.where(ln == i1, neg, el)
    l2 = jnp.max(el2, axis=-1, keepdims=True)
    i2 = jnp.min(jnp.where(el2 == l2, ln, LANES), axis=-1, keepdims=True)
    t = jnp.exp(l2 - l1)
    p1 = 1.0 / (1.0 + t)
    w1 = g_p * p1
    w2 = g_p * (t * p1)
    e1 = (i1 - N_GROUPS).astype(F32)
    e2 = (i2 - N_GROUPS).astype(F32)
    return jnp.where(ln == 0, e1, jnp.where(ln == 1, e2, jnp.where(ln == 2, w1, jnp.where(ln == 3, w2, 0.0))))


def _mix_call(x, mod3, pre_outs, p):
    bsz, seq, d = x.shape
    tm = min(SCAN_TM, seq)
    n_tok = bsz * seq
    steps = seq // tm
    n_p = (tm // SCAN_L) * (RWKV_WIDTH // MXU_DIM)
    n_route = N_GROUPS + N_EXPERTS
    wr = jnp.pad(jnp.concatenate([p["router_group_w"], p["router_expert_w"]], axis=1),
                 ((0, 0), (0, LANES - n_route))).astype(BF16)
    br = jnp.pad(jnp.concatenate([p["router_group_b"], p["router_expert_b"]]), (0, LANES - n_route)).reshape(1, -1)
    row = lambda a: a.reshape(1, -1)
    full = lambda a: pl.BlockSpec(a.shape, lambda b, s: (0,) * a.ndim)
    params = [row(p["rwkv_r_k"]), row(p["rwkv_gn_w"]), row(p["rwkv_gn_b"]), _head_ones(RWKV_WIDTH),
              p["w_out"].astype(BF16), row(p["norm2_g"]), wr, br]
    ya, r, k2, v, kkn, al, g, lw = pre_outs
    before = lambda s: jnp.maximum(s - 1, 0)
    here = lambda s: jnp.minimum(s, steps - 1)
    tok_prev = pl.BlockSpec((1, tm, RWKV_WIDTH), lambda b, s: (b, before(s), 0))
    tok_cur = pl.BlockSpec((1, tm, RWKV_WIDTH), lambda b, s: (b, here(s), 0))
    x_spec = pl.BlockSpec((1, tm, d), lambda b, s: (b, before(s), 0))
    in_specs = [x_spec, pl.BlockSpec((1,) + mod3.shape[1:], lambda b, s: (b, 0, 0)), tok_prev, tok_prev]
    in_specs += [tok_cur] * 6 + [full(a) for a in params]
    return pl.pallas_call(
        _mix_kernel,
        grid=(bsz, steps + 1),
        in_specs=in_specs,
        out_specs=[x_spec, pl.BlockSpec((1, tm * ROW_SUB, LANES), lambda b, s: (b, before(s), 0)),
                   pl.BlockSpec((ROUTE_ROWS, tm), lambda b, s: (0, b * steps + before(s))),
                   pl.BlockSpec((tm, LANES), lambda b, s: (b * steps + before(s), 0))],
        out_shape=[jax.ShapeDtypeStruct((bsz, seq, d), F32),
                   jax.ShapeDtypeStruct((bsz, seq * ROW_SUB, LANES), F32),
                   jax.ShapeDtypeStruct((ROUTE_ROWS, n_tok), F32), jax.ShapeDtypeStruct((n_tok, LANES), F32)],
        scratch_shapes=[pltpu.VMEM((RWKV_WIDTH // MXU_DIM, HEAD_DIM, MXU_DIM), F32),
                        pltpu.VMEM((tm, RWKV_WIDTH), F32),
                        pltpu.VMEM((2 * n_p, 2 * SCAN_L, MXU_DIM), BF16),
                        pltpu.VMEM((2 * n_p, SCAN_L, MXU_DIM), F32),
                        pltpu.VMEM((2 * n_p, SCAN_L, MXU_DIM), F32),
                        pltpu.VMEM((2, tm, RWKV_WIDTH), F32)],
        compiler_params=pltpu.CompilerParams(dimension_semantics=("arbitrary", "arbitrary"),
                                             vmem_limit_bytes=VMEM_LIMIT),
        name="scan_post",
    )(x, mod3, ya, g, r, k2, v, kkn, al, lw, *params)


def _onehots(route_ref):
    t = route_ref.shape[1]
    erow = lax.broadcasted_iota(I32, (N_EXPERTS, t), 0)
    oh1 = erow == route_ref[0:1, :].astype(I32)
    oh2 = erow == route_ref[1:2, :].astype(I32)
    return oh1, oh2


def _count_kernel(route_ref, cnt_ref):
    @pl.when(pl.program_id(0) == 0)
    def _():
        cnt_ref[...] = jnp.zeros_like(cnt_ref)

    oh1, oh2 = _onehots(route_ref)
    cnt = oh1.astype(F32) + oh2.astype(F32)
    cnt_ref[...] += jnp.sum(cnt, axis=1, keepdims=True)


def _dest_kernel(route_ref, cnt_ref, dest_ref, meta_ref, base_ref):
    t = route_ref.shape[1]
    nbp = meta_ref.shape[1]

    @pl.when(pl.program_id(0) == 0)
    def _():
        cnt = cnt_ref[...]
        padded = jnp.floor((cnt + (MOE_BM - 1)) * (1.0 / MOE_BM)) * MOE_BM
        er = lax.broadcasted_iota(I32, (N_EXPERTS, N_EXPERTS), 0)
        ec = lax.broadcasted_iota(I32, (N_EXPERTS, N_EXPERTS), 1)
        lower = (ec < er).astype(BF16)
        p_hi = padded.astype(BF16)
        p_lo = (padded - p_hi.astype(F32)).astype(BF16)
        pstart = _dot(lower, p_hi) + _dot(lower, p_lo)
        base_ref[...] = pstart
        pend = (pstart + padded)[:, 0:1]
        blk0 = (lax.broadcasted_iota(I32, (N_EXPERTS, nbp), 1) * MOE_BM).astype(F32)
        be = jnp.sum((pend <= blk0).astype(I32), axis=0, keepdims=True)
        meta_ref[0:1, :] = jnp.minimum(be, N_EXPERTS - 1)
        used = (pend[N_EXPERTS - 1:N_EXPERTS, :] * (1.0 / MOE_BM)).astype(I32)
        meta_ref[1:2, :] = jnp.broadcast_to(used, (1, nbp))
        on_diag = (lax.broadcasted_iota(I32, (N_EXPERTS, nbp), 0)
                   == lax.broadcasted_iota(I32, (N_EXPERTS, nbp), 1))
        pad_lo = (pstart + cnt)[:, 0:1]
        pad_len = (padded - cnt)[:, 0:1]
        meta_ref[2:3, :] = jnp.sum(jnp.where(on_diag, pad_lo, 0.0), axis=0, keepdims=True).astype(I32)
        meta_ref[3:4, :] = jnp.sum(jnp.where(on_diag, pad_len, 0.0), axis=0, keepdims=True).astype(I32)
        e_id = lax.broadcasted_iota(I32, (N_EXPERTS, nbp), 0)
        later = jnp.logical_and(e_id > jnp.minimum(be, N_EXPERTS - 1), padded[:, 0:1] > 0.0)
        nxt = jnp.min(jnp.where(later, e_id, N_EXPERTS), axis=0, keepdims=True)
        meta_ref[4:5, :] = jnp.where(nxt == N_EXPERTS, -1, nxt)
        meta_ref[5:, :] = jnp.zeros((meta_ref.shape[0] - 5, nbp), I32)

    oh1, oh2 = _onehots(route_ref)
    cnt = oh1.astype(F32) + oh2.astype(F32)
    sr = lax.broadcasted_iota(I32, (t, t), 0)
    sc = lax.broadcasted_iota(I32, (t, t), 1)
    upper = (sr < sc).astype(BF16)
    pos = _dot(cnt.astype(BF16), upper) + base_ref[:, 0:1]
    dest_ref[0:1, :] = jnp.sum(jnp.where(oh1, pos, 0.0), axis=0, keepdims=True).astype(I32)
    dest_ref[1:2, :] = jnp.sum(jnp.where(oh2, pos, 0.0), axis=0, keepdims=True).astype(I32)
    base_ref[...] += jnp.sum(cnt, axis=1, keepdims=True)


def _n_blocks(n_tok):
    return -(-2 * n_tok // MOE_BM) + N_EXPERTS


def _route_call(route):
    n_tok = route.shape[1]
    t = min(ROUTE_T, n_tok)
    nbp = -(-_n_blocks(n_tok) // LANES) * LANES
    cparams = pltpu.CompilerParams(dimension_semantics=("arbitrary",), vmem_limit_bytes=VMEM_LIMIT)
    route_spec = pl.BlockSpec((ROUTE_ROWS, t), lambda j: (0, j))
    cnt_spec = pl.BlockSpec((N_EXPERTS, LANES), lambda j: (0, 0))
    counts = pl.pallas_call(
        _count_kernel, grid=(n_tok // t,), in_specs=[route_spec], out_specs=cnt_spec,
        out_shape=jax.ShapeDtypeStruct((N_EXPERTS, LANES), F32), compiler_params=cparams,
        name="route_count")(route)
    dest, meta = pl.pallas_call(
        _dest_kernel, grid=(n_tok // t,), in_specs=[route_spec, cnt_spec],
        out_specs=[pl.BlockSpec((2, t), lambda j: (0, j)), pl.BlockSpec((8, nbp), lambda j: (0, 0))],
        out_shape=[jax.ShapeDtypeStruct((2, n_tok), I32), jax.ShapeDtypeStruct((8, nbp), I32)],
        scratch_shapes=[pltpu.VMEM((N_EXPERTS, LANES), F32)], compiler_params=cparams,
        name="route_dest")(route, counts)
    return dest, meta


def _zero_fill_slots(meta_ref, xs_hbm, ztile, zsem, wait):
    nbp = meta_ref.shape[0] // 8
    n_blocks = xs_hbm.shape[0] // (MOE_BM * ROW_SUB)

    def go(rows, first_slot):
        cp = pltpu.make_async_copy(
            ztile.at[pl.ds(0, rows * ROW_SUB), :],
            xs_hbm.at[pl.ds(pl.multiple_of(first_slot * ROW_SUB, ROW_SUB), rows * ROW_SUB), :], zsem)
        if wait:
            cp.wait()
        else:
            cp.start()

    def expert_pad(e, carry):
        slot = meta_ref[2 * nbp + e]
        plen = meta_ref[3 * nbp + e]
        bit = MOE_BM // 2
        while bit >= 1:
            has = (plen & bit) != 0
            pl.when(has)(functools.partial(go, bit, slot))
            slot = slot + jnp.where(has, bit, 0)
            bit //= 2
        return carry

    def tail_block(j, carry):
        go(MOE_BM, j * MOE_BM)
        return carry

    lax.fori_loop(0, N_EXPERTS, expert_pad, 0)
    lax.fori_loop(meta_ref[nbp], n_blocks, tail_block, 0)


def _dispatch_kernel(dest_ref, meta_ref, h_ref, xs_hbm, sem, zsem, ztile):
    n_tok = dest_ref.shape[0] // 2
    t = h_ref.shape[0] // ROW_SUB
    base = pl.program_id(0) * t
    first = pl.program_id(0) == 0

    @pl.when(first)
    def _():
        ztile[...] = jnp.zeros_like(ztile)
        _zero_fill_slots(meta_ref, xs_hbm, ztile, zsem, wait=False)

    def start(i, carry):
        tok = base + i
        src = _row_tile(h_ref, i)
        pltpu.make_async_copy(src, _row_tile(xs_hbm, dest_ref[tok]), sem).start(priority=0)
        pltpu.make_async_copy(src, _row_tile(xs_hbm, dest_ref[n_tok + tok]), sem).start(priority=1)
        return carry

    lax.fori_loop(0, t, start, 0, unroll=DMA_UNROLL)
    for _ in range(2):
        pltpu.make_async_copy(h_ref, xs_hbm.at[pl.ds(0, t * ROW_SUB), :], sem).wait()

    @pl.when(first)
    def _():
        _zero_fill_slots(meta_ref, xs_hbm, ztile, zsem, wait=True)


def _dispatch_call(dest_flat, meta_flat, h_tok, n_slots):
    n_tok = h_tok.shape[0] // ROW_SUB
    t = min(DISPATCH_T, n_tok)
    return pl.pallas_call(
        _dispatch_kernel,
        grid_spec=pltpu.PrefetchScalarGridSpec(
            num_scalar_prefetch=2, grid=(n_tok // t,),
            in_specs=[pl.BlockSpec((t * ROW_SUB, LANES), lambda i, dr, mr: (i, 0))],
            out_specs=pl.BlockSpec(memory_space=pl.ANY),
            scratch_shapes=[pltpu.SemaphoreType.DMA, pltpu.SemaphoreType.DMA,
                            pltpu.VMEM((MOE_BM * ROW_SUB, LANES), h_tok.dtype)]),
        out_shape=jax.ShapeDtypeStruct((n_slots * ROW_SUB, LANES), h_tok.dtype),
        compiler_params=pltpu.CompilerParams(dimension_semantics=("arbitrary",),
                                             vmem_limit_bytes=VMEM_LIMIT),
        name="moe_dispatch",
    )(dest_flat, meta_flat, h_tok)


def _expert_kernel(meta_ref, xs_ref, wg_hbm, wu_hbm, wd_hbm, yb_ref, wgf, wuf, wdf, wgb, wub, wdb,
                   wsem, slot_ref):
    j = pl.program_id(0)
    nbp = meta_ref.shape[0] // 8
    used = j < meta_ref[nbp]
    new_expert = jnp.logical_or(j == 0, meta_ref[j] != meta_ref[jnp.maximum(j - 1, 0)])

    def fetch(e, slot):
        return [pltpu.make_async_copy(src.at[e], dst.at[slot], wsem.at[slot])
                for src, dst in ((wg_hbm, wgf), (wu_hbm, wuf), (wd_hbm, wdf))]

    @pl.when(j == 0)
    def _():
        slot_ref[0] = 0
        for cp in fetch(meta_ref[0], 0):
            cp.start()

    @pl.when(jnp.logical_and(used, new_expert))
    def _():
        slot = slot_ref[0]
        for cp in fetch(meta_ref[j], slot):
            cp.wait()
        nxt = meta_ref[4 * nbp + j]

        @pl.when(nxt >= 0)
        def _():
            for cp in fetch(nxt, 1 - slot):
                cp.start()

        wgb[...] = wgf[slot].astype(BF16)
        wub[...] = wuf[slot].astype(BF16)
        wdb[...] = wdf[slot].astype(BF16)
        slot_ref[0] = 1 - slot

    @pl.when(used)
    def _():
        x = _load_row_tiles(xs_ref).astype(BF16)
        gate = _dot(x, wgb[...])
        up = _dot(x, wub[...])
        hid = (jax.nn.silu(gate) * up).astype(BF16)
        _store_row_tiles(yb_ref, _dot(hid, wdb[...]))

    @pl.when(jnp.logical_not(used))
    def _():
        yb_ref[...] = jnp.zeros_like(yb_ref)


def _expert_call(meta_flat, xs, wg, wu, wd):
    n_slots = xs.shape[0] // ROW_SUB
    d, hid = wg.shape[1:]
    blk = pl.BlockSpec((MOE_BM * ROW_SUB, LANES), lambda j, m: (j, 0))
    return pl.pallas_call(
        _expert_kernel,
        grid_spec=pltpu.PrefetchScalarGridSpec(
            num_scalar_prefetch=1, grid=(n_slots // MOE_BM,),
            in_specs=[blk] + [pl.BlockSpec(memory_space=pl.ANY)] * 3,
            out_specs=blk,
            scratch_shapes=[pltpu.VMEM((2, d, hid), F32), pltpu.VMEM((2, d, hid), F32),
                            pltpu.VMEM((2, hid, d), F32),
                            pltpu.VMEM((d, hid), BF16), pltpu.VMEM((d, hid), BF16),
                            pltpu.VMEM((hid, d), BF16),
                            pltpu.SemaphoreType.DMA((2,)), pltpu.SMEM((1,), I32)]),
        out_shape=jax.ShapeDtypeStruct((n_slots * ROW_SUB, LANES), F32),
        compiler_params=pltpu.CompilerParams(dimension_semantics=("arbitrary",),
                                             vmem_limit_bytes=VMEM_LIMIT),
        name="moe_experts",
    )(meta_flat, xs, wg, wu, wd)


def _combine_kernel(dest_ref, yb_hbm, slab_ref, x1_ref, mod_ref, gf_ref, out_ref, gbuf, sem):
    t = x1_ref.shape[0]
    n_tok = dest_ref.shape[0] // 2
    step = pl.program_id(0)
    slot = step % 2

    def issue(tile, buf):
        def body(r, carry):
            tok = tile * t + r
            pltpu.make_async_copy(_row_tile(yb_hbm, dest_ref[tok]), _row_tile(gbuf.at[buf, 0], r),
                                  sem.at[buf]).start(priority=0)
            pltpu.make_async_copy(_row_tile(yb_hbm, dest_ref[n_tok + tok]), _row_tile(gbuf.at[buf, 1], r),
                                  sem.at[buf]).start(priority=1)
            return carry
        lax.fori_loop(0, t, body, 0, unroll=DMA_UNROLL)

    @pl.when(step == 0)
    def _():
        issue(step, slot)

    @pl.when(step + 1 < pl.num_programs(0))
    def _():
        issue(step + 1, 1 - slot)

    for kk in range(2):
        pltpu.make_async_copy(yb_hbm.at[pl.ds(0, t * ROW_SUB), :], gbuf.at[slot, kk], sem.at[slot]).wait()
    slab = slab_ref[...]
    y = (slab[:, 2:3] * _load_row_tiles(gbuf.at[slot, 0])
         + slab[:, 3:4] * _load_row_tiles(gbuf.at[slot, 1]))
    gate2 = mod_ref[0, 5:6, :]
    x2 = x1_ref[...] + gate2 * y
    out_ref[...] = _rms(x2, gf_ref[...])


def _combine_call(dest_flat, yb, slab, x1_tok, mod3, gf, seq):
    n_tok, d = x1_tok.shape
    t = min(COMBINE_T, seq)
    return pl.pallas_call(
        _combine_kernel,
        grid_spec=pltpu.PrefetchScalarGridSpec(
            num_scalar_prefetch=1, grid=(n_tok // t,),
            in_specs=[pl.BlockSpec(memory_space=pl.ANY),
                      pl.BlockSpec((t, LANES), lambda i, dr: (i, 0)),
                      pl.BlockSpec((t, d), lambda i, dr: (i, 0)),
                      pl.BlockSpec((1,) + mod3.shape[1:], lambda i, dr: ((i * t) // seq, 0, 0)),
                      pl.BlockSpec((1, d), lambda i, dr: (0, 0))],
            out_specs=pl.BlockSpec((t, d), lambda i, dr: (i, 0)),
            scratch_shapes=[pltpu.VMEM((2, 2, t * ROW_SUB, LANES), F32), pltpu.SemaphoreType.DMA((2,))]),
        out_shape=jax.ShapeDtypeStruct((n_tok, d), F32),
        compiler_params=pltpu.CompilerParams(dimension_semantics=("arbitrary",),
                                             vmem_limit_bytes=VMEM_LIMIT),
        name="moe_combine",
    )(dest_flat, yb, slab, x1_tok, mod3, gf.reshape(1, d))


def kernel(x, c, ada_w, ada_b, norm1_g, w_in, gmlp_ln_w, gmlp_ln_b, gmlp_ws, gmlp_bs, rwkv_mu, rwkv_w0,
           rwkv_w2, rwkv_a0, rwkv_a2, rwkv_g2, rwkv_k_k, rwkv_k_a, rwkv_r_k, rwkv_gn_w, rwkv_gn_b, w_out,
           norm2_g, router_group_w, router_group_b, router_expert_w, router_expert_b, moe_w_gate,
           moe_w_up, moe_w_down, final_norm_g):
    bsz, seq, d = x.shape
    n_tok = bsz * seq
    assert ada_w.shape[0] == 1, "the final rmsnorm is fused into the (single) layer's combine kernel"
    for l in range(1):
        p = dict(norm1_g=norm1_g[l], w_in=w_in[l], gmlp_ln_w=gmlp_ln_w[l], gmlp_ln_b=gmlp_ln_b[l],
                 gmlp_ws=gmlp_ws[l], gmlp_bs=gmlp_bs[l], rwkv_mu=rwkv_mu[l], rwkv_w0=rwkv_w0[l],
                 rwkv_w2=rwkv_w2[l], rwkv_a0=rwkv_a0[l], rwkv_a2=rwkv_a2[l], rwkv_g2=rwkv_g2[l],
                 rwkv_k_k=rwkv_k_k[l], rwkv_k_a=rwkv_k_a[l], rwkv_r_k=rwkv_r_k[l],
                 rwkv_gn_w=rwkv_gn_w[l], rwkv_gn_b=rwkv_gn_b[l], w_out=w_out[l], norm2_g=norm2_g[l],
                 router_group_w=router_group_w[l], router_group_b=router_group_b[l],
                 router_expert_w=router_expert_w[l], router_expert_b=router_expert_b[l])
        mod3 = _mod_call(c, ada_w[l], ada_b[l]).reshape(bsz, 6, d)
        pre = _pre_call(x, mod3, p)
        x1, h2, route, slab = _scan_call(x, mod3, pre, p)
        dest, meta = _route_call(route)
        dest_flat = dest.reshape(-1)
        n_slots = _n_blocks(n_tok) * MOE_BM
        meta_flat = meta.reshape(-1)
        xs = _dispatch_call(dest_flat, meta_flat, h2.reshape(n_tok * ROW_SUB, LANES), n_slots)
        yb = _expert_call(meta_flat, xs, moe_w_gate[l], moe_w_up[l], moe_w_down[l])
        x = _combine_call(dest_flat, yb, slab, x1.reshape(n_tok, d), mod3, final_norm_g, seq).reshape(bsz, seq, d)
    return x
```

```python
import functools

import jax
import jax.numpy as jnp
from jax import lax
from jax.experimental import pallas as pl
from jax.experimental.pallas import tpu as pltpu

F32 = jnp.float32
BF16 = jnp.bfloat16
I32 = jnp.int32

HEAD_DIM = 64
GMLP_WIDTH = 512
RWKV_WIDTH = 512
GMLP_CHUNK = 128
DECAY_LORA = 32
ICLR_LORA = 32
GATE_LORA = 96
LORA_PAD = 256
N_GROUPS = 4
EXPERTS_PER_GROUP = 8
N_EXPERTS = 32
RMS_EPS = 1e-6
LN_EPS = 1e-5
GN_EPS = 64e-5

LANES = 128
MXU_DIM = 256
VMEM_LIMIT = 56 * 1024 * 1024

PRE_TM = 512
PRE_SUB = 256
SCAN_TM = 512
SCAN_L = 64
HEADS_PER_GROUP = MXU_DIM // HEAD_DIM
ROUTE_T = 512
MOE_BM = 512
DISPATCH_T = 1024
COMBINE_T = 512
ROUTE_ROWS = 8
DMA_UNROLL = 8


def _dot(a, b):
    return jnp.dot(a, b, preferred_element_type=F32)


def _dot_nt(a, b):
    return lax.dot_general(a, b, (((1,), (1,)), ((), ())), preferred_element_type=F32)


def _rms(x, g):
    return x * lax.rsqrt(jnp.mean(x * x, axis=-1, keepdims=True) + RMS_EPS) * g


ROW_SUB = 8


def _store_row_tiles(ref, val):
    m = val.shape[0]
    for s in range(ROW_SUB):
        ref[pl.ds(s, m, stride=ROW_SUB), :] = val[:, s * LANES:(s + 1) * LANES]


def _load_row_tiles(ref):
    m = ref.shape[0] // ROW_SUB
    return jnp.concatenate([ref[pl.ds(s, m, stride=ROW_SUB), :] for s in range(ROW_SUB)], axis=1)


def _row_tile(ref, row):
    return ref.at[pl.ds(pl.multiple_of(row * ROW_SUB, ROW_SUB), ROW_SUB), :]


def _mod_kernel(c_ref, w_ref, b_ref, o_ref):
    c = c_ref[...]
    w = w_ref[...]
    c_hi = c.astype(BF16)
    c_lo = (c - c_hi.astype(F32)).astype(BF16)
    w_hi = w.astype(BF16)
    w_lo = (w - w_hi.astype(F32)).astype(BF16)
    o_ref[...] = _dot(c_hi, w_hi) + _dot(c_lo, w_hi) + _dot(c_hi, w_lo) + b_ref[...]


def _mod_call(c, ada_w, ada_b):
    bsz, d = c.shape
    n = ada_w.shape[1]
    tn = 1024
    return pl.pallas_call(
        _mod_kernel,
        grid=(n // tn,),
        in_specs=[pl.BlockSpec((bsz, d), lambda j: (0, 0)),
                  pl.BlockSpec((d, tn), lambda j: (0, j)),
                  pl.BlockSpec((1, tn), lambda j: (0, j))],
        out_specs=pl.BlockSpec((bsz, tn), lambda j: (0, j)),
        out_shape=jax.ShapeDtypeStruct((bsz, n), F32),
        compiler_params=pltpu.CompilerParams(dimension_semantics=("arbitrary",),
                                             vmem_limit_bytes=VMEM_LIMIT),
        name="adaln_mod",
    )(c, ada_w, ada_b.reshape(1, n))


def _token_shift(z, carry_ref):
    tm = z.shape[0]
    prev = pltpu.roll(z, 1, axis=0)
    row = lax.broadcasted_iota(I32, z.shape, 0)
    prev = jnp.where(row == 0, carry_ref[0:1, :], prev)
    carry_ref[0:1, :] = z[tm - 1:tm, :]
    return prev


def _pre_kernel(x_ref, mod_ref, g1_ref, wuv_ref, wrkv_ref, wl_ref, lnw_ref, lnb_ref, wsp_ref,
                bsx_ref, mu_rkv_ref, mu_l_ref, lora_w_ref, w0_ref, a0_ref, kk_ref, ka_ref, ones_ref,
                ya_ref, r_ref, k2_ref, v_ref, kkn_ref, al_ref, g_ref, lw_ref,
                carry_rkv, carry_l):
    tm = x_ref.shape[1]

    @pl.when(pl.program_id(1) == 0)
    def _():
        carry_rkv[...] = jnp.zeros_like(carry_rkv)
        carry_l[...] = jnp.zeros_like(carry_l)

    shift1 = mod_ref[0, 0:1, :]
    scale1 = mod_ref[0, 1:2, :]
    sub = min(PRE_SUB, tm)
    zs = []
    for i in range(tm // sub):
        x = x_ref[0, i * sub:(i + 1) * sub, :]
        hb = (_rms(x, g1_ref[...]) * (1.0 + scale1) + shift1).astype(BF16)
        zs.append((_dot(hb, wuv_ref[...]),
                   _dot(hb, wrkv_ref[...]),
                   _dot(hb, wl_ref[...])))

    lane = lax.broadcasted_iota(I32, (GMLP_CHUNK, LANES), 1)
    left = lane < HEAD_DIM
    trow = lax.broadcasted_iota(I32, (GMLP_CHUNK, 2 * GMLP_CHUNK), 0)
    tcol = lax.broadcasted_iota(I32, (GMLP_CHUNK, 2 * GMLP_CHUNK), 1) % GMLP_CHUNK
    causal = tcol <= trow
    zero = jnp.zeros((), BF16)
    wpairs = [jnp.where(causal, wsp_ref[p], 0.0).astype(BF16) for p in range(GMLP_WIDTH // LANES)]

    for i, (zuv, zr, zl) in enumerate(zs):
        base = i * sub
        u = jax.nn.gelu(zuv[:, :GMLP_WIDTH])
        v = jax.nn.gelu(zuv[:, GMLP_WIDTH:])
        m = jnp.mean(v, axis=-1, keepdims=True)
        var = jnp.mean(jnp.square(v - m), axis=-1, keepdims=True)
        vn = ((v - m) * lax.rsqrt(var + LN_EPS) * lnw_ref[...] + lnb_ref[...]).astype(BF16)
        for p, wpair in enumerate(wpairs):
            for c in range(sub // GMLP_CHUNK):
                rows = slice(c * GMLP_CHUNK, (c + 1) * GMLP_CHUNK)
                cols = slice(p * LANES, (p + 1) * LANES)
                vc = vn[rows, cols]
                rhs = jnp.concatenate([jnp.where(left, vc, zero), jnp.where(left, zero, vc)], axis=0)
                mixed = _dot(wpair, rhs) + bsx_ref[:, cols]
                ya_ref[0, base + c * GMLP_CHUNK:base + (c + 1) * GMLP_CHUNK, cols] = (
                    u[rows, cols] * mixed).astype(ya_ref.dtype)

        out = slice(base, base + sub)
        zr = zr + (_token_shift(zr, carry_rkv) - zr) * mu_rkv_ref[...]
        zl = zl + (_token_shift(zl, carry_l) - zl) * mu_l_ref[...]
        r = zr[:, :RWKV_WIDTH]
        k = zr[:, RWKV_WIDTH:2 * RWKV_WIDTH]
        vv = zr[:, 2 * RWKV_WIDTH:]
        llane = lax.broadcasted_iota(I32, zl.shape, 1)
        f = jnp.where(llane < DECAY_LORA, jnp.tanh(zl),
                      jnp.where(llane < DECAY_LORA + ICLR_LORA, zl, jax.nn.sigmoid(zl)))
        lo = _dot(f.astype(BF16), lora_w_ref[...])
        wpre = w0_ref[...] + lo[:, :RWKV_WIDTH]
        w_log = -(jnp.maximum(-wpre, 0.0) + jnp.log(1.0 + jnp.exp(-jnp.abs(wpre)))) - 0.5
        lw_ref[0, out, :] = -jnp.exp(w_log)
        alpha = jax.nn.sigmoid(a0_ref[...] + lo[:, RWKV_WIDTH:2 * RWKV_WIDTH])
        g_ref[0, out, :] = lo[:, 2 * RWKV_WIDTH:].astype(g_ref.dtype)
        kkr = k * kk_ref[...]
        ss = _dot((kkr * kkr).astype(BF16), ones_ref[...])
        kkn_ref[0, out, :] = (kkr * lax.rsqrt(jnp.maximum(ss, 1e-24))).astype(kkn_ref.dtype)
        k2_ref[0, out, :] = (k * (1.0 + (alpha - 1.0) * ka_ref[...])).astype(k2_ref.dtype)
        r_ref[0, out, :] = r.astype(r_ref.dtype)
        v_ref[0, out, :] = vv.astype(v_ref.dtype)
        al_ref[0, out, :] = alpha.astype(al_ref.dtype)


def _head_ones(width):
    hid = jnp.arange(width) // HEAD_DIM
    return (hid[:, None] == hid[None, :]).astype(BF16)


def _pre_call(x, mod3, p):
    bsz, seq, d = x.shape
    tm = min(PRE_TM, seq)
    w_in = p["w_in"]
    wuv = w_in[:, :2 * GMLP_WIDTH].astype(BF16)
    wrkv = w_in[:, 2 * GMLP_WIDTH:2 * GMLP_WIDTH + 3 * RWKV_WIDTH].astype(BF16)
    n_lora = DECAY_LORA + ICLR_LORA + GATE_LORA
    wl = jnp.pad(w_in[:, 2 * GMLP_WIDTH + 3 * RWKV_WIDTH:], ((0, 0), (0, LORA_PAD - n_lora))).astype(BF16)
    mu = p["rwkv_mu"]
    mu_rkv = mu[:3 * RWKV_WIDTH].reshape(1, -1)
    mu_l = jnp.pad(mu[3 * RWKV_WIDTH:], (0, LORA_PAD - n_lora)).reshape(1, -1)
    zpad = lambda a, lo_, hi_: jnp.pad(a, ((lo_, LORA_PAD - hi_), (0, 0)))
    lora_w = jnp.concatenate([
        zpad(p["rwkv_w2"], 0, DECAY_LORA),
        zpad(p["rwkv_a2"], DECAY_LORA, DECAY_LORA + ICLR_LORA),
        zpad(p["rwkv_g2"], DECAY_LORA + ICLR_LORA, n_lora)], axis=1).astype(BF16)
    ws = p["gmlp_ws"]
    wsp = jnp.concatenate([ws[0::2], ws[1::2]], axis=2)
    bsx = jnp.repeat(p["gmlp_bs"].T, HEAD_DIM, axis=1)
    row = lambda a: a.reshape(1, -1)
    full = lambda a: pl.BlockSpec(a.shape, lambda b, s: (0,) * a.ndim)
    args = [mod3, row(p["norm1_g"]), wuv, wrkv, wl, row(p["gmlp_ln_w"]), row(p["gmlp_ln_b"]), wsp, bsx,
            mu_rkv, mu_l, lora_w, row(p["rwkv_w0"]), row(p["rwkv_a0"]), row(p["rwkv_k_k"]),
            row(p["rwkv_k_a"]), _head_ones(RWKV_WIDTH)]
    in_specs = [pl.BlockSpec((1, tm, d), lambda b, s: (b, s, 0)),
                pl.BlockSpec((1,) + mod3.shape[1:], lambda b, s: (b, 0, 0))]
    in_specs += [full(a) for a in args[1:]]
    tok = lambda dt: jax.ShapeDtypeStruct((bsz, seq, RWKV_WIDTH), dt)
    out_spec = pl.BlockSpec((1, tm, RWKV_WIDTH), lambda b, s: (b, s, 0))
    outs = [tok(BF16)] * 7 + [tok(F32)]
    return pl.pallas_call(
        _pre_kernel,
        grid=(bsz, seq // tm),
        in_specs=in_specs,
        out_specs=[out_spec] * 8,
        out_shape=outs,
        scratch_shapes=[pltpu.VMEM((8, 3 * RWKV_WIDTH), F32), pltpu.VMEM((8, LORA_PAD), F32)],
        compiler_params=pltpu.CompilerParams(dimension_semantics=("arbitrary", "arbitrary"),
                                             vmem_limit_bytes=VMEM_LIMIT),
        name="pre_mix",
    )(x, *args)


def _blockdiag(xb, mask):
    t = jnp.concatenate([xb] * HEADS_PER_GROUP, axis=0)
    return jnp.where(mask, t, jnp.zeros((), xb.dtype))


def _head_transpose(x):
    xt = x.T
    return jnp.concatenate([xt[hh * HEAD_DIM:(hh + 1) * HEAD_DIM, :] for hh in range(HEADS_PER_GROUP)], axis=1)


def _scan_kernel(x_ref, mod_ref, ya_ref, r_ref, k2_ref, v_ref, kk_ref, al_ref, g_ref, lw_ref,
                 rk_ref, gnw_ref, gnb_ref, ones_ref, wout_ref, g2n_ref, wr_ref, br_ref,
                 x1_ref, h2_ref, route_ref, slab_ref, cnt_ref, state_ref, y_scr):
    tm = x_ref.shape[1]
    L = SCAN_L
    n_chunks = tm // L
    n_groups = RWKV_WIDTH // MXU_DIM

    @pl.when(pl.program_id(1) == 0)
    def _():
        state_ref[...] = jnp.zeros_like(state_ref)

    r = r_ref[0].astype(F32)
    k2 = k2_ref[0].astype(F32)
    v = v_ref[0].astype(F32)
    kk = kk_ref[0].astype(F32)
    b = kk * al_ref[0].astype(F32)
    lw = lw_ref[0]
    lw_hi = lw.astype(BF16)
    lw_lo = (lw - lw_hi.astype(F32)).astype(BF16)

    trow = lax.broadcasted_iota(I32, (L, L), 0)
    tcol = lax.broadcasted_iota(I32, (L, L), 1)
    tri = (tcol <= trow).astype(BF16)
    row = lax.broadcasted_iota(I32, (L, MXU_DIM), 0)
    lane = lax.broadcasted_iota(I32, (L, MXU_DIM), 1)
    lane_j = lane % HEAD_DIM
    strict = lane_j < row
    incl = lane_j <= row
    eye = lane_j == row
    brow = lax.broadcasted_iota(I32, (MXU_DIM, MXU_DIM), 0) // HEAD_DIM
    bcol = lax.broadcasted_iota(I32, (MXU_DIM, MXU_DIM), 1) // HEAD_DIM
    bmask = brow == bcol

    def bd(xf):
        return _blockdiag(xf.astype(BF16), bmask)

    probs = [(c, g) for c in range(n_chunks) for g in range(n_groups)]
    each = lambda fn, *lists: [fn(*a) for a in zip(*lists)]
    lgs = [_dot(tri, lw_hi[c * L:(c + 1) * L]) + _dot(tri, lw_lo[c * L:(c + 1) * L])
           for c in range(n_chunks)]

    def sl(arr, c, g):
        return arr[c * L:(c + 1) * L, g * MXU_DIM:(g + 1) * MXU_DIM]

    lgc = [lgs[c][:, g * MXU_DIM:(g + 1) * MXU_DIM] for c, g in probs]
    glast = [x[L - 1:L, :] for x in lgc]
    eg = [jnp.exp(x) for x in lgc]
    egx = [jnp.exp(x - sl(lw, c, g)) for x, (c, g) in zip(lgc, probs)]
    einv = [jnp.exp(-x) for x in lgc]
    ehat = each(lambda gl, x: jnp.exp(gl - x), glast, lgc)
    at = [-sl(kk, c, g) * e for e, (c, g) in zip(egx, probs)]
    rt = [sl(r, c, g) * e for e, (c, g) in zip(eg, probs)]
    bt = [sl(b, c, g) * e for e, (c, g) in zip(einv, probs)]
    kt = [sl(k2, c, g) * e for e, (c, g) in zip(einv, probs)]
    bh = [sl(b, c, g) * e for e, (c, g) in zip(ehat, probs)]
    kh = [sl(k2, c, g) * e for e, (c, g) in zip(ehat, probs)]
    vc = [sl(v, c, g) for c, g in probs]
    lhs = each(lambda a_, r_: jnp.concatenate([a_, r_], axis=0).astype(BF16), at, rt)
    pb = each(lambda l_, x: _dot_nt(l_, bd(x)), lhs, bt)
    pk = each(lambda l_, x: _dot_nt(l_, bd(x)), lhs, kt)
    a_ab = [jnp.where(strict, x[:L], 0.0) for x in pb]
    a_rb = [jnp.where(incl, x[L:], 0.0) for x in pb]
    a_ak = [jnp.where(strict, x[:L], 0.0) for x in pk]
    a_rk = [jnp.where(incl, x[L:], 0.0) for x in pk]
    ssum = [jnp.where(eye, 1.0, 0.0) + x for x in a_ab]
    xb = [x.astype(BF16) for x in a_ab]
    xp = [_dot(x, _blockdiag(x, bmask)) for x in xb]
    m = 2
    while m < L // 2:
        xb = [x.astype(BF16) for x in xp]
        res = each(lambda x, s_: _dot(jnp.concatenate([x, s_.astype(BF16)], axis=0), _blockdiag(x, bmask)),
                   xb, ssum)
        xp = [x[:L] for x in res]
        ssum = each(lambda s_, x: s_ + x[L:], ssum, res)
        m *= 2
    tmat = each(lambda s_, x: s_ + _dot(s_.astype(BF16), bd(x)), ssum, xp)
    bht = [_head_transpose(x).astype(BF16) for x in bh]
    kht = [_head_transpose(x) for x in kh]
    res = each(lambda a1, a2, k_, x: _dot(jnp.concatenate([a1, a2, k_], axis=0).astype(BF16), bd(x)),
               a_ak, a_rk, kht, vc)
    akv = [x[:L] for x in res]
    arkv = [x[L:2 * L] for x in res]
    khv = [x[2 * L:] for x in res]
    tb = [x.astype(BF16) for x in tmat]
    wm = each(lambda t_, x: _dot(t_, bd(x)), tb, at)
    u0 = each(lambda t_, x: _dot(t_, bd(x)), tb, akv)
    lhs_w = each(lambda a_, b_: jnp.concatenate([a_.astype(BF16), b_], axis=0), a_rb, bht)
    res_w = each(lambda l_, x: _dot(l_, bd(x)), lhs_w, wm)
    res_u = each(lambda l_, x: _dot(l_, bd(x)), lhs_w, u0)
    qp = each(lambda r_, x: r_ + x[:L], rt, res_w)
    mt = each(lambda x, gl: x[L:] + jnp.where(eye, jnp.exp(gl), 0.0), res_w, glast)
    y0 = each(lambda x, z: x[:L] + z, res_u, arkv)
    nt = each(lambda x, z: x[L:] + z, res_u, khv)
    lhs2 = each(lambda q_, m_: jnp.concatenate([q_, m_], axis=0).astype(BF16), qp, mt)

    st = [state_ref[g] for g in range(n_groups)]
    for c in range(n_chunks):
        for g in range(n_groups):
            i = c * n_groups + g
            res_s = _dot(lhs2[i], bd(st[g]))
            y_scr[c * L:(c + 1) * L, g * MXU_DIM:(g + 1) * MXU_DIM] = res_s[:L] + y0[i]
            st[g] = res_s[L:] + nt[i]
    for g in range(n_groups):
        state_ref[g] = st[g]

    y = y_scr[...]
    ones = ones_ref[...]
    inv_n = 1.0 / HEAD_DIM
    mean = _dot(y.astype(BF16), ones) * inv_n
    dlt = y - mean
    var = _dot((dlt * dlt).astype(BF16), ones) * inv_n
    yn = dlt * lax.rsqrt(var + GN_EPS) * gnw_ref[...] + gnb_ref[...]
    bonus = _dot((r * k2 * rk_ref[...]).astype(BF16), ones) * v
    yb = (yn + bonus) * g_ref[0].astype(F32)
    cat = jnp.concatenate([ya_ref[0], yb.astype(BF16)], axis=1)
    y2 = _dot(cat, wout_ref[...])
    gate1 = mod_ref[0, 2:3, :]
    shift2 = mod_ref[0, 3:4, :]
    scale2 = mod_ref[0, 4:5, :]
    x1 = x_ref[0] + gate1 * y2
    x1_ref[0] = x1
    h2 = _rms(x1, g2n_ref[...]) * (1.0 + scale2) + shift2
    _store_row_tiles(h2_ref.at[0], h2)

    logits = _dot(h2.astype(BF16), wr_ref[...]) + br_ref[...]
    slab = _route_slab(logits)
    slab_ref[...] = slab
    slab_t = slab.T[:ROUTE_ROWS, :]
    route_ref[...] = slab_t

    @pl.when(jnp.logical_and(pl.program_id(0) == 0, pl.program_id(1) == 0))
    def _():
        cnt_ref[...] = jnp.zeros_like(cnt_ref)

    erow = lax.broadcasted_iota(I32, (N_EXPERTS, tm), 0)
    hits = ((erow == slab_t[0:1, :].astype(I32)).astype(F32)
            + (erow == slab_t[1:2, :].astype(I32)).astype(F32))
    cnt_ref[...] += jnp.sum(hits, axis=1, keepdims=True)


def _scan_call(x, mod3, pre_outs, p):
    bsz, seq, d = x.shape
    tm = min(SCAN_TM, seq)
    n_tok = bsz * seq
    steps = seq // tm
    n_route = N_GROUPS + N_EXPERTS
    wr = jnp.pad(jnp.concatenate([p["router_group_w"], p["router_expert_w"]], axis=1),
                 ((0, 0), (0, LANES - n_route))).astype(BF16)
    br = jnp.pad(jnp.concatenate([p["router_group_b"], p["router_expert_b"]]), (0, LANES - n_route)).reshape(1, -1)
    row = lambda a: a.reshape(1, -1)
    full = lambda a: pl.BlockSpec(a.shape, lambda b, s: (0,) * a.ndim)
    params = [row(p["rwkv_r_k"]), row(p["rwkv_gn_w"]), row(p["rwkv_gn_b"]), _head_ones(RWKV_WIDTH),
              p["w_out"].astype(BF16), row(p["norm2_g"]), wr, br]
    tok_spec = pl.BlockSpec((1, tm, RWKV_WIDTH), lambda b, s: (b, s, 0))
    x_spec = pl.BlockSpec((1, tm, d), lambda b, s: (b, s, 0))
    in_specs = [x_spec, pl.BlockSpec((1,) + mod3.shape[1:], lambda b, s: (b, 0, 0))]
    in_specs += [tok_spec] * 8 + [full(a) for a in params]
    return pl.pallas_call(
        _scan_kernel,
        grid=(bsz, steps),
        in_specs=in_specs,
        out_specs=[x_spec, pl.BlockSpec((1, tm * ROW_SUB, LANES), lambda b, s: (b, s, 0)),
                   pl.BlockSpec((ROUTE_ROWS, tm), lambda b, s: (0, b * steps + s)),
                   pl.BlockSpec((tm, LANES), lambda b, s: (b * steps + s, 0)),
                   pl.BlockSpec((N_EXPERTS, LANES), lambda b, s: (0, 0))],
        out_shape=[jax.ShapeDtypeStruct((bsz, seq, d), F32),
                   jax.ShapeDtypeStruct((bsz, seq * ROW_SUB, LANES), F32),
                   jax.ShapeDtypeStruct((ROUTE_ROWS, n_tok), F32), jax.ShapeDtypeStruct((n_tok, LANES), F32),
                   jax.ShapeDtypeStruct((N_EXPERTS, LANES), F32)],
        scratch_shapes=[pltpu.VMEM((RWKV_WIDTH // MXU_DIM, HEAD_DIM, MXU_DIM), F32),
                        pltpu.VMEM((tm, RWKV_WIDTH), F32)],
        compiler_params=pltpu.CompilerParams(dimension_semantics=("arbitrary", "arbitrary"),
                                             vmem_limit_bytes=VMEM_LIMIT),
        name="scan_post",
    )(x, mod3, *pre_outs, *params)


def _route_slab(logits):
    ln = lax.broadcasted_iota(I32, logits.shape, 1)
    neg = jnp.float32(-jnp.inf)
    is_g = ln < N_GROUPS
    gl = jnp.where(is_g, logits, neg)
    gmax = jnp.max(gl, axis=-1, keepdims=True)
    gidx = jnp.min(jnp.where(gl == gmax, ln, LANES), axis=-1, keepdims=True)
    g_p = 1.0 / jnp.sum(jnp.where(is_g, jnp.exp(logits - gmax), 0.0), axis=-1, keepdims=True)
    egrp = (ln - N_GROUPS) // EXPERTS_PER_GROUP
    sel = jnp.where(ln >= N_GROUPS, egrp, -1) == gidx
    el = jnp.where(sel, logits, neg)
    l1 = jnp.max(el, axis=-1, keepdims=True)
    i1 = jnp.min(jnp.where(el == l1, ln, LANES), axis=-1, keepdims=True)
    el2 = jnp.where(ln == i1, neg, el)
    l2 = jnp.max(el2, axis=-1, keepdims=True)
    i2 = jnp.min(jnp.where(el2 == l2, ln, LANES), axis=-1, keepdims=True)
    t = jnp.exp(l2 - l1)
    p1 = 1.0 / (1.0 + t)
    w1 = g_p * p1
    w2 = g_p * (t * p1)
    e1 = (i1 - N_GROUPS).astype(F32)
    e2 = (i2 - N_GROUPS).astype(F32)
    return jnp.where(ln == 0, e1, jnp.where(ln == 1, e2, jnp.where(ln == 2, w1, jnp.where(ln == 3, w2, 0.0))))


def _onehots(route_ref):
    t = route_ref.shape[1]
    erow = lax.broadcasted_iota(I32, (N_EXPERTS, t), 0)
    oh1 = erow == route_ref[0:1, :].astype(I32)
    oh2 = erow == route_ref[1:2, :].astype(I32)
    return oh1, oh2


def _dest_kernel(route_ref, cnt_ref, dest_ref, meta_ref, base_ref):
    t = route_ref.shape[1]
    nbp = meta_ref.shape[1]

    @pl.when(pl.program_id(0) == 0)
    def _():
        cnt = cnt_ref[...]
        padded = jnp.floor((cnt + (MOE_BM - 1)) * (1.0 / MOE_BM)) * MOE_BM
        er = lax.broadcasted_iota(I32, (N_EXPERTS, N_EXPERTS), 0)
        ec = lax.broadcasted_iota(I32, (N_EXPERTS, N_EXPERTS), 1)
        lower = (ec < er).astype(BF16)
        p_hi = padded.astype(BF16)
        p_lo = (padded - p_hi.astype(F32)).astype(BF16)
        pstart = _dot(lower, p_hi) + _dot(lower, p_lo)
        base_ref[...] = pstart
        pend = (pstart + padded)[:, 0:1]
        blk0 = (lax.broadcasted_iota(I32, (N_EXPERTS, nbp), 1) * MOE_BM).astype(F32)
        be = jnp.sum((pend <= blk0).astype(I32), axis=0, keepdims=True)
        meta_ref[0:1, :] = jnp.minimum(be, N_EXPERTS - 1)
        used = (pend[N_EXPERTS - 1:N_EXPERTS, :] * (1.0 / MOE_BM)).astype(I32)
        meta_ref[1:2, :] = jnp.broadcast_to(used, (1, nbp))
        on_diag = (lax.broadcasted_iota(I32, (N_EXPERTS, nbp), 0)
                   == lax.broadcasted_iota(I32, (N_EXPERTS, nbp), 1))
        pad_lo = (pstart + cnt)[:, 0:1]
        pad_len = (padded - cnt)[:, 0:1]
        meta_ref[2:3, :] = jnp.sum(jnp.where(on_diag, pad_lo, 0.0), axis=0, keepdims=True).astype(I32)
        meta_ref[3:4, :] = jnp.sum(jnp.where(on_diag, pad_len, 0.0), axis=0, keepdims=True).astype(I32)
        e_id = lax.broadcasted_iota(I32, (N_EXPERTS, nbp), 0)
        later = jnp.logical_and(e_id > jnp.minimum(be, N_EXPERTS - 1), padded[:, 0:1] > 0.0)
        nxt = jnp.min(jnp.where(later, e_id, N_EXPERTS), axis=0, keepdims=True)
        meta_ref[4:5, :] = jnp.where(nxt == N_EXPERTS, -1, nxt)
        meta_ref[5:, :] = jnp.zeros((meta_ref.shape[0] - 5, nbp), I32)

    oh1, oh2 = _onehots(route_ref)
    cnt = oh1.astype(F32) + oh2.astype(F32)
    sr = lax.broadcasted_iota(I32, (t, t), 0)
    sc = lax.broadcasted_iota(I32, (t, t), 1)
    upper = (sr < sc).astype(BF16)
    pos = _dot(cnt.astype(BF16), upper) + base_ref[:, 0:1]
    dest_ref[0:1, :] = jnp.sum(jnp.where(oh1, pos, 0.0), axis=0, keepdims=True).astype(I32)
    dest_ref[1:2, :] = jnp.sum(jnp.where(oh2, pos, 0.0), axis=0, keepdims=True).astype(I32)
    base_ref[...] += jnp.sum(cnt, axis=1, keepdims=True)


def _n_blocks(n_tok):
    return -(-2 * n_tok // MOE_BM) + N_EXPERTS


def _route_call(route, counts):
    n_tok = route.shape[1]
    t = min(ROUTE_T, n_tok)
    nbp = -(-_n_blocks(n_tok) // LANES) * LANES
    cparams = pltpu.CompilerParams(dimension_semantics=("arbitrary",), vmem_limit_bytes=VMEM_LIMIT)
    route_spec = pl.BlockSpec((ROUTE_ROWS, t), lambda j: (0, j))
    cnt_spec = pl.BlockSpec((N_EXPERTS, LANES), lambda j: (0, 0))
    dest, meta = pl.pallas_call(
        _dest_kernel, grid=(n_tok // t,), in_specs=[route_spec, cnt_spec],
        out_specs=[pl.BlockSpec((2, t), lambda j: (0, j)), pl.BlockSpec((8, nbp), lambda j: (0, 0))],
        out_shape=[jax.ShapeDtypeStruct((2, n_tok), I32), jax.ShapeDtypeStruct((8, nbp), I32)],
        scratch_shapes=[pltpu.VMEM((N_EXPERTS, LANES), F32)], compiler_params=cparams,
        name="route_dest")(route, counts)
    return dest, meta


def _zero_fill_slots(meta_ref, xs_hbm, ztile, zsem, wait):
    nbp = meta_ref.shape[0] // 8
    n_blocks = xs_hbm.shape[0] // (MOE_BM * ROW_SUB)

    def go(rows, first_slot):
        cp = pltpu.make_async_copy(
            ztile.at[pl.ds(0, rows * ROW_SUB), :],
            xs_hbm.at[pl.ds(pl.multiple_of(first_slot * ROW_SUB, ROW_SUB), rows * ROW_SUB), :], zsem)
        if wait:
            cp.wait()
        else:
            cp.start()

    def expert_pad(e, carry):
        slot = meta_ref[2 * nbp + e]
        plen = meta_ref[3 * nbp + e]
        bit = MOE_BM // 2
        while bit >= 1:
            has = (plen & bit) != 0
            pl.when(has)(functools.partial(go, bit, slot))
            slot = slot + jnp.where(has, bit, 0)
            bit //= 2
        return carry

    def tail_block(j, carry):
        go(MOE_BM, j * MOE_BM)
        return carry

    lax.fori_loop(0, N_EXPERTS, expert_pad, 0)
    lax.fori_loop(meta_ref[nbp], n_blocks, tail_block, 0)


def _dispatch_kernel(dest_ref, meta_ref, h_ref, xs_hbm, sem, zsem, ztile):
    n_tok = dest_ref.shape[0] // 2
    t = h_ref.shape[0] // ROW_SUB
    base = pl.program_id(0) * t
    first = pl.program_id(0) == 0

    @pl.when(first)
    def _():
        ztile[...] = jnp.zeros_like(ztile)
        _zero_fill_slots(meta_ref, xs_hbm, ztile, zsem, wait=False)

    def start(i, carry):
        tok = base + i
        src = _row_tile(h_ref, i)
        pltpu.make_async_copy(src, _row_tile(xs_hbm, dest_ref[tok]), sem).start(priority=0)
        pltpu.make_async_copy(src, _row_tile(xs_hbm, dest_ref[n_tok + tok]), sem).start(priority=1)
        return carry

    lax.fori_loop(0, t, start, 0, unroll=DMA_UNROLL)
    for _ in range(2):
        pltpu.make_async_copy(h_ref, xs_hbm.at[pl.ds(0, t * ROW_SUB), :], sem).wait()

    @pl.when(first)
    def _():
        _zero_fill_slots(meta_ref, xs_hbm, ztile, zsem, wait=True)


def _dispatch_call(dest_flat, meta_flat, h_tok, n_slots):
    n_tok = h_tok.shape[0] // ROW_SUB
    t = min(DISPATCH_T, n_tok)
    return pl.pallas_call(
        _dispatch_kernel,
        grid_spec=pltpu.PrefetchScalarGridSpec(
            num_scalar_prefetch=2, grid=(n_tok // t,),
            in_specs=[pl.BlockSpec((t * ROW_SUB, LANES), lambda i, dr, mr: (i, 0))],
            out_specs=pl.BlockSpec(memory_space=pl.ANY),
            scratch_shapes=[pltpu.SemaphoreType.DMA, pltpu.SemaphoreType.DMA,
                            pltpu.VMEM((MOE_BM * ROW_SUB, LANES), h_tok.dtype)]),
        out_shape=jax.ShapeDtypeStruct((n_slots * ROW_SUB, LANES), h_tok.dtype),
        compiler_params=pltpu.CompilerParams(dimension_semantics=("arbitrary",),
                                             vmem_limit_bytes=VMEM_LIMIT),
        name="moe_dispatch",
    )(dest_flat, meta_flat, h_tok)


def _expert_kernel(meta_ref, xs_ref, wg_hbm, wu_hbm, wd_hbm, yb_ref, wgf, wuf, wdf, wgb, wub, wdb,
                   wsem, slot_ref):
    j = pl.program_id(0)
    nbp = meta_ref.shape[0] // 8
    used = j < meta_ref[nbp]
    new_expert = jnp.logical_or(j == 0, meta_ref[j] != meta_ref[jnp.maximum(j - 1, 0)])

    def fetch(e, slot):
        return [pltpu.make_async_copy(src.at[e], dst.at[slot], wsem.at[slot])
                for src, dst in ((wg_hbm, wgf), (wu_hbm, wuf), (wd_hbm, wdf))]

    @pl.when(j == 0)
    def _():
        slot_ref[0] = 0
        for cp in fetch(meta_ref[0], 0):
            cp.start()

    @pl.when(jnp.logical_and(used, new_expert))
    def _():
        slot = slot_ref[0]
        for cp in fetch(meta_ref[j], slot):
            cp.wait()
        nxt = meta_ref[4 * nbp + j]

        @pl.when(nxt >= 0)
        def _():
            for cp in fetch(nxt, 1 - slot):
                cp.start()

        wgb[...] = wgf[slot].astype(BF16)
        wub[...] = wuf[slot].astype(BF16)
        wdb[...] = wdf[slot].astype(BF16)
        slot_ref[0] = 1 - slot

    @pl.when(used)
    def _():
        x = _load_row_tiles(xs_ref).astype(BF16)
        gate = _dot(x, wgb[...])
        up = _dot(x, wub[...])
        hid = (jax.nn.silu(gate) * up).astype(BF16)
        _store_row_tiles(yb_ref, _dot(hid, wdb[...]))

    @pl.when(jnp.logical_not(used))
    def _():
        yb_ref[...] = jnp.zeros_like(yb_ref)


def _expert_call(meta_flat, xs, wg, wu, wd):
    n_slots = xs.shape[0] // ROW_SUB
    d, hid = wg.shape[1:]
    blk = pl.BlockSpec((MOE_BM * ROW_SUB, LANES), lambda j, m: (j, 0))
    return pl.pallas_call(
        _expert_kernel,
        grid_spec=pltpu.PrefetchScalarGridSpec(
            num_scalar_prefetch=1, grid=(n_slots // MOE_BM,),
            in_specs=[blk] + [pl.BlockSpec(memory_space=pl.ANY)] * 3,
            out_specs=blk,
            scratch_shapes=[pltpu.VMEM((2, d, hid), F32), pltpu.VMEM((2, d, hid), F32),
                            pltpu.VMEM((2, hid, d), F32),
                            pltpu.VMEM((d, hid), BF16), pltpu.VMEM((d, hid), BF16),
                            pltpu.VMEM((hid, d), BF16),
                            pltpu.SemaphoreType.DMA((2,)), pltpu.SMEM((1,), I32)]),
        out_shape=jax.ShapeDtypeStruct((n_slots * ROW_SUB, LANES), F32),
        compiler_params=pltpu.CompilerParams(dimension_semantics=("arbitrary",),
                                             vmem_limit_bytes=VMEM_LIMIT),
        name="moe_experts",
    )(meta_flat, xs, wg, wu, wd)


def _combine_kernel(dest_ref, yb_hbm, slab_ref, x1_ref, mod_ref, gf_ref, out_ref, gbuf, sem):
    t = x1_ref.shape[0]
    n_tok = dest_ref.shape[0] // 2
    step = pl.program_id(0)
    slot = step % 2

    def issue(tile, buf):
        def body(r, carry):
            tok = tile * t + r
            pltpu.make_async_copy(_row_tile(yb_hbm, dest_ref[tok]), _row_tile(gbuf.at[buf, 0], r),
                                  sem.at[buf]).start(priority=0)
            pltpu.make_async_copy(_row_tile(yb_hbm, dest_ref[n_tok + tok]), _row_tile(gbuf.at[buf, 1], r),
                                  sem.at[buf]).start(priority=1)
            return carry
        lax.fori_loop(0, t, body, 0, unroll=DMA_UNROLL)

    @pl.when(step == 0)
    def _():
        issue(step, slot)

    @pl.when(step + 1 < pl.num_programs(0))
    def _():
        issue(step + 1, 1 - slot)

    for kk in range(2):
        pltpu.make_async_copy(yb_hbm.at[pl.ds(0, t * ROW_SUB), :], gbuf.at[slot, kk], sem.at[slot]).wait()
    slab = slab_ref[...]
    y = (slab[:, 2:3] * _load_row_tiles(gbuf.at[slot, 0])
         + slab[:, 3:4] * _load_row_tiles(gbuf.at[slot, 1]))
    gate2 = mod_ref[0, 5:6, :]
    x2 = x1_ref[...] + gate2 * y
    out_ref[...] = _rms(x2, gf_ref[...])


def _combine_call(dest_flat, yb, slab, x1_tok, mod3, gf, seq):
    n_tok, d = x1_tok.shape
    t = min(COMBINE_T, seq)
    return pl.pallas_call(
        _combine_kernel,
        grid_spec=pltpu.PrefetchScalarGridSpec(
            num_scalar_prefetch=1, grid=(n_tok // t,),
            in_specs=[pl.BlockSpec(memory_space=pl.ANY),
                      pl.BlockSpec((t, LANES), lambda i, dr: (i, 0)),
                      pl.BlockSpec((t, d), lambda i, dr: (i, 0)),
                      pl.BlockSpec((1,) + mod3.shape[1:], lambda i, dr: ((i * t) // seq, 0, 0)),
                      pl.BlockSpec((1, d), lambda i, dr: (0, 0))],
            out_specs=pl.BlockSpec((t, d), lambda i, dr: (i, 0)),
            scratch_shapes=[pltpu.VMEM((2, 2, t * ROW_SUB, LANES), F32), pltpu.SemaphoreType.DMA((2,))]),
        out_shape=jax.ShapeDtypeStruct((n_tok, d), F32),
        compiler_params=pltpu.CompilerParams(dimension_semantics=("arbitrary",),
                                             vmem_limit_bytes=VMEM_LIMIT),
        name="moe_combine",
    )(dest_flat, yb, slab, x1_tok, mod3, gf.reshape(1, d))


def kernel(x, c, ada_w, ada_b, norm1_g, w_in, gmlp_ln_w, gmlp_ln_b, gmlp_ws, gmlp_bs, rwkv_mu, rwkv_w0,
           rwkv_w2, rwkv_a0, rwkv_a2, rwkv_g2, rwkv_k_k, rwkv_k_a, rwkv_r_k, rwkv_gn_w, rwkv_gn_b, w_out,
           norm2_g, router_group_w, router_group_b, router_expert_w, router_expert_b, moe_w_gate,
           moe_w_up, moe_w_down, final_norm_g):
    bsz, seq, d = x.shape
    n_tok = bsz * seq
    assert ada_w.shape[0] == 1, "the final rmsnorm is fused into the (single) layer's combine kernel"
    for l in range(1):
        p = dict(norm1_g=norm1_g[l], w_in=w_in[l], gmlp_ln_w=gmlp_ln_w[l], gmlp_ln_b=gmlp_ln_b[l],
                 gmlp_ws=gmlp_ws[l], gmlp_bs=gmlp_bs[l], rwkv_mu=rwkv_mu[l], rwkv_w0=rwkv_w0[l],
                 rwkv_w2=rwkv_w2[l], rwkv_a0=rwkv_a0[l], rwkv_a2=rwkv_a2[l], rwkv_g2=rwkv_g2[l],
                 rwkv_k_k=rwkv_k_k[l], rwkv_k_a=rwkv_k_a[l], rwkv_r_k=rwkv_r_k[l],
                 rwkv_gn_w=rwkv_gn_w[l], rwkv_gn_b=rwkv_gn_b[l], w_out=w_out[l], norm2_g=norm2_g[l],
                 router_group_w=router_group_w[l], router_group_b=router_group_b[l],
                 router_expert_w=router_expert_w[l], router_expert_b=router_expert_b[l])
        mod3 = _mod_call(c, ada_w[l], ada_b[l]).reshape(bsz, 6, d)
        pre = _pre_call(x, mod3, p)
        x1, h2, route, slab, counts = _scan_call(x, mod3, pre, p)
        dest, meta = _route_call(route, counts)
        dest_flat = dest.reshape(-1)
        n_slots = _n_blocks(n_tok) * MOE_BM
        meta_flat = meta.reshape(-1)
        xs = _dispatch_call(dest_flat, meta_flat, h2.reshape(n_tok * ROW_SUB, LANES), n_slots)
        yb = _expert_call(meta_flat, xs, moe_w_gate[l], moe_w_up[l], moe_w_down[l])
        x = _combine_call(dest_flat, yb, slab, x1.reshape(n_tok, d), mod3, final_norm_g, seq).reshape(bsz, seq, d)
    return x
```

```python
import functools

import jax
import jax.numpy as jnp
from jax import lax
from jax.experimental import pallas as pl
from jax.experimental.pallas import tpu as pltpu

F32 = jnp.float32
BF16 = jnp.bfloat16
I32 = jnp.int32

HEAD_DIM = 64
GMLP_WIDTH = 512
RWKV_WIDTH = 512
GMLP_CHUNK = 128
DECAY_LORA = 32
ICLR_LORA = 32
GATE_LORA = 96
LORA_PAD = 256
N_GROUPS = 4
EXPERTS_PER_GROUP = 8
N_EXPERTS = 32
RMS_EPS = 1e-6
LN_EPS = 1e-5
GN_EPS = 64e-5

LANES = 128
MXU_DIM = 256
VMEM_LIMIT = 56 * 1024 * 1024

PRE_TM = 512
PRE_SUB = 256
SCAN_TM = 512
SCAN_L = 64
HEADS_PER_GROUP = MXU_DIM // HEAD_DIM
ROUTE_T = 512
MOE_BM = 512
EXPERT_SPLIT = 2
DISPATCH_T = 1024
COMBINE_T = 512
ROUTE_ROWS = 8
DMA_UNROLL = 8


def _dot(a, b):
    return jnp.dot(a, b, preferred_element_type=F32)


def _dot_nt(a, b):
    return lax.dot_general(a, b, (((1,), (1,)), ((), ())), preferred_element_type=F32)


def _rms(x, g):
    return x * lax.rsqrt(jnp.mean(x * x, axis=-1, keepdims=True) + RMS_EPS) * g


ROW_SUB = 8


def _store_row_tiles(ref, val):
    m = val.shape[0]
    for s in range(ROW_SUB):
        ref[pl.ds(s, m, stride=ROW_SUB), :] = val[:, s * LANES:(s + 1) * LANES]


def _load_row_tiles(ref):
    m = ref.shape[0] // ROW_SUB
    return jnp.concatenate([ref[pl.ds(s, m, stride=ROW_SUB), :] for s in range(ROW_SUB)], axis=1)


def _row_tile(ref, row):
    return ref.at[pl.ds(pl.multiple_of(row * ROW_SUB, ROW_SUB), ROW_SUB), :]


def _mod_kernel(c_ref, w_ref, b_ref, o_ref):
    c = c_ref[...]
    w = w_ref[...]
    c_hi = c.astype(BF16)
    c_lo = (c - c_hi.astype(F32)).astype(BF16)
    w_hi = w.astype(BF16)
    w_lo = (w - w_hi.astype(F32)).astype(BF16)
    o_ref[...] = _dot(c_hi, w_hi) + _dot(c_lo, w_hi) + _dot(c_hi, w_lo) + b_ref[...]


def _mod_call(c, ada_w, ada_b):
    bsz, d = c.shape
    n = ada_w.shape[1]
    tn = 1024
    return pl.pallas_call(
        _mod_kernel,
        grid=(n // tn,),
        in_specs=[pl.BlockSpec((bsz, d), lambda j: (0, 0)),
                  pl.BlockSpec((d, tn), lambda j: (0, j)),
                  pl.BlockSpec((1, tn), lambda j: (0, j))],
        out_specs=pl.BlockSpec((bsz, tn), lambda j: (0, j)),
        out_shape=jax.ShapeDtypeStruct((bsz, n), F32),
        compiler_params=pltpu.CompilerParams(dimension_semantics=("arbitrary",),
                                             vmem_limit_bytes=VMEM_LIMIT),
        name="adaln_mod",
    )(c, ada_w, ada_b.reshape(1, n))


def _token_shift(z, carry_ref):
    tm = z.shape[0]
    prev = pltpu.roll(z, 1, axis=0)
    row = lax.broadcasted_iota(I32, z.shape, 0)
    prev = jnp.where(row == 0, carry_ref[0:1, :], prev)
    carry_ref[0:1, :] = z[tm - 1:tm, :]
    return prev


def _pre_kernel(x_ref, mod_ref, g1_ref, wuv_ref, wrkv_ref, wl_ref, lnw_ref, lnb_ref, wsp_ref,
                bsx_ref, mu_rkv_ref, mu_l_ref, lora_w_ref, w0_ref, a0_ref, kk_ref, ka_ref, ones_ref,
                ya_ref, r_ref, k2_ref, v_ref, kkn_ref, al_ref, g_ref, lw_ref,
                carry_rkv, carry_l):
    tm = x_ref.shape[1]

    @pl.when(pl.program_id(1) == 0)
    def _():
        carry_rkv[...] = jnp.zeros_like(carry_rkv)
        carry_l[...] = jnp.zeros_like(carry_l)

    shift1 = mod_ref[0, 0:1, :]
    scale1 = mod_ref[0, 1:2, :]
    sub = min(PRE_SUB, tm)
    zs = []
    for i in range(tm // sub):
        x = x_ref[0, i * sub:(i + 1) * sub, :]
        hb = (_rms(x, g1_ref[...]) * (1.0 + scale1) + shift1).astype(BF16)
        zs.append((_dot(hb, wuv_ref[...]),
                   _dot(hb, wrkv_ref[...]),
                   _dot(hb, wl_ref[...])))

    lane = lax.broadcasted_iota(I32, (GMLP_CHUNK, LANES), 1)
    left = lane < HEAD_DIM
    trow = lax.broadcasted_iota(I32, (GMLP_CHUNK, 2 * GMLP_CHUNK), 0)
    tcol = lax.broadcasted_iota(I32, (GMLP_CHUNK, 2 * GMLP_CHUNK), 1) % GMLP_CHUNK
    causal = tcol <= trow
    zero = jnp.zeros((), BF16)
    wpairs = [jnp.where(causal, wsp_ref[p], 0.0).astype(BF16) for p in range(GMLP_WIDTH // LANES)]

    for i, (zuv, zr, zl) in enumerate(zs):
        base = i * sub
        u = jax.nn.gelu(zuv[:, :GMLP_WIDTH])
        v = jax.nn.gelu(zuv[:, GMLP_WIDTH:])
        m = jnp.mean(v, axis=-1, keepdims=True)
        var = jnp.mean(jnp.square(v - m), axis=-1, keepdims=True)
        vn = ((v - m) * lax.rsqrt(var + LN_EPS) * lnw_ref[...] + lnb_ref[...]).astype(BF16)
        for p, wpair in enumerate(wpairs):
            for c in range(sub // GMLP_CHUNK):
                rows = slice(c * GMLP_CHUNK, (c + 1) * GMLP_CHUNK)
                cols = slice(p * LANES, (p + 1) * LANES)
                vc = vn[rows, cols]
                rhs = jnp.concatenate([jnp.where(left, vc, zero), jnp.where(left, zero, vc)], axis=0)
                mixed = _dot(wpair, rhs) + bsx_ref[:, cols]
                ya_ref[0, base + c * GMLP_CHUNK:base + (c + 1) * GMLP_CHUNK, cols] = (
                    u[rows, cols] * mixed).astype(ya_ref.dtype)

        out = slice(base, base + sub)
        zr = zr + (_token_shift(zr, carry_rkv) - zr) * mu_rkv_ref[...]
        zl = zl + (_token_shift(zl, carry_l) - zl) * mu_l_ref[...]
        r = zr[:, :RWKV_WIDTH]
        k = zr[:, RWKV_WIDTH:2 * RWKV_WIDTH]
        vv = zr[:, 2 * RWKV_WIDTH:]
        llane = lax.broadcasted_iota(I32, zl.shape, 1)
        f = jnp.where(llane < DECAY_LORA, jnp.tanh(zl),
                      jnp.where(llane < DECAY_LORA + ICLR_LORA, zl, jax.nn.sigmoid(zl)))
        lo = _dot(f.astype(BF16), lora_w_ref[...])
        wpre = w0_ref[...] + lo[:, :RWKV_WIDTH]
        w_log = -(jnp.maximum(-wpre, 0.0) + jnp.log(1.0 + jnp.exp(-jnp.abs(wpre)))) - 0.5
        lw_ref[0, out, :] = -jnp.exp(w_log)
        alpha = jax.nn.sigmoid(a0_ref[...] + lo[:, RWKV_WIDTH:2 * RWKV_WIDTH])
        g_ref[0, out, :] = lo[:, 2 * RWKV_WIDTH:].astype(g_ref.dtype)
        kkr = k * kk_ref[...]
        ss = _dot((kkr * kkr).astype(BF16), ones_ref[...])
        kkn_ref[0, out, :] = (kkr * lax.rsqrt(jnp.maximum(ss, 1e-24))).astype(kkn_ref.dtype)
        k2_ref[0, out, :] = (k * (1.0 + (alpha - 1.0) * ka_ref[...])).astype(k2_ref.dtype)
        r_ref[0, out, :] = r.astype(r_ref.dtype)
        v_ref[0, out, :] = vv.astype(v_ref.dtype)
        al_ref[0, out, :] = alpha.astype(al_ref.dtype)


def _head_ones(width):
    hid = jnp.arange(width) // HEAD_DIM
    return (hid[:, None] == hid[None, :]).astype(BF16)


def _pre_call(x, mod3, p):
    bsz, seq, d = x.shape
    tm = min(PRE_TM, seq)
    w_in = p["w_in"]
    wuv = w_in[:, :2 * GMLP_WIDTH].astype(BF16)
    wrkv = w_in[:, 2 * GMLP_WIDTH:2 * GMLP_WIDTH + 3 * RWKV_WIDTH].astype(BF16)
    n_lora = DECAY_LORA + ICLR_LORA + GATE_LORA
    wl = jnp.pad(w_in[:, 2 * GMLP_WIDTH + 3 * RWKV_WIDTH:], ((0, 0), (0, LORA_PAD - n_lora))).astype(BF16)
    mu = p["rwkv_mu"]
    mu_rkv = mu[:3 * RWKV_WIDTH].reshape(1, -1)
    mu_l = jnp.pad(mu[3 * RWKV_WIDTH:], (0, LORA_PAD - n_lora)).reshape(1, -1)
    zpad = lambda a, lo_, hi_: jnp.pad(a, ((lo_, LORA_PAD - hi_), (0, 0)))
    lora_w = jnp.concatenate([
        zpad(p["rwkv_w2"], 0, DECAY_LORA),
        zpad(p["rwkv_a2"], DECAY_LORA, DECAY_LORA + ICLR_LORA),
        zpad(p["rwkv_g2"], DECAY_LORA + ICLR_LORA, n_lora)], axis=1).astype(BF16)
    ws = p["gmlp_ws"]
    wsp = jnp.concatenate([ws[0::2], ws[1::2]], axis=2)
    bsx = jnp.repeat(p["gmlp_bs"].T, HEAD_DIM, axis=1)
    row = lambda a: a.reshape(1, -1)
    full = lambda a: pl.BlockSpec(a.shape, lambda b, s: (0,) * a.ndim)
    args = [mod3, row(p["norm1_g"]), wuv, wrkv, wl, row(p["gmlp_ln_w"]), row(p["gmlp_ln_b"]), wsp, bsx,
            mu_rkv, mu_l, lora_w, row(p["rwkv_w0"]), row(p["rwkv_a0"]), row(p["rwkv_k_k"]),
            row(p["rwkv_k_a"]), _head_ones(RWKV_WIDTH)]
    in_specs = [pl.BlockSpec((1, tm, d), lambda b, s: (b, s, 0)),
                pl.BlockSpec((1,) + mod3.shape[1:], lambda b, s: (b, 0, 0))]
    in_specs += [full(a) for a in args[1:]]
    tok = lambda dt: jax.ShapeDtypeStruct((bsz, seq, RWKV_WIDTH), dt)
    out_spec = pl.BlockSpec((1, tm, RWKV_WIDTH), lambda b, s: (b, s, 0))
    outs = [tok(BF16)] * 7 + [tok(F32)]
    return pl.pallas_call(
        _pre_kernel,
        grid=(bsz, seq // tm),
        in_specs=in_specs,
        out_specs=[out_spec] * 8,
        out_shape=outs,
        scratch_shapes=[pltpu.VMEM((8, 3 * RWKV_WIDTH), F32), pltpu.VMEM((8, LORA_PAD), F32)],
        compiler_params=pltpu.CompilerParams(dimension_semantics=("arbitrary", "arbitrary"),
                                             vmem_limit_bytes=VMEM_LIMIT),
        name="pre_mix",
    )(x, *args)


def _blockdiag(xb, mask):
    t = jnp.concatenate([xb] * HEADS_PER_GROUP, axis=0)
    return jnp.where(mask, t, jnp.zeros((), xb.dtype))


def _head_transpose(x):
    xt = x.T
    return jnp.concatenate([xt[hh * HEAD_DIM:(hh + 1) * HEAD_DIM, :] for hh in range(HEADS_PER_GROUP)], axis=1)


def _scan_kernel(x_ref, mod_ref, ya_ref, r_ref, k2_ref, v_ref, kk_ref, al_ref, g_ref, lw_ref,
                 rk_ref, gnw_ref, gnb_ref, ones_ref, wout_ref, g2n_ref, wr_ref, br_ref,
                 x1_ref, h2_ref, route_ref, slab_ref, cnt_ref, state_ref):
    tm = x_ref.shape[1]
    L = SCAN_L
    n_chunks = tm // L
    n_groups = RWKV_WIDTH // MXU_DIM

    @pl.when(pl.program_id(1) == 0)
    def _():
        state_ref[...] = jnp.zeros_like(state_ref)

    @pl.when(jnp.logical_and(pl.program_id(0) == 0, pl.program_id(1) == 0))
    def _():
        cnt_ref[...] = jnp.zeros_like(cnt_ref)

    r = r_ref[0].astype(F32)
    k2 = k2_ref[0].astype(F32)
    v = v_ref[0].astype(F32)
    kk = kk_ref[0].astype(F32)
    b = kk * al_ref[0].astype(F32)
    lw = lw_ref[0]
    lw_hi = lw.astype(BF16)
    lw_lo = (lw - lw_hi.astype(F32)).astype(BF16)

    trow = lax.broadcasted_iota(I32, (L, L), 0)
    tcol = lax.broadcasted_iota(I32, (L, L), 1)
    tri = (tcol <= trow).astype(BF16)
    row = lax.broadcasted_iota(I32, (L, MXU_DIM), 0)
    lane = lax.broadcasted_iota(I32, (L, MXU_DIM), 1)
    lane_j = lane % HEAD_DIM
    strict = lane_j < row
    incl = lane_j <= row
    eye = lane_j == row
    brow = lax.broadcasted_iota(I32, (MXU_DIM, MXU_DIM), 0) // HEAD_DIM
    bcol = lax.broadcasted_iota(I32, (MXU_DIM, MXU_DIM), 1) // HEAD_DIM
    bmask = brow == bcol

    def bd(xf):
        return _blockdiag(xf.astype(BF16), bmask)

    probs = [(c, g) for c in range(n_chunks) for g in range(n_groups)]
    each = lambda fn, *lists: [fn(*a) for a in zip(*lists)]
    lgs = [_dot(tri, lw_hi[c * L:(c + 1) * L]) + _dot(tri, lw_lo[c * L:(c + 1) * L])
           for c in range(n_chunks)]

    def sl(arr, c, g):
        return arr[c * L:(c + 1) * L, g * MXU_DIM:(g + 1) * MXU_DIM]

    lgc = [lgs[c][:, g * MXU_DIM:(g + 1) * MXU_DIM] for c, g in probs]
    glast = [x[L - 1:L, :] for x in lgc]
    eg = [jnp.exp(x) for x in lgc]
    egx = [jnp.exp(x - sl(lw, c, g)) for x, (c, g) in zip(lgc, probs)]
    einv = [jnp.exp(-x) for x in lgc]
    ehat = each(lambda gl, x: jnp.exp(gl - x), glast, lgc)
    at = [-sl(kk, c, g) * e for e, (c, g) in zip(egx, probs)]
    rt = [sl(r, c, g) * e for e, (c, g) in zip(eg, probs)]
    bt = [sl(b, c, g) * e for e, (c, g) in zip(einv, probs)]
    kt = [sl(k2, c, g) * e for e, (c, g) in zip(einv, probs)]
    bh = [sl(b, c, g) * e for e, (c, g) in zip(ehat, probs)]
    kh = [sl(k2, c, g) * e for e, (c, g) in zip(ehat, probs)]
    vc = [sl(v, c, g) for c, g in probs]
    lhs = each(lambda a_, r_: jnp.concatenate([a_, r_], axis=0).astype(BF16), at, rt)
    pb = each(lambda l_, x: _dot_nt(l_, bd(x)), lhs, bt)
    pk = each(lambda l_, x: _dot_nt(l_, bd(x)), lhs, kt)
    a_ab = [jnp.where(strict, x[:L], 0.0) for x in pb]
    a_rb = [jnp.where(incl, x[L:], 0.0) for x in pb]
    a_ak = [jnp.where(strict, x[:L], 0.0) for x in pk]
    a_rk = [jnp.where(incl, x[L:], 0.0) for x in pk]
    ssum = [jnp.where(eye, 1.0, 0.0) + x for x in a_ab]
    xb = [x.astype(BF16) for x in a_ab]
    xp = [_dot(x, _blockdiag(x, bmask)) for x in xb]
    m = 2
    while m < L // 2:
        xb = [x.astype(BF16) for x in xp]
        res = each(lambda x, s_: _dot(jnp.concatenate([x, s_.astype(BF16)], axis=0), _blockdiag(x, bmask)),
                   xb, ssum)
        xp = [x[:L] for x in res]
        ssum = each(lambda s_, x: s_ + x[L:], ssum, res)
        m *= 2
    tmat = each(lambda s_, x: s_ + _dot(s_.astype(BF16), bd(x)), ssum, xp)
    bht = [_head_transpose(x).astype(BF16) for x in bh]
    kht = [_head_transpose(x) for x in kh]
    res = each(lambda a1, a2, k_, x: _dot(jnp.concatenate([a1, a2, k_], axis=0).astype(BF16), bd(x)),
               a_ak, a_rk, kht, vc)
    akv = [x[:L] for x in res]
    arkv = [x[L:2 * L] for x in res]
    khv = [x[2 * L:] for x in res]
    tb = [x.astype(BF16) for x in tmat]
    wm = each(lambda t_, x: _dot(t_, bd(x)), tb, at)
    u0 = each(lambda t_, x: _dot(t_, bd(x)), tb, akv)
    lhs_w = each(lambda a_, b_: jnp.concatenate([a_.astype(BF16), b_], axis=0), a_rb, bht)
    res_w = each(lambda l_, x: _dot(l_, bd(x)), lhs_w, wm)
    res_u = each(lambda l_, x: _dot(l_, bd(x)), lhs_w, u0)
    qp = each(lambda r_, x: r_ + x[:L], rt, res_w)
    mt = each(lambda x, gl: x[L:] + jnp.where(eye, jnp.exp(gl), 0.0), res_w, glast)
    y0 = each(lambda x, z: x[:L] + z, res_u, arkv)
    nt = each(lambda x, z: x[L:] + z, res_u, khv)
    lhs2 = each(lambda q_, m_: jnp.concatenate([q_, m_], axis=0).astype(BF16), qp, mt)

    st = [state_ref[g] for g in range(n_groups)]
    y_rows = []
    for c in range(n_chunks):
        y_heads = []
        for g in range(n_groups):
            i = c * n_groups + g
            res_s = _dot(lhs2[i], bd(st[g]))
            y_heads.append(res_s[:L] + y0[i])
            st[g] = res_s[L:] + nt[i]
        y_rows.append(jnp.concatenate(y_heads, axis=1))
    for g in range(n_groups):
        state_ref[g] = st[g]

    y = jnp.concatenate(y_rows, axis=0)
    ones = ones_ref[...]
    inv_n = 1.0 / HEAD_DIM
    mean = _dot(y.astype(BF16), ones) * inv_n
    dlt = y - mean
    var = _dot((dlt * dlt).astype(BF16), ones) * inv_n
    yn = dlt * lax.rsqrt(var + GN_EPS) * gnw_ref[...] + gnb_ref[...]
    bonus = _dot((r * k2 * rk_ref[...]).astype(BF16), ones) * v
    yb = (yn + bonus) * g_ref[0].astype(F32)
    cat = jnp.concatenate([ya_ref[0], yb.astype(BF16)], axis=1)
    y2 = _dot(cat, wout_ref[...])
    gate1 = mod_ref[0, 2:3, :]
    shift2 = mod_ref[0, 3:4, :]
    scale2 = mod_ref[0, 4:5, :]
    x1 = x_ref[0] + gate1 * y2
    x1_ref[0] = x1
    h2 = _rms(x1, g2n_ref[...]) * (1.0 + scale2) + shift2
    _store_row_tiles(h2_ref.at[0], h2)

    logits = _dot(h2.astype(BF16), wr_ref[...]) + br_ref[...]
    slab = _route_slab(logits)
    slab_ref[...] = slab
    slab_t = slab.T[:ROUTE_ROWS, :]
    route_ref[...] = slab_t
    erow = lax.broadcasted_iota(I32, (N_EXPERTS, tm), 0)
    hits = ((erow == slab_t[0:1, :].astype(I32)).astype(F32)
            + (erow == slab_t[1:2, :].astype(I32)).astype(F32))
    cnt_ref[...] += jnp.sum(hits, axis=1, keepdims=True)


def _scan_call(x, mod3, pre_outs, p):
    bsz, seq, d = x.shape
    tm = min(SCAN_TM, seq)
    n_tok = bsz * seq
    steps = seq // tm
    n_route = N_GROUPS + N_EXPERTS
    wr = jnp.pad(jnp.concatenate([p["router_group_w"], p["router_expert_w"]], axis=1),
                 ((0, 0), (0, LANES - n_route))).astype(BF16)
    br = jnp.pad(jnp.concatenate([p["router_group_b"], p["router_expert_b"]]), (0, LANES - n_route)).reshape(1, -1)
    row = lambda a: a.reshape(1, -1)
    full = lambda a: pl.BlockSpec(a.shape, lambda b, s: (0,) * a.ndim)
    params = [row(p["rwkv_r_k"]), row(p["rwkv_gn_w"]), row(p["rwkv_gn_b"]), _head_ones(RWKV_WIDTH),
              p["w_out"].astype(BF16), row(p["norm2_g"]), wr, br]
    tok_spec = pl.BlockSpec((1, tm, RWKV_WIDTH), lambda b, s: (b, s, 0))
    x_spec = pl.BlockSpec((1, tm, d), lambda b, s: (b, s, 0))
    in_specs = [x_spec, pl.BlockSpec((1,) + mod3.shape[1:], lambda b, s: (b, 0, 0))]
    in_specs += [tok_spec] * 8 + [full(a) for a in params]
    return pl.pallas_call(
        _scan_kernel,
        grid=(bsz, steps),
        in_specs=in_specs,
        out_specs=[x_spec, pl.BlockSpec((1, tm * ROW_SUB, LANES), lambda b, s: (b, s, 0)),
                   pl.BlockSpec((ROUTE_ROWS, tm), lambda b, s: (0, b * steps + s)),
                   pl.BlockSpec((tm, LANES), lambda b, s: (b * steps + s, 0)),
                   pl.BlockSpec((N_EXPERTS, LANES), lambda b, s: (0, 0))],
        out_shape=[jax.ShapeDtypeStruct((bsz, seq, d), F32),
                   jax.ShapeDtypeStruct((bsz, seq * ROW_SUB, LANES), F32),
                   jax.ShapeDtypeStruct((ROUTE_ROWS, n_tok), F32), jax.ShapeDtypeStruct((n_tok, LANES), F32),
                   jax.ShapeDtypeStruct((N_EXPERTS, LANES), F32)],
        scratch_shapes=[pltpu.VMEM((RWKV_WIDTH // MXU_DIM, HEAD_DIM, MXU_DIM), F32)],
        compiler_params=pltpu.CompilerParams(dimension_semantics=("arbitrary", "arbitrary"),
                                             vmem_limit_bytes=VMEM_LIMIT),
        name="scan_post",
    )(x, mod3, *pre_outs, *params)


def _route_slab(logits):
    ln = lax.broadcasted_iota(I32, logits.shape, 1)
    neg = jnp.float32(-jnp.inf)
    is_g = ln < N_GROUPS
    gl = jnp.where(is_g, logits, neg)
    gmax = jnp.max(gl, axis=-1, keepdims=True)
    gidx = jnp.min(jnp.where(gl == gmax, ln, LANES), axis=-1, keepdims=True)
    g_p = 1.0 / jnp.sum(jnp.where(is_g, jnp.exp(logits - gmax), 0.0), axis=-1, keepdims=True)
    egrp = (ln - N_GROUPS) // EXPERTS_PER_GROUP
    sel = jnp.where(ln >= N_GROUPS, egrp, -1) == gidx
    el = jnp.where(sel, logits, neg)
    l1 = jnp.max(el, axis=-1, keepdims=True)
    i1 = jnp.min(jnp.where(el == l1, ln, LANES), axis=-1, keepdims=True)
    el2 = jnp.where(ln == i1, neg, el)
    l2 = jnp.max(el2, axis=-1, keepdims=True)
    i2 = jnp.min(jnp.where(el2 == l2, ln, LANES), axis=-1, keepdims=True)
    t = jnp.exp(l2 - l1)
    p1 = 1.0 / (1.0 + t)
    w1 = g_p * p1
    w2 = g_p * (t * p1)
    e1 = (i1 - N_GROUPS).astype(F32)
    e2 = (i2 - N_GROUPS).astype(F32)
    return jnp.where(ln == 0, e1, jnp.where(ln == 1, e2, jnp.where(ln == 2, w1, jnp.where(ln == 3, w2, 0.0))))


def _onehots(route_ref):
    t = route_ref.shape[1]
    erow = lax.broadcasted_iota(I32, (N_EXPERTS, t), 0)
    oh1 = erow == route_ref[0:1, :].astype(I32)
    oh2 = erow == route_ref[1:2, :].astype(I32)
    return oh1, oh2


def _dest_kernel(route_ref, cnt_ref, dest_ref, meta_ref, base_ref, upper_ref):
    t = route_ref.shape[1]
    nbp = meta_ref.shape[1]

    @pl.when(pl.program_id(0) == 0)
    def _():
        sr = lax.broadcasted_iota(I32, (t, t), 0)
        sc = lax.broadcasted_iota(I32, (t, t), 1)
        upper_ref[...] = (sr < sc).astype(BF16)
        cnt = cnt_ref[...]
        padded = jnp.floor((cnt + (MOE_BM - 1)) * (1.0 / MOE_BM)) * MOE_BM
        er = lax.broadcasted_iota(I32, (N_EXPERTS, N_EXPERTS), 0)
        ec = lax.broadcasted_iota(I32, (N_EXPERTS, N_EXPERTS), 1)
        lower = (ec < er).astype(BF16)
        p_hi = padded.astype(BF16)
        p_lo = (padded - p_hi.astype(F32)).astype(BF16)
        pstart = _dot(lower, p_hi) + _dot(lower, p_lo)
        base_ref[...] = pstart
        pend = (pstart + padded)[:, 0:1]
        blk0 = (lax.broadcasted_iota(I32, (N_EXPERTS, nbp), 1) * MOE_BM).astype(F32)
        be = jnp.sum((pend <= blk0).astype(I32), axis=0, keepdims=True)
        meta_ref[0:1, :] = jnp.minimum(be, N_EXPERTS - 1)
        used = (pend[N_EXPERTS - 1:N_EXPERTS, :] * (1.0 / MOE_BM)).astype(I32)
        meta_ref[1:2, :] = jnp.broadcast_to(used, (1, nbp))
        on_diag = (lax.broadcasted_iota(I32, (N_EXPERTS, nbp), 0)
                   == lax.broadcasted_iota(I32, (N_EXPERTS, nbp), 1))
        pad_lo = (pstart + cnt)[:, 0:1]
        pad_len = (padded - cnt)[:, 0:1]
        meta_ref[2:3, :] = jnp.sum(jnp.where(on_diag, pad_lo, 0.0), axis=0, keepdims=True).astype(I32)
        meta_ref[3:4, :] = jnp.sum(jnp.where(on_diag, pad_len, 0.0), axis=0, keepdims=True).astype(I32)
        e_id = lax.broadcasted_iota(I32, (N_EXPERTS, nbp), 0)
        later = jnp.logical_and(e_id > jnp.minimum(be, N_EXPERTS - 1), padded[:, 0:1] > 0.0)
        nxt = jnp.min(jnp.where(later, e_id, N_EXPERTS), axis=0, keepdims=True)
        meta_ref[4:5, :] = jnp.where(nxt == N_EXPERTS, -1, nxt)
        meta_ref[5:, :] = jnp.zeros((meta_ref.shape[0] - 5, nbp), I32)

    oh1, oh2 = _onehots(route_ref)
    cnt = oh1.astype(F32) + oh2.astype(F32)
    pos = _dot(cnt.astype(BF16), upper_ref[...]) + base_ref[:, 0:1]
    dest_ref[0:1, :] = jnp.sum(jnp.where(oh1, pos, 0.0), axis=0, keepdims=True).astype(I32)
    dest_ref[1:2, :] = jnp.sum(jnp.where(oh2, pos, 0.0), axis=0, keepdims=True).astype(I32)
    base_ref[...] += jnp.sum(cnt, axis=1, keepdims=True)


def _n_blocks(n_tok):
    return -(-2 * n_tok // MOE_BM) + N_EXPERTS


def _route_call(route, counts):
    n_tok = route.shape[1]
    t = min(ROUTE_T, n_tok)
    nbp = -(-_n_blocks(n_tok) // LANES) * LANES
    cparams = pltpu.CompilerParams(dimension_semantics=("arbitrary",), vmem_limit_bytes=VMEM_LIMIT)
    route_spec = pl.BlockSpec((ROUTE_ROWS, t), lambda j: (0, j))
    cnt_spec = pl.BlockSpec((N_EXPERTS, LANES), lambda j: (0, 0))
    dest, meta = pl.pallas_call(
        _dest_kernel, grid=(n_tok // t,), in_specs=[route_spec, cnt_spec],
        out_specs=[pl.BlockSpec((2, t), lambda j: (0, j)), pl.BlockSpec((8, nbp), lambda j: (0, 0))],
        out_shape=[jax.ShapeDtypeStruct((2, n_tok), I32), jax.ShapeDtypeStruct((8, nbp), I32)],
        scratch_shapes=[pltpu.VMEM((N_EXPERTS, LANES), F32), pltpu.VMEM((t, t), BF16)],
        compiler_params=cparams, name="route_dest")(route, counts)
    return dest, meta


def _zero_fill_slots(meta_ref, xs_hbm, ztile, zsem, wait):
    nbp = meta_ref.shape[0] // 8
    n_blocks = xs_hbm.shape[0] // (MOE_BM * ROW_SUB)

    def go(rows, first_slot):
        cp = pltpu.make_async_copy(
            ztile.at[pl.ds(0, rows * ROW_SUB), :],
            xs_hbm.at[pl.ds(pl.multiple_of(first_slot * ROW_SUB, ROW_SUB), rows * ROW_SUB), :], zsem)
        if wait:
            cp.wait()
        else:
            cp.start()

    def expert_pad(e, carry):
        slot = meta_ref[2 * nbp + e]
        plen = meta_ref[3 * nbp + e]
        bit = MOE_BM // 2
        while bit >= 1:
            has = (plen & bit) != 0
            pl.when(has)(functools.partial(go, bit, slot))
            slot = slot + jnp.where(has, bit, 0)
            bit //= 2
        return carry

    def tail_block(j, carry):
        go(MOE_BM, j * MOE_BM)
        return carry

    lax.fori_loop(0, N_EXPERTS, expert_pad, 0)
    lax.fori_loop(meta_ref[nbp], n_blocks, tail_block, 0)


def _dispatch_kernel(dest_ref, meta_ref, h_ref, xs_hbm, sem, zsem, ztile):
    n_tok = dest_ref.shape[0] // 2
    t = h_ref.shape[0] // ROW_SUB
    base = pl.program_id(0) * t
    first = pl.program_id(0) == 0

    @pl.when(first)
    def _():
        ztile[...] = jnp.zeros_like(ztile)
        _zero_fill_slots(meta_ref, xs_hbm, ztile, zsem, wait=False)

    def start(i, carry):
        tok = base + i
        src = _row_tile(h_ref, i)
        pltpu.make_async_copy(src, _row_tile(xs_hbm, dest_ref[tok]), sem).start(priority=0)
        pltpu.make_async_copy(src, _row_tile(xs_hbm, dest_ref[n_tok + tok]), sem).start(priority=1)
        return carry

    lax.fori_loop(0, t, start, 0, unroll=DMA_UNROLL)
    for _ in range(2):
        pltpu.make_async_copy(h_ref, xs_hbm.at[pl.ds(0, t * ROW_SUB), :], sem).wait()

    @pl.when(first)
    def _():
        _zero_fill_slots(meta_ref, xs_hbm, ztile, zsem, wait=True)


def _dispatch_call(dest_flat, meta_flat, h_tok, n_slots):
    n_tok = h_tok.shape[0] // ROW_SUB
    t = min(DISPATCH_T, n_tok)
    return pl.pallas_call(
        _dispatch_kernel,
        grid_spec=pltpu.PrefetchScalarGridSpec(
            num_scalar_prefetch=2, grid=(n_tok // t,),
            in_specs=[pl.BlockSpec((t * ROW_SUB, LANES), lambda i, dr, mr: (i, 0))],
            out_specs=pl.BlockSpec(memory_space=pl.ANY),
            scratch_shapes=[pltpu.SemaphoreType.DMA, pltpu.SemaphoreType.DMA,
                            pltpu.VMEM((MOE_BM * ROW_SUB, LANES), h_tok.dtype)]),
        out_shape=jax.ShapeDtypeStruct((n_slots * ROW_SUB, LANES), h_tok.dtype),
        compiler_params=pltpu.CompilerParams(dimension_semantics=("arbitrary",),
                                             vmem_limit_bytes=VMEM_LIMIT),
        name="moe_dispatch",
    )(dest_flat, meta_flat, h_tok)


def _expert_kernel(meta_ref, xs_ref, wg_hbm, wu_hbm, wd_hbm, yb_ref, wgf, wuf, wdf, wgb, wub, wdb,
                   wsem, slot_ref):
    j = pl.program_id(0)
    nbp = meta_ref.shape[0] // 8
    used = j < meta_ref[nbp]
    new_expert = jnp.logical_or(j == 0, meta_ref[j] != meta_ref[jnp.maximum(j - 1, 0)])

    def fetch(e, slot):
        return [pltpu.make_async_copy(src.at[e], dst.at[slot], wsem.at[slot])
                for src, dst in ((wg_hbm, wgf), (wu_hbm, wuf), (wd_hbm, wdf))]

    @pl.when(j == 0)
    def _():
        slot_ref[0] = 0
        for cp in fetch(meta_ref[0], 0):
            cp.start()

    @pl.when(jnp.logical_and(used, new_expert))
    def _():
        slot = slot_ref[0]
        for cp in fetch(meta_ref[j], slot):
            cp.wait()
        nxt = meta_ref[4 * nbp + j]

        @pl.when(nxt >= 0)
        def _():
            for cp in fetch(nxt, 1 - slot):
                cp.start()

        wgb[...] = wgf[slot].astype(BF16)
        wub[...] = wuf[slot].astype(BF16)
        wdb[...] = wdf[slot].astype(BF16)
        slot_ref[0] = 1 - slot

    @pl.when(used)
    def _():
        sub = MOE_BM // EXPERT_SPLIT
        gu = []
        for i in range(EXPERT_SPLIT):
            x = _load_row_tiles(xs_ref.at[pl.ds(i * sub * ROW_SUB, sub * ROW_SUB), :]).astype(BF16)
            gu.append((_dot(x, wgb[...]), _dot(x, wub[...])))
        for i, (gate, up) in enumerate(gu):
            hid = (jax.nn.silu(gate) * up).astype(BF16)
            _store_row_tiles(yb_ref.at[pl.ds(i * sub * ROW_SUB, sub * ROW_SUB), :], _dot(hid, wdb[...]))

    @pl.when(jnp.logical_not(used))
    def _():
        yb_ref[...] = jnp.zeros_like(yb_ref)


def _expert_call(meta_flat, xs, wg, wu, wd):
    n_slots = xs.shape[0] // ROW_SUB
    d, hid = wg.shape[1:]
    blk = pl.BlockSpec((MOE_BM * ROW_SUB, LANES), lambda j, m: (j, 0))
    return pl.pallas_call(
        _expert_kernel,
        grid_spec=pltpu.PrefetchScalarGridSpec(
            num_scalar_prefetch=1, grid=(n_slots // MOE_BM,),
            in_specs=[blk] + [pl.BlockSpec(memory_space=pl.ANY)] * 3,
            out_specs=blk,
            scratch_shapes=[pltpu.VMEM((2, d, hid), F32), pltpu.VMEM((2, d, hid), F32),
                            pltpu.VMEM((2, hid, d), F32),
                            pltpu.VMEM((d, hid), BF16), pltpu.VMEM((d, hid), BF16),
                            pltpu.VMEM((hid, d), BF16),
                            pltpu.SemaphoreType.DMA((2,)), pltpu.SMEM((1,), I32)]),
        out_shape=jax.ShapeDtypeStruct((n_slots * ROW_SUB, LANES), F32),
        compiler_params=pltpu.CompilerParams(dimension_semantics=("arbitrary",),
                                             vmem_limit_bytes=VMEM_LIMIT),
        name="moe_experts",
    )(meta_flat, xs, wg, wu, wd)


def _combine_kernel(dest_ref, yb_hbm, slab_ref, x1_ref, mod_ref, gf_ref, out_ref, gbuf, sem):
    t = x1_ref.shape[0]
    n_tok = dest_ref.shape[0] // 2
    step = pl.program_id(0)
    slot = step % 2

    def issue(tile, buf):
        base = tile * t
        for r in range(t):
            pltpu.make_async_copy(_row_tile(yb_hbm, dest_ref[base + r]),
                                  gbuf.at[buf, 0, pl.ds(r * ROW_SUB, ROW_SUB), :], sem.at[buf]).start(priority=0)
            pltpu.make_async_copy(_row_tile(yb_hbm, dest_ref[n_tok + base + r]),
                                  gbuf.at[buf, 1, pl.ds(r * ROW_SUB, ROW_SUB), :], sem.at[buf]).start(priority=1)

    @pl.when(step == 0)
    def _():
        issue(0, 0)

    has_next = step + 1 < pl.num_programs(0)
    for buf in range(2):
        @pl.when(jnp.logical_and(has_next, slot != buf))
        def _():
            issue(step + 1, buf)

    for kk in range(2):
        pltpu.make_async_copy(yb_hbm.at[pl.ds(0, t * ROW_SUB), :], gbuf.at[slot, kk], sem.at[slot]).wait()
    slab = slab_ref[...]
    y = (slab[:, 2:3] * _load_row_tiles(gbuf.at[slot, 0])
         + slab[:, 3:4] * _load_row_tiles(gbuf.at[slot, 1]))
    gate2 = mod_ref[0, 5:6, :]
    x2 = x1_ref[...] + gate2 * y
    out_ref[...] = _rms(x2, gf_ref[...])


def _combine_call(dest_flat, yb, slab, x1_tok, mod3, gf, seq):
    n_tok, d = x1_tok.shape
    t = min(COMBINE_T, seq)
    return pl.pallas_call(
        _combine_kernel,
        grid_spec=pltpu.PrefetchScalarGridSpec(
            num_scalar_prefetch=1, grid=(n_tok // t,),
            in_specs=[pl.BlockSpec(memory_space=pl.ANY),
                      pl.BlockSpec((t, LANES), lambda i, dr: (i, 0)),
                      pl.BlockSpec((t, d), lambda i, dr: (i, 0)),
                      pl.BlockSpec((1,) + mod3.shape[1:], lambda i, dr: ((i * t) // seq, 0, 0)),
                      pl.BlockSpec((1, d), lambda i, dr: (0, 0))],
            out_specs=pl.BlockSpec((t, d), lambda i, dr: (i, 0)),
            scratch_shapes=[pltpu.VMEM((2, 2, t * ROW_SUB, LANES), F32), pltpu.SemaphoreType.DMA((2,))]),
        out_shape=jax.ShapeDtypeStruct((n_tok, d), F32),
        compiler_params=pltpu.CompilerParams(dimension_semantics=("arbitrary",),
                                             vmem_limit_bytes=VMEM_LIMIT),
        name="moe_combine",
    )(dest_flat, yb, slab, x1_tok, mod3, gf.reshape(1, d))


def kernel(x, c, ada_w, ada_b, norm1_g, w_in, gmlp_ln_w, gmlp_ln_b, gmlp_ws, gmlp_bs, rwkv_mu, rwkv_w0,
           rwkv_w2, rwkv_a0, rwkv_a2, rwkv_g2, rwkv_k_k, rwkv_k_a, rwkv_r_k, rwkv_gn_w, rwkv_gn_b, w_out,
           norm2_g, router_group_w, router_group_b, router_expert_w, router_expert_b, moe_w_gate,
           moe_w_up, moe_w_down, final_norm_g):
    bsz, seq, d = x.shape
    n_tok = bsz * seq
    assert ada_w.shape[0] == 1, "the final rmsnorm is fused into the (single) layer's combine kernel"
    for l in range(1):
        p = dict(norm1_g=norm1_g[l], w_in=w_in[l], gmlp_ln_w=gmlp_ln_w[l], gmlp_ln_b=gmlp_ln_b[l],
                 gmlp_ws=gmlp_ws[l], gmlp_bs=gmlp_bs[l], rwkv_mu=rwkv_mu[l], rwkv_w0=rwkv_w0[l],
                 rwkv_w2=rwkv_w2[l], rwkv_a0=rwkv_a0[l], rwkv_a2=rwkv_a2[l], rwkv_g2=rwkv_g2[l],
                 rwkv_k_k=rwkv_k_k[l], rwkv_k_a=rwkv_k_a[l], rwkv_r_k=rwkv_r_k[l],
                 rwkv_gn_w=rwkv_gn_w[l], rwkv_gn_b=rwkv_gn_b[l], w_out=w_out[l], norm2_g=norm2_g[l],
                 router_group_w=router_group_w[l], router_group_b=router_group_b[l],
                 router_expert_w=router_expert_w[l], router_expert_b=router_expert_b[l])
        mod3 = _mod_call(c, ada_w[l], ada_b[l]).reshape(bsz, 6, d)
        pre = _pre_call(x, mod3, p)
        x1, h2, route, slab, counts = _scan_call(x, mod3, pre, p)
        dest, meta = _route_call(route, counts)
        dest_flat = dest.reshape(-1)
        n_slots = _n_blocks(n_tok) * MOE_BM
        meta_flat = meta.reshape(-1)
        xs = _dispatch_call(dest_flat, meta_flat, h2.reshape(n_tok * ROW_SUB, LANES), n_slots)
        yb = _expert_call(meta_flat, xs, moe_w_gate[l], moe_w_up[l], moe_w_down[l])
        x = _combine_call(dest_flat, yb, slab, x1.reshape(n_tok, d), mod3, final_norm_g, seq).reshape(bsz, seq, d)
    return x
```

```python
import functools

import jax
import jax.numpy as jnp
from jax import lax
from jax.experimental import pallas as pl
from jax.experimental.pallas import tpu as pltpu

F32 = jnp.float32
BF16 = jnp.bfloat16
I32 = jnp.int32

HEAD_DIM = 64
GMLP_WIDTH = 512
RWKV_WIDTH = 512
GMLP_CHUNK = 128
DECAY_LORA = 32
ICLR_LORA = 32
GATE_LORA = 96
LORA_PAD = 256
N_GROUPS = 4
EXPERTS_PER_GROUP = 8
N_EXPERTS = 32
RMS_EPS = 1e-6
LN_EPS = 1e-5
GN_EPS = 64e-5

LANES = 128
MXU_DIM = 256
VMEM_LIMIT = 56 * 1024 * 1024

PRE_TM = 1024
PRE_SUB = 256
SCAN_TM = 512
SCAN_L = 64
HEADS_PER_GROUP = MXU_DIM // HEAD_DIM
ROUTE_T = 512
MOE_BM = 512
EXPERT_SPLIT = 2
DISPATCH_T = 2048
COMBINE_T = 1024
ROUTE_ROWS = 8
DMA_UNROLL = 8


def _dot(a, b):
    return jnp.dot(a, b, preferred_element_type=F32)


def _dot_nt(a, b):
    return lax.dot_general(a, b, (((1,), (1,)), ((), ())), preferred_element_type=F32)


def _rms(x, g):
    return x * lax.rsqrt(jnp.mean(x * x, axis=-1, keepdims=True) + RMS_EPS) * g


ROW_SUB = 8


def _store_row_tiles(ref, val):
    m = val.shape[0]
    for s in range(ROW_SUB):
        ref[pl.ds(s, m, stride=ROW_SUB), :] = val[:, s * LANES:(s + 1) * LANES]


def _load_row_tiles(ref):
    m = ref.shape[0] // ROW_SUB
    return jnp.concatenate([ref[pl.ds(s, m, stride=ROW_SUB), :] for s in range(ROW_SUB)], axis=1)


def _row_tile(ref, row):
    return ref.at[pl.ds(pl.multiple_of(row * ROW_SUB, ROW_SUB), ROW_SUB), :]


def _mod_kernel(c_ref, w_ref, b_ref, o_ref):
    c = c_ref[...]
    w = w_ref[...]
    c_hi = c.astype(BF16)
    c_lo = (c - c_hi.astype(F32)).astype(BF16)
    w_hi = w.astype(BF16)
    w_lo = (w - w_hi.astype(F32)).astype(BF16)
    o_ref[...] = _dot(c_hi, w_hi) + _dot(c_lo, w_hi) + _dot(c_hi, w_lo) + b_ref[...]


def _mod_call(c, ada_w, ada_b):
    bsz, d = c.shape
    n = ada_w.shape[1]
    tn = 1024
    return pl.pallas_call(
        _mod_kernel,
        grid=(n // tn,),
        in_specs=[pl.BlockSpec((bsz, d), lambda j: (0, 0)),
                  pl.BlockSpec((d, tn), lambda j: (0, j)),
                  pl.BlockSpec((1, tn), lambda j: (0, j))],
        out_specs=pl.BlockSpec((bsz, tn), lambda j: (0, j)),
        out_shape=jax.ShapeDtypeStruct((bsz, n), F32),
        compiler_params=pltpu.CompilerParams(dimension_semantics=("arbitrary",),
                                             vmem_limit_bytes=VMEM_LIMIT),
        name="adaln_mod",
    )(c, ada_w, ada_b.reshape(1, n))


def _token_shift(z, carry_ref):
    tm = z.shape[0]
    prev = pltpu.roll(z, 1, axis=0)
    row = lax.broadcasted_iota(I32, z.shape, 0)
    prev = jnp.where(row == 0, carry_ref[0:1, :], prev)
    carry_ref[0:1, :] = z[tm - 1:tm, :]
    return prev


def _pre_kernel(x_ref, mod_ref, g1_ref, wuv_ref, wrkv_ref, wl_ref, lnw_ref, lnb_ref, wsp_ref,
                bsx_ref, mu_rkv_ref, mu_l_ref, lora_w_ref, w0_ref, a0_ref, kk_ref, ka_ref, ones_ref,
                ya_ref, r_ref, k2_ref, v_ref, kkn_ref, al_ref, g_ref, lw_ref,
                carry_rkv, carry_l):
    tm = x_ref.shape[1]

    @pl.when(pl.program_id(1) == 0)
    def _():
        carry_rkv[...] = jnp.zeros_like(carry_rkv)
        carry_l[...] = jnp.zeros_like(carry_l)

    shift1 = mod_ref[0, 0:1, :]
    scale1 = mod_ref[0, 1:2, :]
    sub = min(PRE_SUB, tm)
    zs = []
    for i in range(tm // sub):
        x = x_ref[0, i * sub:(i + 1) * sub, :]
        hb = (_rms(x, g1_ref[...]) * (1.0 + scale1) + shift1).astype(BF16)
        zs.append((_dot(hb, wuv_ref[...]),
                   _dot(hb, wrkv_ref[...]),
                   _dot(hb, wl_ref[...])))

    lane = lax.broadcasted_iota(I32, (GMLP_CHUNK, LANES), 1)
    left = lane < HEAD_DIM
    trow = lax.broadcasted_iota(I32, (GMLP_CHUNK, 2 * GMLP_CHUNK), 0)
    tcol = lax.broadcasted_iota(I32, (GMLP_CHUNK, 2 * GMLP_CHUNK), 1) % GMLP_CHUNK
    causal = tcol <= trow
    zero = jnp.zeros((), BF16)
    wpairs = [jnp.where(causal, wsp_ref[p], 0.0).astype(BF16) for p in range(GMLP_WIDTH // LANES)]

    for i, (zuv, zr, zl) in enumerate(zs):
        base = i * sub
        u = jax.nn.gelu(zuv[:, :GMLP_WIDTH])
        v = jax.nn.gelu(zuv[:, GMLP_WIDTH:])
        m = jnp.mean(v, axis=-1, keepdims=True)
        var = jnp.mean(jnp.square(v - m), axis=-1, keepdims=True)
        vn = ((v - m) * lax.rsqrt(var + LN_EPS) * lnw_ref[...] + lnb_ref[...]).astype(BF16)
        for p, wpair in enumerate(wpairs):
            for c in range(sub // GMLP_CHUNK):
                rows = slice(c * GMLP_CHUNK, (c + 1) * GMLP_CHUNK)
                cols = slice(p * LANES, (p + 1) * LANES)
                vc = vn[rows, cols]
                rhs = jnp.concatenate([jnp.where(left, vc, zero), jnp.where(left, zero, vc)], axis=0)
                mixed = _dot(wpair, rhs) + bsx_ref[:, cols]
                ya_ref[0, base + c * GMLP_CHUNK:base + (c + 1) * GMLP_CHUNK, cols] = (
                    u[rows, cols] * mixed).astype(ya_ref.dtype)

        out = slice(base, base + sub)
        zr = zr + (_token_shift(zr, carry_rkv) - zr) * mu_rkv_ref[...]
        zl = zl + (_token_shift(zl, carry_l) - zl) * mu_l_ref[...]
        r = zr[:, :RWKV_WIDTH]
        k = zr[:, RWKV_WIDTH:2 * RWKV_WIDTH]
        vv = zr[:, 2 * RWKV_WIDTH:]
        llane = lax.broadcasted_iota(I32, zl.shape, 1)
        f = jnp.where(llane < DECAY_LORA, jnp.tanh(zl),
                      jnp.where(llane < DECAY_LORA + ICLR_LORA, zl, jax.nn.sigmoid(zl)))
        lo = _dot(f.astype(BF16), lora_w_ref[...])
        wpre = w0_ref[...] + lo[:, :RWKV_WIDTH]
        w_log = -(jnp.maximum(-wpre, 0.0) + jnp.log(1.0 + jnp.exp(-jnp.abs(wpre)))) - 0.5
        lw_ref[0, out, :] = -jnp.exp(w_log)
        alpha = jax.nn.sigmoid(a0_ref[...] + lo[:, RWKV_WIDTH:2 * RWKV_WIDTH])
        g_ref[0, out, :] = lo[:, 2 * RWKV_WIDTH:].astype(g_ref.dtype)
        kkr = k * kk_ref[...]
        ss = _dot((kkr * kkr).astype(BF16), ones_ref[...])
        kkn_ref[0, out, :] = (kkr * lax.rsqrt(jnp.maximum(ss, 1e-24))).astype(kkn_ref.dtype)
        k2_ref[0, out, :] = (k * (1.0 + (alpha - 1.0) * ka_ref[...])).astype(k2_ref.dtype)
        r_ref[0, out, :] = r.astype(r_ref.dtype)
        v_ref[0, out, :] = vv.astype(v_ref.dtype)
        al_ref[0, out, :] = alpha.astype(al_ref.dtype)


def _head_ones(width):
    hid = jnp.arange(width) // HEAD_DIM
    return (hid[:, None] == hid[None, :]).astype(BF16)


def _pre_call(x, mod3, p):
    bsz, seq, d = x.shape
    tm = min(PRE_TM, seq)
    w_in = p["w_in"]
    wuv = w_in[:, :2 * GMLP_WIDTH].astype(BF16)
    wrkv = w_in[:, 2 * GMLP_WIDTH:2 * GMLP_WIDTH + 3 * RWKV_WIDTH].astype(BF16)
    n_lora = DECAY_LORA + ICLR_LORA + GATE_LORA
    wl = jnp.pad(w_in[:, 2 * GMLP_WIDTH + 3 * RWKV_WIDTH:], ((0, 0), (0, LORA_PAD - n_lora))).astype(BF16)
    mu = p["rwkv_mu"]
    mu_rkv = mu[:3 * RWKV_WIDTH].reshape(1, -1)
    mu_l = jnp.pad(mu[3 * RWKV_WIDTH:], (0, LORA_PAD - n_lora)).reshape(1, -1)
    zpad = lambda a, lo_, hi_: jnp.pad(a, ((lo_, LORA_PAD - hi_), (0, 0)))
    lora_w = jnp.concatenate([
        zpad(p["rwkv_w2"], 0, DECAY_LORA),
        zpad(p["rwkv_a2"], DECAY_LORA, DECAY_LORA + ICLR_LORA),
        zpad(p["rwkv_g2"], DECAY_LORA + ICLR_LORA, n_lora)], axis=1).astype(BF16)
    ws = p["gmlp_ws"]
    wsp = jnp.concatenate([ws[0::2], ws[1::2]], axis=2)
    bsx = jnp.repeat(p["gmlp_bs"].T, HEAD_DIM, axis=1)
    row = lambda a: a.reshape(1, -1)
    full = lambda a: pl.BlockSpec(a.shape, lambda b, s: (0,) * a.ndim)
    args = [mod3, row(p["norm1_g"]), wuv, wrkv, wl, row(p["gmlp_ln_w"]), row(p["gmlp_ln_b"]), wsp, bsx,
            mu_rkv, mu_l, lora_w, row(p["rwkv_w0"]), row(p["rwkv_a0"]), row(p["rwkv_k_k"]),
            row(p["rwkv_k_a"]), _head_ones(RWKV_WIDTH)]
    in_specs = [pl.BlockSpec((1, tm, d), lambda b, s: (b, s, 0)),
                pl.BlockSpec((1,) + mod3.shape[1:], lambda b, s: (b, 0, 0))]
    in_specs += [full(a) for a in args[1:]]
    tok = lambda dt: jax.ShapeDtypeStruct((bsz, seq, RWKV_WIDTH), dt)
    out_spec = pl.BlockSpec((1, tm, RWKV_WIDTH), lambda b, s: (b, s, 0))
    outs = [tok(BF16)] * 7 + [tok(F32)]
    return pl.pallas_call(
        _pre_kernel,
        grid=(bsz, seq // tm),
        in_specs=in_specs,
        out_specs=[out_spec] * 8,
        out_shape=outs,
        scratch_shapes=[pltpu.VMEM((8, 3 * RWKV_WIDTH), F32), pltpu.VMEM((8, LORA_PAD), F32)],
        compiler_params=pltpu.CompilerParams(dimension_semantics=("arbitrary", "arbitrary"),
                                             vmem_limit_bytes=VMEM_LIMIT),
        name="pre_mix",
    )(x, *args)


def _blockdiag(xb, mask):
    t = jnp.concatenate([xb] * HEADS_PER_GROUP, axis=0)
    return jnp.where(mask, t, jnp.zeros((), xb.dtype))


def _head_transpose(x):
    xt = x.T
    return jnp.concatenate([xt[hh * HEAD_DIM:(hh + 1) * HEAD_DIM, :] for hh in range(HEADS_PER_GROUP)], axis=1)


def _scan_kernel(x_ref, mod_ref, ya_ref, r_ref, k2_ref, v_ref, kk_ref, al_ref, g_ref, lw_ref,
                 rk_ref, gnw_ref, gnb_ref, ones_ref, wout_ref, g2n_ref, wr_ref, br_ref,
                 x1_ref, h2_ref, route_ref, slab_ref, cnt_ref, state_ref):
    tm = x_ref.shape[1]
    L = SCAN_L
    n_chunks = tm // L
    n_groups = RWKV_WIDTH // MXU_DIM

    @pl.when(pl.program_id(1) == 0)
    def _():
        state_ref[...] = jnp.zeros_like(state_ref)

    @pl.when(jnp.logical_and(pl.program_id(0) == 0, pl.program_id(1) == 0))
    def _():
        cnt_ref[...] = jnp.zeros_like(cnt_ref)

    r = r_ref[0].astype(F32)
    k2 = k2_ref[0].astype(F32)
    v = v_ref[0].astype(F32)
    kk = kk_ref[0].astype(F32)
    b = kk * al_ref[0].astype(F32)
    lw = lw_ref[0]
    lw_hi = lw.astype(BF16)
    lw_lo = (lw - lw_hi.astype(F32)).astype(BF16)

    trow = lax.broadcasted_iota(I32, (L, L), 0)
    tcol = lax.broadcasted_iota(I32, (L, L), 1)
    tri = (tcol <= trow).astype(BF16)
    row = lax.broadcasted_iota(I32, (L, MXU_DIM), 0)
    lane = lax.broadcasted_iota(I32, (L, MXU_DIM), 1)
    lane_j = lane % HEAD_DIM
    strict = lane_j < row
    incl = lane_j <= row
    eye = lane_j == row
    brow = lax.broadcasted_iota(I32, (MXU_DIM, MXU_DIM), 0) // HEAD_DIM
    bcol = lax.broadcasted_iota(I32, (MXU_DIM, MXU_DIM), 1) // HEAD_DIM
    bmask = brow == bcol

    def bd(xf):
        return _blockdiag(xf.astype(BF16), bmask)

    probs = [(c, g) for c in range(n_chunks) for g in range(n_groups)]
    each = lambda fn, *lists: [fn(*a) for a in zip(*lists)]
    lgs = [_dot(tri, lw_hi[c * L:(c + 1) * L]) + _dot(tri, lw_lo[c * L:(c + 1) * L])
           for c in range(n_chunks)]

    def sl(arr, c, g):
        return arr[c * L:(c + 1) * L, g * MXU_DIM:(g + 1) * MXU_DIM]

    lgc = [lgs[c][:, g * MXU_DIM:(g + 1) * MXU_DIM] for c, g in probs]
    glast = [x[L - 1:L, :] for x in lgc]
    eg = [jnp.exp(x) for x in lgc]
    egx = [jnp.exp(x - sl(lw, c, g)) for x, (c, g) in zip(lgc, probs)]
    einv = [jnp.exp(-x) for x in lgc]
    ehat = each(lambda gl, x: jnp.exp(gl - x), glast, lgc)
    at = [-sl(kk, c, g) * e for e, (c, g) in zip(egx, probs)]
    rt = [sl(r, c, g) * e for e, (c, g) in zip(eg, probs)]
    bt = [sl(b, c, g) * e for e, (c, g) in zip(einv, probs)]
    kt = [sl(k2, c, g) * e for e, (c, g) in zip(einv, probs)]
    bh = [sl(b, c, g) * e for e, (c, g) in zip(ehat, probs)]
    kh = [sl(k2, c, g) * e for e, (c, g) in zip(ehat, probs)]
    vc = [sl(v, c, g) for c, g in probs]
    lhs = each(lambda a_, r_: jnp.concatenate([a_, r_], axis=0).astype(BF16), at, rt)
    pb = each(lambda l_, x: _dot_nt(l_, bd(x)), lhs, bt)
    pk = each(lambda l_, x: _dot_nt(l_, bd(x)), lhs, kt)
    a_ab = [jnp.where(strict, x[:L], 0.0) for x in pb]
    a_rb = [jnp.where(incl, x[L:], 0.0) for x in pb]
    a_ak = [jnp.where(strict, x[:L], 0.0) for x in pk]
    a_rk = [jnp.where(incl, x[L:], 0.0) for x in pk]
    ssum = [jnp.where(eye, 1.0, 0.0) + x for x in a_ab]
    xb = [x.astype(BF16) for x in a_ab]
    xp = [_dot(x, _blockdiag(x, bmask)) for x in xb]
    m = 2
    while m < L // 2:
        xb = [x.astype(BF16) for x in xp]
        res = each(lambda x, s_: _dot(jnp.concatenate([x, s_.astype(BF16)], axis=0), _blockdiag(x, bmask)),
                   xb, ssum)
        xp = [x[:L] for x in res]
        ssum = each(lambda s_, x: s_ + x[L:], ssum, res)
        m *= 2
    tmat = each(lambda s_, x: s_ + _dot(s_.astype(BF16), bd(x)), ssum, xp)
    bht = [_head_transpose(x).astype(BF16) for x in bh]
    kht = [_head_transpose(x) for x in kh]
    res = each(lambda a1, a2, k_, x: _dot(jnp.concatenate([a1, a2, k_], axis=0).astype(BF16), bd(x)),
               a_ak, a_rk, kht, vc)
    akv = [x[:L] for x in res]
    arkv = [x[L:2 * L] for x in res]
    khv = [x[2 * L:] for x in res]
    tb = [x.astype(BF16) for x in tmat]
    wm = each(lambda t_, x: _dot(t_, bd(x)), tb, at)
    u0 = each(lambda t_, x: _dot(t_, bd(x)), tb, akv)
    lhs_w = each(lambda a_, b_: jnp.concatenate([a_.astype(BF16), b_], axis=0), a_rb, bht)
    res_w = each(lambda l_, x: _dot(l_, bd(x)), lhs_w, wm)
    res_u = each(lambda l_, x: _dot(l_, bd(x)), lhs_w, u0)
    qp = each(lambda r_, x: r_ + x[:L], rt, res_w)
    mt = each(lambda x, gl: x[L:] + jnp.where(eye, jnp.exp(gl), 0.0), res_w, glast)
    y0 = each(lambda x, z: x[:L] + z, res_u, arkv)
    nt = each(lambda x, z: x[L:] + z, res_u, khv)
    lhs2 = each(lambda q_, m_: jnp.concatenate([q_, m_], axis=0).astype(BF16), qp, mt)

    st = [state_ref[g] for g in range(n_groups)]
    y_rows = []
    for c in range(n_chunks):
        y_heads = []
        for g in range(n_groups):
            i = c * n_groups + g
            res_s = _dot(lhs2[i], bd(st[g]))
            y_heads.append(res_s[:L] + y0[i])
            st[g] = res_s[L:] + nt[i]
        y_rows.append(jnp.concatenate(y_heads, axis=1))
    for g in range(n_groups):
        state_ref[g] = st[g]

    y = jnp.concatenate(y_rows, axis=0)
    ones = ones_ref[...]
    inv_n = 1.0 / HEAD_DIM
    mean = _dot(y.astype(BF16), ones) * inv_n
    dlt = y - mean
    var = _dot((dlt * dlt).astype(BF16), ones) * inv_n
    yn = dlt * lax.rsqrt(var + GN_EPS) * gnw_ref[...] + gnb_ref[...]
    bonus = _dot((r * k2 * rk_ref[...]).astype(BF16), ones) * v
    yb = (yn + bonus) * g_ref[0].astype(F32)
    cat = jnp.concatenate([ya_ref[0], yb.astype(BF16)], axis=1)
    y2 = _dot(cat, wout_ref[...])
    gate1 = mod_ref[0, 2:3, :]
    shift2 = mod_ref[0, 3:4, :]
    scale2 = mod_ref[0, 4:5, :]
    x1 = x_ref[0] + gate1 * y2
    x1_ref[0] = x1
    h2 = _rms(x1, g2n_ref[...]) * (1.0 + scale2) + shift2
    _store_row_tiles(h2_ref.at[0], h2)

    logits = _dot(h2.astype(BF16), wr_ref[...]) + br_ref[...]
    slab = _route_slab(logits)
    slab_ref[...] = slab
    slab_t = slab.T[:ROUTE_ROWS, :]
    route_ref[...] = slab_t
    erow = lax.broadcasted_iota(I32, (N_EXPERTS, tm), 0)
    hits = ((erow == slab_t[0:1, :].astype(I32)).astype(F32)
            + (erow == slab_t[1:2, :].astype(I32)).astype(F32))
    cnt_ref[...] += jnp.sum(hits, axis=1, keepdims=True)


def _scan_call(x, mod3, pre_outs, p):
    bsz, seq, d = x.shape
    tm = min(SCAN_TM, seq)
    n_tok = bsz * seq
    steps = seq // tm
    n_route = N_GROUPS + N_EXPERTS
    wr = jnp.pad(jnp.concatenate([p["router_group_w"], p["router_expert_w"]], axis=1),
                 ((0, 0), (0, LANES - n_route))).astype(BF16)
    br = jnp.pad(jnp.concatenate([p["router_group_b"], p["router_expert_b"]]), (0, LANES - n_route)).reshape(1, -1)
    row = lambda a: a.reshape(1, -1)
    full = lambda a: pl.BlockSpec(a.shape, lambda b, s: (0,) * a.ndim)
    params = [row(p["rwkv_r_k"]), row(p["rwkv_gn_w"]), row(p["rwkv_gn_b"]), _head_ones(RWKV_WIDTH),
              p["w_out"].astype(BF16), row(p["norm2_g"]), wr, br]
    tok_spec = pl.BlockSpec((1, tm, RWKV_WIDTH), lambda b, s: (b, s, 0))
    x_spec = pl.BlockSpec((1, tm, d), lambda b, s: (b, s, 0))
    in_specs = [x_spec, pl.BlockSpec((1,) + mod3.shape[1:], lambda b, s: (b, 0, 0))]
    in_specs += [tok_spec] * 8 + [full(a) for a in params]
    return pl.pallas_call(
        _scan_kernel,
        grid=(bsz, steps),
        in_specs=in_specs,
        out_specs=[x_spec, pl.BlockSpec((1, tm * ROW_SUB, LANES), lambda b, s: (b, s, 0)),
                   pl.BlockSpec((ROUTE_ROWS, tm), lambda b, s: (0, b * steps + s)),
                   pl.BlockSpec((tm, LANES), lambda b, s: (b * steps + s, 0)),
                   pl.BlockSpec((N_EXPERTS, LANES), lambda b, s: (0, 0))],
        out_shape=[jax.ShapeDtypeStruct((bsz, seq, d), F32),
                   jax.ShapeDtypeStruct((bsz, seq * ROW_SUB, LANES), F32),
                   jax.ShapeDtypeStruct((ROUTE_ROWS, n_tok), F32), jax.ShapeDtypeStruct((n_tok, LANES), F32),
                   jax.ShapeDtypeStruct((N_EXPERTS, LANES), F32)],
        scratch_shapes=[pltpu.VMEM((RWKV_WIDTH // MXU_DIM, HEAD_DIM, MXU_DIM), F32)],
        compiler_params=pltpu.CompilerParams(dimension_semantics=("arbitrary", "arbitrary"),
                                             vmem_limit_bytes=VMEM_LIMIT),
        name="scan_post",
    )(x, mod3, *pre_outs, *params)


def _route_slab(logits):
    ln = lax.broadcasted_iota(I32, logits.shape, 1)
    neg = jnp.float32(-jnp.inf)
    is_g = ln < N_GROUPS
    gl = jnp.where(is_g, logits, neg)
    gmax = jnp.max(gl, axis=-1, keepdims=True)
    gidx = jnp.min(jnp.where(gl == gmax, ln, LANES), axis=-1, keepdims=True)
    g_p = 1.0 / jnp.sum(jnp.where(is_g, jnp.exp(logits - gmax), 0.0), axis=-1, keepdims=True)
    egrp = (ln - N_GROUPS) // EXPERTS_PER_GROUP
    sel = jnp.where(ln >= N_GROUPS, egrp, -1) == gidx
    el = jnp.where(sel, logits, neg)
    l1 = jnp.max(el, axis=-1, keepdims=True)
    i1 = jnp.min(jnp.where(el == l1, ln, LANES), axis=-1, keepdims=True)
    el2 = jnp.where(ln == i1, neg, el)
    l2 = jnp.max(el2, axis=-1, keepdims=True)
    i2 = jnp.min(jnp.where(el2 == l2, ln, LANES), axis=-1, keepdims=True)
    t = jnp.exp(l2 - l1)
    p1 = 1.0 / (1.0 + t)
    w1 = g_p * p1
    w2 = g_p * (t * p1)
    e1 = (i1 - N_GROUPS).astype(F32)
    e2 = (i2 - N_GROUPS).astype(F32)
    return jnp.where(ln == 0, e1, jnp.where(ln == 1, e2, jnp.where(ln == 2, w1, jnp.where(ln == 3, w2, 0.0))))


def _onehots(route_ref):
    t = route_ref.shape[1]
    erow = lax.broadcasted_iota(I32, (N_EXPERTS, t), 0)
    oh1 = erow == route_ref[0:1, :].astype(I32)
    oh2 = erow == route_ref[1:2, :].astype(I32)
    return oh1, oh2


def _dest_kernel(route_ref, cnt_ref, dest_ref, meta_ref, base_ref, upper_ref):
    t = route_ref.shape[1]
    nbp = meta_ref.shape[1]

    @pl.when(pl.program_id(0) == 0)
    def _():
        sr = lax.broadcasted_iota(I32, (t, t), 0)
        sc = lax.broadcasted_iota(I32, (t, t), 1)
        upper_ref[...] = (sr < sc).astype(BF16)
        cnt = cnt_ref[...]
        padded = jnp.floor((cnt + (MOE_BM - 1)) * (1.0 / MOE_BM)) * MOE_BM
        er = lax.broadcasted_iota(I32, (N_EXPERTS, N_EXPERTS), 0)
        ec = lax.broadcasted_iota(I32, (N_EXPERTS, N_EXPERTS), 1)
        lower = (ec < er).astype(BF16)
        p_hi = padded.astype(BF16)
        p_lo = (padded - p_hi.astype(F32)).astype(BF16)
        pstart = _dot(lower, p_hi) + _dot(lower, p_lo)
        base_ref[...] = pstart
        pend = (pstart + padded)[:, 0:1]
        blk0 = (lax.broadcasted_iota(I32, (N_EXPERTS, nbp), 1) * MOE_BM).astype(F32)
        be = jnp.sum((pend <= blk0).astype(I32), axis=0, keepdims=True)
        meta_ref[0:1, :] = jnp.minimum(be, N_EXPERTS - 1)
        used = (pend[N_EXPERTS - 1:N_EXPERTS, :] * (1.0 / MOE_BM)).astype(I32)
        meta_ref[1:2, :] = jnp.broadcast_to(used, (1, nbp))
        on_diag = (lax.broadcasted_iota(I32, (N_EXPERTS, nbp), 0)
                   == lax.broadcasted_iota(I32, (N_EXPERTS, nbp), 1))
        pad_lo = (pstart + cnt)[:, 0:1]
        pad_len = (padded - cnt)[:, 0:1]
        meta_ref[2:3, :] = jnp.sum(jnp.where(on_diag, pad_lo, 0.0), axis=0, keepdims=True).astype(I32)
        meta_ref[3:4, :] = jnp.sum(jnp.where(on_diag, pad_len, 0.0), axis=0, keepdims=True).astype(I32)
        e_id = lax.broadcasted_iota(I32, (N_EXPERTS, nbp), 0)
        later = jnp.logical_and(e_id > jnp.minimum(be, N_EXPERTS - 1), padded[:, 0:1] > 0.0)
        nxt = jnp.min(jnp.where(later, e_id, N_EXPERTS), axis=0, keepdims=True)
        meta_ref[4:5, :] = jnp.where(nxt == N_EXPERTS, -1, nxt)
        meta_ref[5:, :] = jnp.zeros((meta_ref.shape[0] - 5, nbp), I32)

    oh1, oh2 = _onehots(route_ref)
    cnt = oh1.astype(F32) + oh2.astype(F32)
    pos = _dot(cnt.astype(BF16), upper_ref[...]) + base_ref[:, 0:1]
    dest_ref[0:1, :] = jnp.sum(jnp.where(oh1, pos, 0.0), axis=0, keepdims=True).astype(I32)
    dest_ref[1:2, :] = jnp.sum(jnp.where(oh2, pos, 0.0), axis=0, keepdims=True).astype(I32)
    base_ref[...] += jnp.sum(cnt, axis=1, keepdims=True)


def _n_blocks(n_tok):
    return -(-2 * n_tok // MOE_BM) + N_EXPERTS


def _route_call(route, counts):
    n_tok = route.shape[1]
    t = min(ROUTE_T, n_tok)
    nbp = -(-_n_blocks(n_tok) // LANES) * LANES
    cparams = pltpu.CompilerParams(dimension_semantics=("arbitrary",), vmem_limit_bytes=VMEM_LIMIT)
    route_spec = pl.BlockSpec((ROUTE_ROWS, t), lambda j: (0, j))
    cnt_spec = pl.BlockSpec((N_EXPERTS, LANES), lambda j: (0, 0))
    dest, meta = pl.pallas_call(
        _dest_kernel, grid=(n_tok // t,), in_specs=[route_spec, cnt_spec],
        out_specs=[pl.BlockSpec((2, t), lambda j: (0, j)), pl.BlockSpec((8, nbp), lambda j: (0, 0))],
        out_shape=[jax.ShapeDtypeStruct((2, n_tok), I32), jax.ShapeDtypeStruct((8, nbp), I32)],
        scratch_shapes=[pltpu.VMEM((N_EXPERTS, LANES), F32), pltpu.VMEM((t, t), BF16)],
        compiler_params=cparams, name="route_dest")(route, counts)
    return dest, meta


def _zero_fill_slots(meta_ref, xs_hbm, ztile, zsem, wait):
    nbp = meta_ref.shape[0] // 8
    n_blocks = xs_hbm.shape[0] // (MOE_BM * ROW_SUB)

    def go(rows, first_slot):
        cp = pltpu.make_async_copy(
            ztile.at[pl.ds(0, rows * ROW_SUB), :],
            xs_hbm.at[pl.ds(pl.multiple_of(first_slot * ROW_SUB, ROW_SUB), rows * ROW_SUB), :], zsem)
        if wait:
            cp.wait()
        else:
            cp.start()

    def expert_pad(e, carry):
        slot = meta_ref[2 * nbp + e]
        plen = meta_ref[3 * nbp + e]
        bit = MOE_BM // 2
        while bit >= 1:
            has = (plen & bit) != 0
            pl.when(has)(functools.partial(go, bit, slot))
            slot = slot + jnp.where(has, bit, 0)
            bit //= 2
        return carry

    def tail_block(j, carry):
        go(MOE_BM, j * MOE_BM)
        return carry

    lax.fori_loop(0, N_EXPERTS, expert_pad, 0)
    lax.fori_loop(meta_ref[nbp], n_blocks, tail_block, 0)


def _dispatch_kernel(dest_ref, meta_ref, h_ref, xs_hbm, sem, zsem, ztile):
    n_tok = dest_ref.shape[0] // 2
    t = h_ref.shape[0] // ROW_SUB
    base = pl.program_id(0) * t
    first = pl.program_id(0) == 0

    @pl.when(first)
    def _():
        ztile[...] = jnp.zeros_like(ztile)
        _zero_fill_slots(meta_ref, xs_hbm, ztile, zsem, wait=False)

    def start(i, carry):
        tok = base + i
        src = _row_tile(h_ref, i)
        pltpu.make_async_copy(src, _row_tile(xs_hbm, dest_ref[tok]), sem).start(priority=0)
        pltpu.make_async_copy(src, _row_tile(xs_hbm, dest_ref[n_tok + tok]), sem).start(priority=1)
        return carry

    lax.fori_loop(0, t, start, 0, unroll=DMA_UNROLL)
    for _ in range(2):
        pltpu.make_async_copy(h_ref, xs_hbm.at[pl.ds(0, t * ROW_SUB), :], sem).wait()

    @pl.when(first)
    def _():
        _zero_fill_slots(meta_ref, xs_hbm, ztile, zsem, wait=True)


def _dispatch_call(dest_flat, meta_flat, h_tok, n_slots):
    n_tok = h_tok.shape[0] // ROW_SUB
    t = min(DISPATCH_T, n_tok)
    return pl.pallas_call(
        _dispatch_kernel,
        grid_spec=pltpu.PrefetchScalarGridSpec(
            num_scalar_prefetch=2, grid=(n_tok // t,),
            in_specs=[pl.BlockSpec((t * ROW_SUB, LANES), lambda i, dr, mr: (i, 0))],
            out_specs=pl.BlockSpec(memory_space=pl.ANY),
            scratch_shapes=[pltpu.SemaphoreType.DMA, pltpu.SemaphoreType.DMA,
                            pltpu.VMEM((MOE_BM * ROW_SUB, LANES), h_tok.dtype)]),
        out_shape=jax.ShapeDtypeStruct((n_slots * ROW_SUB, LANES), h_tok.dtype),
        compiler_params=pltpu.CompilerParams(dimension_semantics=("arbitrary",),
                                             vmem_limit_bytes=VMEM_LIMIT),
        name="moe_dispatch",
    )(dest_flat, meta_flat, h_tok)


def _expert_kernel(meta_ref, xs_ref, wg_hbm, wu_hbm, wd_hbm, yb_ref, wgf, wuf, wdf, wgb, wub, wdb,
                   wsem, slot_ref):
    j = pl.program_id(0)
    nbp = meta_ref.shape[0] // 8
    used = j < meta_ref[nbp]
    new_expert = jnp.logical_or(j == 0, meta_ref[j] != meta_ref[jnp.maximum(j - 1, 0)])

    def fetch(e, slot):
        return [pltpu.make_async_copy(src.at[e], dst.at[slot], wsem.at[slot])
                for src, dst in ((wg_hbm, wgf), (wu_hbm, wuf), (wd_hbm, wdf))]

    @pl.when(j == 0)
    def _():
        slot_ref[0] = 0
        for cp in fetch(meta_ref[0], 0):
            cp.start()

    @pl.when(jnp.logical_and(used, new_expert))
    def _():
        slot = slot_ref[0]
        for cp in fetch(meta_ref[j], slot):
            cp.wait()
        nxt = meta_ref[4 * nbp + j]

        @pl.when(nxt >= 0)
        def _():
            for cp in fetch(nxt, 1 - slot):
                cp.start()

        wgb[...] = wgf[slot].astype(BF16)
        wub[...] = wuf[slot].astype(BF16)
        wdb[...] = wdf[slot].astype(BF16)
        slot_ref[0] = 1 - slot

    @pl.when(used)
    def _():
        sub = MOE_BM // EXPERT_SPLIT
        gu = []
        for i in range(EXPERT_SPLIT):
            x = _load_row_tiles(xs_ref.at[pl.ds(i * sub * ROW_SUB, sub * ROW_SUB), :]).astype(BF16)
            gu.append((_dot(x, wgb[...]), _dot(x, wub[...])))
        for i, (gate, up) in enumerate(gu):
            hid = (jax.nn.silu(gate) * up).astype(BF16)
            _store_row_tiles(yb_ref.at[pl.ds(i * sub * ROW_SUB, sub * ROW_SUB), :], _dot(hid, wdb[...]))

    @pl.when(jnp.logical_not(used))
    def _():
        yb_ref[...] = jnp.zeros_like(yb_ref)


def _expert_call(meta_flat, xs, wg, wu, wd):
    n_slots = xs.shape[0] // ROW_SUB
    d, hid = wg.shape[1:]
    blk = pl.BlockSpec((MOE_BM * ROW_SUB, LANES), lambda j, m: (j, 0))
    return pl.pallas_call(
        _expert_kernel,
        grid_spec=pltpu.PrefetchScalarGridSpec(
            num_scalar_prefetch=1, grid=(n_slots // MOE_BM,),
            in_specs=[blk] + [pl.BlockSpec(memory_space=pl.ANY)] * 3,
            out_specs=blk,
            scratch_shapes=[pltpu.VMEM((2, d, hid), F32), pltpu.VMEM((2, d, hid), F32),
                            pltpu.VMEM((2, hid, d), F32),
                            pltpu.VMEM((d, hid), BF16), pltpu.VMEM((d, hid), BF16),
                            pltpu.VMEM((hid, d), BF16),
                            pltpu.SemaphoreType.DMA((2,)), pltpu.SMEM((1,), I32)]),
        out_shape=jax.ShapeDtypeStruct((n_slots * ROW_SUB, LANES), F32),
        compiler_params=pltpu.CompilerParams(dimension_semantics=("arbitrary",),
                                             vmem_limit_bytes=VMEM_LIMIT),
        name="moe_experts",
    )(meta_flat, xs, wg, wu, wd)


def _combine_kernel(dest_ref, yb_hbm, slab_ref, x1_ref, mod_ref, gf_ref, out_ref, gbuf, sem):
    t = x1_ref.shape[0]
    n_tok = dest_ref.shape[0] // 2
    step = pl.program_id(0)
    slot = step % 2

    def issue(tile, buf):
        def body(r, carry):
            tok = tile * t + r
            pltpu.make_async_copy(_row_tile(yb_hbm, dest_ref[tok]), _row_tile(gbuf.at[buf, 0], r),
                                  sem.at[buf]).start(priority=0)
            pltpu.make_async_copy(_row_tile(yb_hbm, dest_ref[n_tok + tok]), _row_tile(gbuf.at[buf, 1], r),
                                  sem.at[buf]).start(priority=1)
            return carry
        lax.fori_loop(0, t, body, 0, unroll=DMA_UNROLL)

    @pl.when(step == 0)
    def _():
        issue(step, slot)

    @pl.when(step + 1 < pl.num_programs(0))
    def _():
        issue(step + 1, 1 - slot)

    for kk in range(2):
        pltpu.make_async_copy(yb_hbm.at[pl.ds(0, t * ROW_SUB), :], gbuf.at[slot, kk], sem.at[slot]).wait()
    slab = slab_ref[...]
    y = (slab[:, 2:3] * _load_row_tiles(gbuf.at[slot, 0])
         + slab[:, 3:4] * _load_row_tiles(gbuf.at[slot, 1]))
    gate2 = mod_ref[0, 5:6, :]
    x2 = x1_ref[...] + gate2 * y
    out_ref[...] = _rms(x2, gf_ref[...])


def _combine_call(dest_flat, yb, slab, x1_tok, mod3, gf, seq):
    n_tok, d = x1_tok.shape
    t = min(COMBINE_T, seq)
    return pl.pallas_call(
        _combine_kernel,
        grid_spec=pltpu.PrefetchScalarGridSpec(
            num_scalar_prefetch=1, grid=(n_tok // t,),
            in_specs=[pl.BlockSpec(memory_space=pl.ANY),
                      pl.BlockSpec((t, LANES), lambda i, dr: (i, 0)),
                      pl.BlockSpec((t, d), lambda i, dr: (i, 0)),
                      pl.BlockSpec((1,) + mod3.shape[1:], lambda i, dr: ((i * t) // seq, 0, 0)),
                      pl.BlockSpec((1, d), lambda i, dr: (0, 0))],
            out_specs=pl.BlockSpec((t, d), lambda i, dr: (i, 0)),
            scratch_shapes=[pltpu.VMEM((2, 2, t * ROW_SUB, LANES), F32), pltpu.SemaphoreType.DMA((2,))]),
        out_shape=jax.ShapeDtypeStruct((n_tok, d), F32),
        compiler_params=pltpu.CompilerParams(dimension_semantics=("arbitrary",),
                                             vmem_limit_bytes=VMEM_LIMIT),
        name="moe_combine",
    )(dest_flat, yb, slab, x1_tok, mod3, gf.reshape(1, d))


def kernel(x, c, ada_w, ada_b, norm1_g, w_in, gmlp_ln_w, gmlp_ln_b, gmlp_ws, gmlp_bs, rwkv_mu, rwkv_w0,
           rwkv_w2, rwkv_a0, rwkv_a2, rwkv_g2, rwkv_k_k, rwkv_k_a, rwkv_r_k, rwkv_gn_w, rwkv_gn_b, w_out,
           norm2_g, router_group_w, router_group_b, router_expert_w, router_expert_b, moe_w_gate,
           moe_w_up, moe_w_down, final_norm_g):
    bsz, seq, d = x.shape
    n_tok = bsz * seq
    assert ada_w.shape[0] == 1, "the final rmsnorm is fused into the (single) layer's combine kernel"
    for l in range(1):
        p = dict(norm1_g=norm1_g[l], w_in=w_in[l], gmlp_ln_w=gmlp_ln_w[l], gmlp_ln_b=gmlp_ln_b[l],
                 gmlp_ws=gmlp_ws[l], gmlp_bs=gmlp_bs[l], rwkv_mu=rwkv_mu[l], rwkv_w0=rwkv_w0[l],
                 rwkv_w2=rwkv_w2[l], rwkv_a0=rwkv_a0[l], rwkv_a2=rwkv_a2[l], rwkv_g2=rwkv_g2[l],
                 rwkv_k_k=rwkv_k_k[l], rwkv_k_a=rwkv_k_a[l], rwkv_r_k=rwkv_r_k[l],
                 rwkv_gn_w=rwkv_gn_w[l], rwkv_gn_b=rwkv_gn_b[l], w_out=w_out[l], norm2_g=norm2_g[l],
                 router_group_w=router_group_w[l], router_group_b=router_group_b[l],
                 router_expert_w=router_expert_w[l], router_expert_b=router_expert_b[l])
        mod3 = _mod_call(c, ada_w[l], ada_b[l]).reshape(bsz, 6, d)
        pre = _pre_call(x, mod3, p)
        x1, h2, route, slab, counts = _scan_call(x, mod3, pre, p)
        dest, meta = _route_call(route, counts)
        dest_flat = dest.reshape(-1)
        n_slots = _n_blocks(n_tok) * MOE_BM
        meta_flat = meta.reshape(-1)
        xs = _dispatch_call(dest_flat, meta_flat, h2.reshape(n_tok * ROW_SUB, LANES), n_slots)
        yb = _expert_call(meta_flat, xs, moe_w_gate[l], moe_w_up[l], moe_w_down[l])
        x = _combine_call(dest_flat, yb, slab, x1.reshape(n_tok, d), mod3, final_norm_g, seq).reshape(bsz, seq, d)
    return x
```

```python
import functools

import jax
import jax.numpy as jnp
from jax import lax
from jax.experimental import pallas as pl
from jax.experimental.pallas import tpu as pltpu

F32 = jnp.float32
BF16 = jnp.bfloat16
I32 = jnp.int32

HEAD_DIM = 64
GMLP_WIDTH = 512
RWKV_WIDTH = 512
GMLP_CHUNK = 128
DECAY_LORA = 32
ICLR_LORA = 32
GATE_LORA = 96
LORA_PAD = 256
N_GROUPS = 4
EXPERTS_PER_GROUP = 8
N_EXPERTS = 32
RMS_EPS = 1e-6
LN_EPS = 1e-5
GN_EPS = 64e-5

LANES = 128
MXU_DIM = 256
VMEM_LIMIT = 56 * 1024 * 1024

PRE_TM = 1024
PRE_SUB = 256
SCAN_TM = 512
SCAN_L = 64
HEADS_PER_GROUP = MXU_DIM // HEAD_DIM
ROUTE_T = 512
MOE_BM = 512
EXPERT_SPLIT = 2
DISPATCH_T = 2048
COMBINE_T = 512
ROUTE_ROWS = 8
DMA_UNROLL = 8


def _dot(a, b):
    return jnp.dot(a, b, preferred_element_type=F32)


def _dot_nt(a, b):
    return lax.dot_general(a, b, (((1,), (1,)), ((), ())), preferred_element_type=F32)


def _rms(x, g):
    return x * lax.rsqrt(jnp.mean(x * x, axis=-1, keepdims=True) + RMS_EPS) * g


ROW_SUB = 8


def _store_row_tiles(ref, val):
    m = val.shape[0]
    for s in range(ROW_SUB):
        ref[pl.ds(s, m, stride=ROW_SUB), :] = val[:, s * LANES:(s + 1) * LANES]


def _load_row_tiles(ref):
    m = ref.shape[0] // ROW_SUB
    return jnp.concatenate([ref[pl.ds(s, m, stride=ROW_SUB), :] for s in range(ROW_SUB)], axis=1)


def _row_tile(ref, row):
    return ref.at[pl.ds(pl.multiple_of(row * ROW_SUB, ROW_SUB), ROW_SUB), :]


def _mod_kernel(c_ref, w_ref, b_ref, o_ref):
    c = c_ref[...]
    w = w_ref[...]
    c_hi = c.astype(BF16)
    c_lo = (c - c_hi.astype(F32)).astype(BF16)
    w_hi = w.astype(BF16)
    w_lo = (w - w_hi.astype(F32)).astype(BF16)
    o_ref[...] = _dot(c_hi, w_hi) + _dot(c_lo, w_hi) + _dot(c_hi, w_lo) + b_ref[...]


def _mod_call(c, ada_w, ada_b):
    bsz, d = c.shape
    n = ada_w.shape[1]
    tn = 1024
    return pl.pallas_call(
        _mod_kernel,
        grid=(n // tn,),
        in_specs=[pl.BlockSpec((bsz, d), lambda j: (0, 0)),
                  pl.BlockSpec((d, tn), lambda j: (0, j)),
                  pl.BlockSpec((1, tn), lambda j: (0, j))],
        out_specs=pl.BlockSpec((bsz, tn), lambda j: (0, j)),
        out_shape=jax.ShapeDtypeStruct((bsz, n), F32),
        compiler_params=pltpu.CompilerParams(dimension_semantics=("arbitrary",),
                                             vmem_limit_bytes=VMEM_LIMIT),
        name="adaln_mod",
    )(c, ada_w, ada_b.reshape(1, n))


def _token_shift(z, carry_ref):
    tm = z.shape[0]
    prev = pltpu.roll(z, 1, axis=0)
    row = lax.broadcasted_iota(I32, z.shape, 0)
    prev = jnp.where(row == 0, carry_ref[0:1, :], prev)
    carry_ref[0:1, :] = z[tm - 1:tm, :]
    return prev


def _pre_kernel(x_ref, mod_ref, g1_ref, wuv_ref, wrkv_ref, wl_ref, lnw_ref, lnb_ref, wsp_ref,
                bsx_ref, mu_rkv_ref, mu_l_ref, lora_w_ref, w0_ref, a0_ref, kk_ref, ka_ref, ones_ref,
                ya_ref, r_ref, k2_ref, v_ref, kkn_ref, al_ref, g_ref, lw_ref,
                carry_rkv, carry_l):
    tm = x_ref.shape[1]

    @pl.when(pl.program_id(1) == 0)
    def _():
        carry_rkv[...] = jnp.zeros_like(carry_rkv)
        carry_l[...] = jnp.zeros_like(carry_l)

    shift1 = mod_ref[0, 0:1, :]
    scale1 = mod_ref[0, 1:2, :]
    sub = min(PRE_SUB, tm)
    zs = []
    for i in range(tm // sub):
        x = x_ref[0, i * sub:(i + 1) * sub, :]
        hb = (_rms(x, g1_ref[...]) * (1.0 + scale1) + shift1).astype(BF16)
        zs.append((_dot(hb, wuv_ref[...]),
                   _dot(hb, wrkv_ref[...]),
                   _dot(hb, wl_ref[...])))

    lane = lax.broadcasted_iota(I32, (GMLP_CHUNK, LANES), 1)
    left = lane < HEAD_DIM
    trow = lax.broadcasted_iota(I32, (GMLP_CHUNK, 2 * GMLP_CHUNK), 0)
    tcol = lax.broadcasted_iota(I32, (GMLP_CHUNK, 2 * GMLP_CHUNK), 1) % GMLP_CHUNK
    causal = tcol <= trow
    zero = jnp.zeros((), BF16)
    wpairs = [jnp.where(causal, wsp_ref[p], 0.0).astype(BF16) for p in range(GMLP_WIDTH // LANES)]

    for i, (zuv, zr, zl) in enumerate(zs):
        base = i * sub
        u = jax.nn.gelu(zuv[:, :GMLP_WIDTH])
        v = jax.nn.gelu(zuv[:, GMLP_WIDTH:])
        m = jnp.mean(v, axis=-1, keepdims=True)
        var = jnp.mean(jnp.square(v - m), axis=-1, keepdims=True)
        vn = ((v - m) * lax.rsqrt(var + LN_EPS) * lnw_ref[...] + lnb_ref[...]).astype(BF16)
        for p, wpair in enumerate(wpairs):
            for c in range(sub // GMLP_CHUNK):
                rows = slice(c * GMLP_CHUNK, (c + 1) * GMLP_CHUNK)
                cols = slice(p * LANES, (p + 1) * LANES)
                vc = vn[rows, cols]
                rhs = jnp.concatenate([jnp.where(left, vc, zero), jnp.where(left, zero, vc)], axis=0)
                mixed = _dot(wpair, rhs) + bsx_ref[:, cols]
                ya_ref[0, base + c * GMLP_CHUNK:base + (c + 1) * GMLP_CHUNK, cols] = (
                    u[rows, cols] * mixed).astype(ya_ref.dtype)

        out = slice(base, base + sub)
        zr = zr + (_token_shift(zr, carry_rkv) - zr) * mu_rkv_ref[...]
        zl = zl + (_token_shift(zl, carry_l) - zl) * mu_l_ref[...]
        r = zr[:, :RWKV_WIDTH]
        k = zr[:, RWKV_WIDTH:2 * RWKV_WIDTH]
        vv = zr[:, 2 * RWKV_WIDTH:]
        llane = lax.broadcasted_iota(I32, zl.shape, 1)
        f = jnp.where(llane < DECAY_LORA, jnp.tanh(zl),
                      jnp.where(llane < DECAY_LORA + ICLR_LORA, zl, jax.nn.sigmoid(zl)))
        lo = _dot(f.astype(BF16), lora_w_ref[...])
        wpre = w0_ref[...] + lo[:, :RWKV_WIDTH]
        w_log = -(jnp.maximum(-wpre, 0.0) + jnp.log(1.0 + jnp.exp(-jnp.abs(wpre)))) - 0.5
        lw_ref[0, out, :] = -jnp.exp(w_log)
        alpha = jax.nn.sigmoid(a0_ref[...] + lo[:, RWKV_WIDTH:2 * RWKV_WIDTH])
        g_ref[0, out, :] = lo[:, 2 * RWKV_WIDTH:].astype(g_ref.dtype)
        kkr = k * kk_ref[...]
        ss = _dot((kkr * kkr).astype(BF16), ones_ref[...])
        kkn_ref[0, out, :] = (kkr * lax.rsqrt(jnp.maximum(ss, 1e-24))).astype(kkn_ref.dtype)
        k2_ref[0, out, :] = (k * (1.0 + (alpha - 1.0) * ka_ref[...])).astype(k2_ref.dtype)
        r_ref[0, out, :] = r.astype(r_ref.dtype)
        v_ref[0, out, :] = vv.astype(v_ref.dtype)
        al_ref[0, out, :] = alpha.astype(al_ref.dtype)


def _head_ones(width):
    hid = jnp.arange(width) // HEAD_DIM
    return (hid[:, None] == hid[None, :]).astype(BF16)


def _pre_call(x, mod3, p):
    bsz, seq, d = x.shape
    tm = min(PRE_TM, seq)
    w_in = p["w_in"]
    wuv = w_in[:, :2 * GMLP_WIDTH].astype(BF16)
    wrkv = w_in[:, 2 * GMLP_WIDTH:2 * GMLP_WIDTH + 3 * RWKV_WIDTH].astype(BF16)
    n_lora = DECAY_LORA + ICLR_LORA + GATE_LORA
    wl = jnp.pad(w_in[:, 2 * GMLP_WIDTH + 3 * RWKV_WIDTH:], ((0, 0), (0, LORA_PAD - n_lora))).astype(BF16)
    mu = p["rwkv_mu"]
    mu_rkv = mu[:3 * RWKV_WIDTH].reshape(1, -1)
    mu_l = jnp.pad(mu[3 * RWKV_WIDTH:], (0, LORA_PAD - n_lora)).reshape(1, -1)
    zpad = lambda a, lo_, hi_: jnp.pad(a, ((lo_, LORA_PAD - hi_), (0, 0)))
    lora_w = jnp.concatenate([
        zpad(p["rwkv_w2"], 0, DECAY_LORA),
        zpad(p["rwkv_a2"], DECAY_LORA, DECAY_LORA + ICLR_LORA),
        zpad(p["rwkv_g2"], DECAY_LORA + ICLR_LORA, n_lora)], axis=1).astype(BF16)
    ws = p["gmlp_ws"]
    wsp = jnp.concatenate([ws[0::2], ws[1::2]], axis=2)
    bsx = jnp.repeat(p["gmlp_bs"].T, HEAD_DIM, axis=1)
    row = lambda a: a.reshape(1, -1)
    full = lambda a: pl.BlockSpec(a.shape, lambda b, s: (0,) * a.ndim)
    args = [mod3, row(p["norm1_g"]), wuv, wrkv, wl, row(p["gmlp_ln_w"]), row(p["gmlp_ln_b"]), wsp, bsx,
            mu_rkv, mu_l, lora_w, row(p["rwkv_w0"]), row(p["rwkv_a0"]), row(p["rwkv_k_k"]),
            row(p["rwkv_k_a"]), _head_ones(RWKV_WIDTH)]
    in_specs = [pl.BlockSpec((1, tm, d), lambda b, s: (b, s, 0)),
                pl.BlockSpec((1,) + mod3.shape[1:], lambda b, s: (b, 0, 0))]
    in_specs += [full(a) for a in args[1:]]
    tok = lambda dt: jax.ShapeDtypeStruct((bsz, seq, RWKV_WIDTH), dt)
    out_spec = pl.BlockSpec((1, tm, RWKV_WIDTH), lambda b, s: (b, s, 0))
    outs = [tok(BF16)] * 7 + [tok(F32)]
    return pl.pallas_call(
        _pre_kernel,
        grid=(bsz, seq // tm),
        in_specs=in_specs,
        out_specs=[out_spec] * 8,
        out_shape=outs,
        scratch_shapes=[pltpu.VMEM((8, 3 * RWKV_WIDTH), F32), pltpu.VMEM((8, LORA_PAD), F32)],
        compiler_params=pltpu.CompilerParams(dimension_semantics=("arbitrary", "arbitrary"),
                                             vmem_limit_bytes=VMEM_LIMIT),
        name="pre_mix",
    )(x, *args)


def _blockdiag(xb, mask):
    t = jnp.concatenate([xb] * HEADS_PER_GROUP, axis=0)
    return jnp.where(mask, t, jnp.zeros((), xb.dtype))


def _head_transpose(x):
    xt = x.T
    return jnp.concatenate([xt[hh * HEAD_DIM:(hh + 1) * HEAD_DIM, :] for hh in range(HEADS_PER_GROUP)], axis=1)


def _scan_kernel(x_ref, mod_ref, ya_ref, r_ref, k2_ref, v_ref, kk_ref, al_ref, g_ref, lw_ref,
                 rk_ref, gnw_ref, gnb_ref, ones_ref, wout_ref, g2n_ref, wr_ref, br_ref,
                 x1_ref, h2_ref, route_ref, slab_ref, cnt_ref, state_ref):
    tm = x_ref.shape[1]
    L = SCAN_L
    n_chunks = tm // L
    n_groups = RWKV_WIDTH // MXU_DIM

    @pl.when(pl.program_id(1) == 0)
    def _():
        state_ref[...] = jnp.zeros_like(state_ref)

    @pl.when(jnp.logical_and(pl.program_id(0) == 0, pl.program_id(1) == 0))
    def _():
        cnt_ref[...] = jnp.zeros_like(cnt_ref)

    r = r_ref[0].astype(F32)
    k2 = k2_ref[0].astype(F32)
    v = v_ref[0].astype(F32)
    kk = kk_ref[0].astype(F32)
    b = kk * al_ref[0].astype(F32)
    lw = lw_ref[0]
    lw_hi = lw.astype(BF16)
    lw_lo = (lw - lw_hi.astype(F32)).astype(BF16)

    trow = lax.broadcasted_iota(I32, (L, L), 0)
    tcol = lax.broadcasted_iota(I32, (L, L), 1)
    tri = (tcol <= trow).astype(BF16)
    row = lax.broadcasted_iota(I32, (L, MXU_DIM), 0)
    lane = lax.broadcasted_iota(I32, (L, MXU_DIM), 1)
    lane_j = lane % HEAD_DIM
    strict = lane_j < row
    incl = lane_j <= row
    eye = lane_j == row
    brow = lax.broadcasted_iota(I32, (MXU_DIM, MXU_DIM), 0) // HEAD_DIM
    bcol = lax.broadcasted_iota(I32, (MXU_DIM, MXU_DIM), 1) // HEAD_DIM
    bmask = brow == bcol

    def bd(xf):
        return _blockdiag(xf.astype(BF16), bmask)

    probs = [(c, g) for c in range(n_chunks) for g in range(n_groups)]
    each = lambda fn, *lists: [fn(*a) for a in zip(*lists)]
    lgs = [_dot(tri, lw_hi[c * L:(c + 1) * L]) + _dot(tri, lw_lo[c * L:(c + 1) * L])
           for c in range(n_chunks)]

    def sl(arr, c, g):
        return arr[c * L:(c + 1) * L, g * MXU_DIM:(g + 1) * MXU_DIM]

    lgc = [lgs[c][:, g * MXU_DIM:(g + 1) * MXU_DIM] for c, g in probs]
    glast = [x[L - 1:L, :] for x in lgc]
    eg = [jnp.exp(x) for x in lgc]
    egx = [jnp.exp(x - sl(lw, c, g)) for x, (c, g) in zip(lgc, probs)]
    einv = [jnp.exp(-x) for x in lgc]
    ehat = each(lambda gl, x: jnp.exp(gl - x), glast, lgc)
    at = [-sl(kk, c, g) * e for e, (c, g) in zip(egx, probs)]
    rt = [sl(r, c, g) * e for e, (c, g) in zip(eg, probs)]
    bt = [sl(b, c, g) * e for e, (c, g) in zip(einv, probs)]
    kt = [sl(k2, c, g) * e for e, (c, g) in zip(einv, probs)]
    bh = [sl(b, c, g) * e for e, (c, g) in zip(ehat, probs)]
    kh = [sl(k2, c, g) * e for e, (c, g) in zip(ehat, probs)]
    vc = [sl(v, c, g) for c, g in probs]
    lhs = each(lambda a_, r_: jnp.concatenate([a_, r_], axis=0).astype(BF16), at, rt)
    pb = each(lambda l_, x: _dot_nt(l_, bd(x)), lhs, bt)
    pk = each(lambda l_, x: _dot_nt(l_, bd(x)), lhs, kt)
    a_ab = [jnp.where(strict, x[:L], 0.0) for x in pb]
    a_rb = [jnp.where(incl, x[L:], 0.0) for x in pb]
    a_ak = [jnp.where(strict, x[:L], 0.0) for x in pk]
    a_rk = [jnp.where(incl, x[L:], 0.0) for x in pk]
    ssum = [jnp.where(eye, 1.0, 0.0) + x for x in a_ab]
    xb = [x.astype(BF16) for x in a_ab]
    xp = [_dot(x, _blockdiag(x, bmask)) for x in xb]
    m = 2
    while m < L // 2:
        xb = [x.astype(BF16) for x in xp]
        res = each(lambda x, s_: _dot(jnp.concatenate([x, s_.astype(BF16)], axis=0), _blockdiag(x, bmask)),
                   xb, ssum)
        xp = [x[:L] for x in res]
        ssum = each(lambda s_, x: s_ + x[L:], ssum, res)
        m *= 2
    tmat = each(lambda s_, x: s_ + _dot(s_.astype(BF16), bd(x)), ssum, xp)
    bht = [_head_transpose(x).astype(BF16) for x in bh]
    kht = [_head_transpose(x) for x in kh]
    res = each(lambda a1, a2, k_, x: _dot(jnp.concatenate([a1, a2, k_], axis=0).astype(BF16), bd(x)),
               a_ak, a_rk, kht, vc)
    akv = [x[:L] for x in res]
    arkv = [x[L:2 * L] for x in res]
    khv = [x[2 * L:] for x in res]
    tb = [x.astype(BF16) for x in tmat]
    wm = each(lambda t_, x: _dot(t_, bd(x)), tb, at)
    u0 = each(lambda t_, x: _dot(t_, bd(x)), tb, akv)
    lhs_w = each(lambda a_, b_: jnp.concatenate([a_.astype(BF16), b_], axis=0), a_rb, bht)
    res_w = each(lambda l_, x: _dot(l_, bd(x)), lhs_w, wm)
    res_u = each(lambda l_, x: _dot(l_, bd(x)), lhs_w, u0)
    qp = each(lambda r_, x: r_ + x[:L], rt, res_w)
    mt = each(lambda x, gl: x[L:] + jnp.where(eye, jnp.exp(gl), 0.0), res_w, glast)
    y0 = each(lambda x, z: x[:L] + z, res_u, arkv)
    nt = each(lambda x, z: x[L:] + z, res_u, khv)
    lhs2 = each(lambda q_, m_: jnp.concatenate([q_, m_], axis=0).astype(BF16), qp, mt)

    st = [state_ref[g] for g in range(n_groups)]
    y_rows = []
    for c in range(n_chunks):
        y_heads = []
        for g in range(n_groups):
            i = c * n_groups + g
            res_s = _dot(lhs2[i], bd(st[g]))
            y_heads.append(res_s[:L] + y0[i])
            st[g] = res_s[L:] + nt[i]
        y_rows.append(jnp.concatenate(y_heads, axis=1))
    for g in range(n_groups):
        state_ref[g] = st[g]

    y = jnp.concatenate(y_rows, axis=0)
    ones = ones_ref[...]
    inv_n = 1.0 / HEAD_DIM
    mean = _dot(y.astype(BF16), ones) * inv_n
    dlt = y - mean
    var = _dot((dlt * dlt).astype(BF16), ones) * inv_n
    yn = dlt * lax.rsqrt(var + GN_EPS) * gnw_ref[...] + gnb_ref[...]
    bonus = _dot((r * k2 * rk_ref[...]).astype(BF16), ones) * v
    yb = (yn + bonus) * g_ref[0].astype(F32)
    cat = jnp.concatenate([ya_ref[0], yb.astype(BF16)], axis=1)
    y2 = _dot(cat, wout_ref[...])
    gate1 = mod_ref[0, 2:3, :]
    shift2 = mod_ref[0, 3:4, :]
    scale2 = mod_ref[0, 4:5, :]
    x1 = x_ref[0] + gate1 * y2
    x1_ref[0] = x1
    h2 = _rms(x1, g2n_ref[...]) * (1.0 + scale2) + shift2
    _store_row_tiles(h2_ref.at[0], h2)

    logits = _dot(h2.astype(BF16), wr_ref[...]) + br_ref[...]
    slab = _route_slab(logits)
    slab_ref[...] = slab
    slab_t = slab.T[:ROUTE_ROWS, :]
    route_ref[...] = slab_t
    erow = lax.broadcasted_iota(I32, (N_EXPERTS, tm), 0)
    hits = ((erow == slab_t[0:1, :].astype(I32)).astype(F32)
            + (erow == slab_t[1:2, :].astype(I32)).astype(F32))
    cnt_ref[...] += jnp.sum(hits, axis=1, keepdims=True)


def _scan_call(x, mod3, pre_outs, p):
    bsz, seq, d = x.shape
    tm = min(SCAN_TM, seq)
    n_tok = bsz * seq
    steps = seq // tm
    n_route = N_GROUPS + N_EXPERTS
    wr = jnp.pad(jnp.concatenate([p["router_group_w"], p["router_expert_w"]], axis=1),
                 ((0, 0), (0, LANES - n_route))).astype(BF16)
    br = jnp.pad(jnp.concatenate([p["router_group_b"], p["router_expert_b"]]), (0, LANES - n_route)).reshape(1, -1)
    row = lambda a: a.reshape(1, -1)
    full = lambda a: pl.BlockSpec(a.shape, lambda b, s: (0,) * a.ndim)
    params = [row(p["rwkv_r_k"]), row(p["rwkv_gn_w"]), row(p["rwkv_gn_b"]), _head_ones(RWKV_WIDTH),
              p["w_out"].astype(BF16), row(p["norm2_g"]), wr, br]
    tok_spec = pl.BlockSpec((1, tm, RWKV_WIDTH), lambda b, s: (b, s, 0))
    x_spec = pl.BlockSpec((1, tm, d), lambda b, s: (b, s, 0))
    in_specs = [x_spec, pl.BlockSpec((1,) + mod3.shape[1:], lambda b, s: (b, 0, 0))]
    in_specs += [tok_spec] * 8 + [full(a) for a in params]
    return pl.pallas_call(
        _scan_kernel,
        grid=(bsz, steps),
        in_specs=in_specs,
        out_specs=[x_spec, pl.BlockSpec((1, tm * ROW_SUB, LANES), lambda b, s: (b, s, 0)),
                   pl.BlockSpec((ROUTE_ROWS, tm), lambda b, s: (0, b * steps + s)),
                   pl.BlockSpec((tm, LANES), lambda b, s: (b * steps + s, 0)),
                   pl.BlockSpec((N_EXPERTS, LANES), lambda b, s: (0, 0))],
        out_shape=[jax.ShapeDtypeStruct((bsz, seq, d), F32),
                   jax.ShapeDtypeStruct((bsz, seq * ROW_SUB, LANES), F32),
                   jax.ShapeDtypeStruct((ROUTE_ROWS, n_tok), F32), jax.ShapeDtypeStruct((n_tok, LANES), F32),
                   jax.ShapeDtypeStruct((N_EXPERTS, LANES), F32)],
        scratch_shapes=[pltpu.VMEM((RWKV_WIDTH // MXU_DIM, HEAD_DIM, MXU_DIM), F32)],
        compiler_params=pltpu.CompilerParams(dimension_semantics=("arbitrary", "arbitrary"),
                                             vmem_limit_bytes=VMEM_LIMIT),
        name="scan_post",
    )(x, mod3, *pre_outs, *params)


def _route_slab(logits):
    ln = lax.broadcasted_iota(I32, logits.shape, 1)
    neg = jnp.float32(-jnp.inf)
    is_g = ln < N_GROUPS
    gl = jnp.where(is_g, logits, neg)
    gmax = jnp.max(gl, axis=-1, keepdims=True)
    gidx = jnp.min(jnp.where(gl == gmax, ln, LANES), axis=-1, keepdims=True)
    g_p = 1.0 / jnp.sum(jnp.where(is_g, jnp.exp(logits - gmax), 0.0), axis=-1, keepdims=True)
    egrp = (ln - N_GROUPS) // EXPERTS_PER_GROUP
    sel = jnp.where(ln >= N_GROUPS, egrp, -1) == gidx
    el = jnp.where(sel, logits, neg)
    l1 = jnp.max(el, axis=-1, keepdims=True)
    i1 = jnp.min(jnp.where(el == l1, ln, LANES), axis=-1, keepdims=True)
    el2 = jnp.where(ln == i1, neg, el)
    l2 = jnp.max(el2, axis=-1, keepdims=True)
    i2 = jnp.min(jnp.where(el2 == l2, ln, LANES), axis=-1, keepdims=True)
    t = jnp.exp(l2 - l1)
    p1 = 1.0 / (1.0 + t)
    w1 = g_p * p1
    w2 = g_p * (t * p1)
    e1 = (i1 - N_GROUPS).astype(F32)
    e2 = (i2 - N_GROUPS).astype(F32)
    return jnp.where(ln == 0, e1, jnp.where(ln == 1, e2, jnp.where(ln == 2, w1, jnp.where(ln == 3, w2, 0.0))))


def _onehots(route_ref):
    t = route_ref.shape[1]
    erow = lax.broadcasted_iota(I32, (N_EXPERTS, t), 0)
    oh1 = erow == route_ref[0:1, :].astype(I32)
    oh2 = erow == route_ref[1:2, :].astype(I32)
    return oh1, oh2


def _dest_kernel(route_ref, cnt_ref, dest_ref, meta_ref, base_ref, upper_ref):
    t = route_ref.shape[1]
    nbp = meta_ref.shape[1]

    @pl.when(pl.program_id(0) == 0)
    def _():
        sr = lax.broadcasted_iota(I32, (t, t), 0)
        sc = lax.broadcasted_iota(I32, (t, t), 1)
        upper_ref[...] = (sr < sc).astype(BF16)
        cnt = cnt_ref[...]
        padded = jnp.floor((cnt + (MOE_BM - 1)) * (1.0 / MOE_BM)) * MOE_BM
        er = lax.broadcasted_iota(I32, (N_EXPERTS, N_EXPERTS), 0)
        ec = lax.broadcasted_iota(I32, (N_EXPERTS, N_EXPERTS), 1)
        lower = (ec < er).astype(BF16)
        p_hi = padded.astype(BF16)
        p_lo = (padded - p_hi.astype(F32)).astype(BF16)
        pstart = _dot(lower, p_hi) + _dot(lower, p_lo)
        base_ref[...] = pstart
        pend = (pstart + padded)[:, 0:1]
        blk0 = (lax.broadcasted_iota(I32, (N_EXPERTS, nbp), 1) * MOE_BM).astype(F32)
        be = jnp.sum((pend <= blk0).astype(I32), axis=0, keepdims=True)
        meta_ref[0:1, :] = jnp.minimum(be, N_EXPERTS - 1)
        used = (pend[N_EXPERTS - 1:N_EXPERTS, :] * (1.0 / MOE_BM)).astype(I32)
        meta_ref[1:2, :] = jnp.broadcast_to(used, (1, nbp))
        on_diag = (lax.broadcasted_iota(I32, (N_EXPERTS, nbp), 0)
                   == lax.broadcasted_iota(I32, (N_EXPERTS, nbp), 1))
        pad_lo = (pstart + cnt)[:, 0:1]
        pad_len = (padded - cnt)[:, 0:1]
        meta_ref[2:3, :] = jnp.sum(jnp.where(on_diag, pad_lo, 0.0), axis=0, keepdims=True).astype(I32)
        meta_ref[3:4, :] = jnp.sum(jnp.where(on_diag, pad_len, 0.0), axis=0, keepdims=True).astype(I32)
        e_id = lax.broadcasted_iota(I32, (N_EXPERTS, nbp), 0)
        later = jnp.logical_and(e_id > jnp.minimum(be, N_EXPERTS - 1), padded[:, 0:1] > 0.0)
        nxt = jnp.min(jnp.where(later, e_id, N_EXPERTS), axis=0, keepdims=True)
        meta_ref[4:5, :] = jnp.where(nxt == N_EXPERTS, -1, nxt)
        meta_ref[5:, :] = jnp.zeros((meta_ref.shape[0] - 5, nbp), I32)

    oh1, oh2 = _onehots(route_ref)
    cnt = oh1.astype(F32) + oh2.astype(F32)
    pos = _dot(cnt.astype(BF16), upper_ref[...]) + base_ref[:, 0:1]
    dest_ref[0:1, :] = jnp.sum(jnp.where(oh1, pos, 0.0), axis=0, keepdims=True).astype(I32)
    dest_ref[1:2, :] = jnp.sum(jnp.where(oh2, pos, 0.0), axis=0, keepdims=True).astype(I32)
    base_ref[...] += jnp.sum(cnt, axis=1, keepdims=True)


def _n_blocks(n_tok):
    return -(-2 * n_tok // MOE_BM) + N_EXPERTS


def _route_call(route, counts):
    n_tok = route.shape[1]
    t = min(ROUTE_T, n_tok)
    nbp = -(-_n_blocks(n_tok) // LANES) * LANES
    cparams = pltpu.CompilerParams(dimension_semantics=("arbitrary",), vmem_limit_bytes=VMEM_LIMIT)
    route_spec = pl.BlockSpec((ROUTE_ROWS, t), lambda j: (0, j))
    cnt_spec = pl.BlockSpec((N_EXPERTS, LANES), lambda j: (0, 0))
    dest, meta = pl.pallas_call(
        _dest_kernel, grid=(n_tok // t,), in_specs=[route_spec, cnt_spec],
        out_specs=[pl.BlockSpec((2, t), lambda j: (0, j)), pl.BlockSpec((8, nbp), lambda j: (0, 0))],
        out_shape=[jax.ShapeDtypeStruct((2, n_tok), I32), jax.ShapeDtypeStruct((8, nbp), I32)],
        scratch_shapes=[pltpu.VMEM((N_EXPERTS, LANES), F32), pltpu.VMEM((t, t), BF16)],
        compiler_params=cparams, name="route_dest")(route, counts)
    return dest, meta


def _zero_fill_slots(meta_ref, xs_hbm, ztile, zsem, wait):
    nbp = meta_ref.shape[0] // 8
    n_blocks = xs_hbm.shape[0] // (MOE_BM * ROW_SUB)

    def go(rows, first_slot):
        cp = pltpu.make_async_copy(
            ztile.at[pl.ds(0, rows * ROW_SUB), :],
            xs_hbm.at[pl.ds(pl.multiple_of(first_slot * ROW_SUB, ROW_SUB), rows * ROW_SUB), :], zsem)
        if wait:
            cp.wait()
        else:
            cp.start()

    def expert_pad(e, carry):
        slot = meta_ref[2 * nbp + e]
        plen = meta_ref[3 * nbp + e]
        bit = MOE_BM // 2
        while bit >= 1:
            has = (plen & bit) != 0
            pl.when(has)(functools.partial(go, bit, slot))
            slot = slot + jnp.where(has, bit, 0)
            bit //= 2
        return carry

    def tail_block(j, carry):
        go(MOE_BM, j * MOE_BM)
        return carry

    lax.fori_loop(0, N_EXPERTS, expert_pad, 0)
    lax.fori_loop(meta_ref[nbp], n_blocks, tail_block, 0)


def _dispatch_kernel(dest_ref, meta_ref, h_ref, xs_hbm, sem, zsem, ztile):
    n_tok = dest_ref.shape[0] // 2
    t = h_ref.shape[0] // ROW_SUB
    base = pl.program_id(0) * t
    first = pl.program_id(0) == 0

    @pl.when(first)
    def _():
        ztile[...] = jnp.zeros_like(ztile)
        _zero_fill_slots(meta_ref, xs_hbm, ztile, zsem, wait=False)

    def start(i, carry):
        tok = base + i
        src = _row_tile(h_ref, i)
        pltpu.make_async_copy(src, _row_tile(xs_hbm, dest_ref[tok]), sem).start(priority=0)
        pltpu.make_async_copy(src, _row_tile(xs_hbm, dest_ref[n_tok + tok]), sem).start(priority=1)
        return carry

    lax.fori_loop(0, t, start, 0, unroll=DMA_UNROLL)
    for _ in range(2):
        pltpu.make_async_copy(h_ref, xs_hbm.at[pl.ds(0, t * ROW_SUB), :], sem).wait()

    @pl.when(first)
    def _():
        _zero_fill_slots(meta_ref, xs_hbm, ztile, zsem, wait=True)


def _dispatch_call(dest_flat, meta_flat, h_tok, n_slots):
    n_tok = h_tok.shape[0] // ROW_SUB
    t = min(DISPATCH_T, n_tok)
    return pl.pallas_call(
        _dispatch_kernel,
        grid_spec=pltpu.PrefetchScalarGridSpec(
            num_scalar_prefetch=2, grid=(n_tok // t,),
            in_specs=[pl.BlockSpec((t * ROW_SUB, LANES), lambda i, dr, mr: (i, 0))],
            out_specs=pl.BlockSpec(memory_space=pl.ANY),
            scratch_shapes=[pltpu.SemaphoreType.DMA, pltpu.SemaphoreType.DMA,
                            pltpu.VMEM((MOE_BM * ROW_SUB, LANES), h_tok.dtype)]),
        out_shape=jax.ShapeDtypeStruct((n_slots * ROW_SUB, LANES), h_tok.dtype),
        compiler_params=pltpu.CompilerParams(dimension_semantics=("arbitrary",),
                                             vmem_limit_bytes=VMEM_LIMIT),
        name="moe_dispatch",
    )(dest_flat, meta_flat, h_tok)


def _expert_kernel(meta_ref, xs_ref, wg_hbm, wu_hbm, wd_hbm, yb_ref, wgf, wuf, wdf, wgb, wub, wdb,
                   wsem, slot_ref):
    j = pl.program_id(0)
    nbp = meta_ref.shape[0] // 8
    used = j < meta_ref[nbp]
    new_expert = jnp.logical_or(j == 0, meta_ref[j] != meta_ref[jnp.maximum(j - 1, 0)])

    def fetch(e, slot):
        return [pltpu.make_async_copy(src.at[e], dst.at[slot], wsem.at[slot])
                for src, dst in ((wg_hbm, wgf), (wu_hbm, wuf), (wd_hbm, wdf))]

    @pl.when(j == 0)
    def _():
        slot_ref[0] = 0
        for cp in fetch(meta_ref[0], 0):
            cp.start()

    @pl.when(jnp.logical_and(used, new_expert))
    def _():
        slot = slot_ref[0]
        for cp in fetch(meta_ref[j], slot):
            cp.wait()
        nxt = meta_ref[4 * nbp + j]

        @pl.when(nxt >= 0)
        def _():
            for cp in fetch(nxt, 1 - slot):
                cp.start()

        wgb[...] = wgf[slot].astype(BF16)
        wub[...] = wuf[slot].astype(BF16)
        wdb[...] = wdf[slot].astype(BF16)
        slot_ref[0] = 1 - slot

    @pl.when(used)
    def _():
        sub = MOE_BM // EXPERT_SPLIT
        gu = []
        for i in range(EXPERT_SPLIT):
            x = _load_row_tiles(xs_ref.at[pl.ds(i * sub * ROW_SUB, sub * ROW_SUB), :]).astype(BF16)
            gu.append((_dot(x, wgb[...]), _dot(x, wub[...])))
        for i, (gate, up) in enumerate(gu):
            hid = (jax.nn.silu(gate) * up).astype(BF16)
            _store_row_tiles(yb_ref.at[pl.ds(i * sub * ROW_SUB, sub * ROW_SUB), :], _dot(hid, wdb[...]))

    @pl.when(jnp.logical_not(used))
    def _():
        yb_ref[...] = jnp.zeros_like(yb_ref)


def _expert_call(meta_flat, xs, wg, wu, wd):
    n_slots = xs.shape[0] // ROW_SUB
    d, hid = wg.shape[1:]
    blk = pl.BlockSpec((MOE_BM * ROW_SUB, LANES), lambda j, m: (j, 0))
    return pl.pallas_call(
        _expert_kernel,
        grid_spec=pltpu.PrefetchScalarGridSpec(
            num_scalar_prefetch=1, grid=(n_slots // MOE_BM,),
            in_specs=[blk] + [pl.BlockSpec(memory_space=pl.ANY)] * 3,
            out_specs=blk,
            scratch_shapes=[pltpu.VMEM((2, d, hid), F32), pltpu.VMEM((2, d, hid), F32),
                            pltpu.VMEM((2, hid, d), F32),
                            pltpu.VMEM((d, hid), BF16), pltpu.VMEM((d, hid), BF16),
                            pltpu.VMEM((hid, d), BF16),
                            pltpu.SemaphoreType.DMA((2,)), pltpu.SMEM((1,), I32)]),
        out_shape=jax.ShapeDtypeStruct((n_slots * ROW_SUB, LANES), F32),
        compiler_params=pltpu.CompilerParams(dimension_semantics=("arbitrary",),
                                             vmem_limit_bytes=VMEM_LIMIT),
        name="moe_experts",
    )(meta_flat, xs, wg, wu, wd)


def _combine_kernel(dest_ref, yb_hbm, slab_ref, x1_ref, mod_ref, gf_ref, out_ref, gbuf, sem):
    t = x1_ref.shape[0]
    n_tok = dest_ref.shape[0] // 2
    step = pl.program_id(0)
    slot = step % 2

    def issue(tile, buf):
        def body(r, carry):
            tok = tile * t + r
            pltpu.make_async_copy(_row_tile(yb_hbm, dest_ref[tok]), _row_tile(gbuf.at[buf, 0], r),
                                  sem.at[buf]).start(priority=0)
            pltpu.make_async_copy(_row_tile(yb_hbm, dest_ref[n_tok + tok]), _row_tile(gbuf.at[buf, 1], r),
                                  sem.at[buf]).start(priority=1)
            return carry
        lax.fori_loop(0, t, body, 0, unroll=DMA_UNROLL)

    @pl.when(step == 0)
    def _():
        issue(step, slot)

    @pl.when(step + 1 < pl.num_programs(0))
    def _():
        issue(step + 1, 1 - slot)

    for kk in range(2):
        pltpu.make_async_copy(yb_hbm.at[pl.ds(0, t * ROW_SUB), :], gbuf.at[slot, kk], sem.at[slot]).wait()
    slab = slab_ref[...]
    y = (slab[:, 2:3] * _load_row_tiles(gbuf.at[slot, 0])
         + slab[:, 3:4] * _load_row_tiles(gbuf.at[slot, 1]))
    gate2 = mod_ref[0, 5:6, :]
    x2 = x1_ref[...] + gate2 * y
    out_ref[...] = _rms(x2, gf_ref[...])


def _combine_call(dest_flat, yb, slab, x1_tok, mod3, gf, seq):
    n_tok, d = x1_tok.shape
    t = min(COMBINE_T, seq)
    return pl.pallas_call(
        _combine_kernel,
        grid_spec=pltpu.PrefetchScalarGridSpec(
            num_scalar_prefetch=1, grid=(n_tok // t,),
            in_specs=[pl.BlockSpec(memory_space=pl.ANY),
                      pl.BlockSpec((t, LANES), lambda i, dr: (i, 0)),
                      pl.BlockSpec((t, d), lambda i, dr: (i, 0)),
                      pl.BlockSpec((1,) + mod3.shape[1:], lambda i, dr: ((i * t) // seq, 0, 0)),
                      pl.BlockSpec((1, d), lambda i, dr: (0, 0))],
            out_specs=pl.BlockSpec((t, d), lambda i, dr: (i, 0)),
            scratch_shapes=[pltpu.VMEM((2, 2, t * ROW_SUB, LANES), F32), pltpu.SemaphoreType.DMA((2,))]),
        out_shape=jax.ShapeDtypeStruct((n_tok, d), F32),
        compiler_params=pltpu.CompilerParams(dimension_semantics=("arbitrary",),
                                             vmem_limit_bytes=VMEM_LIMIT),
        name="moe_combine",
    )(dest_flat, yb, slab, x1_tok, mod3, gf.reshape(1, d))


def kernel(x, c, ada_w, ada_b, norm1_g, w_in, gmlp_ln_w, gmlp_ln_b, gmlp_ws, gmlp_bs, rwkv_mu, rwkv_w0,
           rwkv_w2, rwkv_a0, rwkv_a2, rwkv_g2, rwkv_k_k, rwkv_k_a, rwkv_r_k, rwkv_gn_w, rwkv_gn_b, w_out,
           norm2_g, router_group_w, router_group_b, router_expert_w, router_expert_b, moe_w_gate,
           moe_w_up, moe_w_down, final_norm_g):
    bsz, seq, d = x.shape
    n_tok = bsz * seq
    assert ada_w.shape[0] == 1, "the final rmsnorm is fused into the (single) layer's combine kernel"
    for l in range(1):
        p = dict(norm1_g=norm1_g[l], w_in=w_in[l], gmlp_ln_w=gmlp_ln_w[l], gmlp_ln_b=gmlp_ln_b[l],
                 gmlp_ws=gmlp_ws[l], gmlp_bs=gmlp_bs[l], rwkv_mu=rwkv_mu[l], rwkv_w0=rwkv_w0[l],
                 rwkv_w2=rwkv_w2[l], rwkv_a0=rwkv_a0[l], rwkv_a2=rwkv_a2[l], rwkv_g2=rwkv_g2[l],
                 rwkv_k_k=rwkv_k_k[l], rwkv_k_a=rwkv_k_a[l], rwkv_r_k=rwkv_r_k[l],
                 rwkv_gn_w=rwkv_gn_w[l], rwkv_gn_b=rwkv_gn_b[l], w_out=w_out[l], norm2_g=norm2_g[l],
                 router_group_w=router_group_w[l], router_group_b=router_group_b[l],
                 router_expert_w=router_expert_w[l], router_expert_b=router_expert_b[l])
        mod3 = _mod_call(c, ada_w[l], ada_b[l]).reshape(bsz, 6, d)
        pre = _pre_call(x, mod3, p)
        x1, h2, route, slab, counts = _scan_call(x, mod3, pre, p)
        dest, meta = _route_call(route, counts)
        dest_flat = dest.reshape(-1)
        n_slots = _n_blocks(n_tok) * MOE_BM
        meta_flat = meta.reshape(-1)
        xs = _dispatch_call(dest_flat, meta_flat, h2.reshape(n_tok * ROW_SUB, LANES), n_slots)
        yb = _expert_call(meta_flat, xs, moe_w_gate[l], moe_w_up[l], moe_w_down[l])
        x = _combine_call(dest_flat, yb, slab, x1.reshape(n_tok, d), mod3, final_norm_g, seq).reshape(bsz, seq, d)
    return x
```

```python
import functools

import jax
import jax.numpy as jnp
from jax import lax
from jax.experimental import pallas as pl
from jax.experimental.pallas import tpu as pltpu

F32 = jnp.float32
BF16 = jnp.bfloat16
I32 = jnp.int32

HEAD_DIM = 64
GMLP_WIDTH = 512
RWKV_WIDTH = 512
GMLP_CHUNK = 128
DECAY_LORA = 32
ICLR_LORA = 32
GATE_LORA = 96
LORA_PAD = 256
N_GROUPS = 4
EXPERTS_PER_GROUP = 8
N_EXPERTS = 32
RMS_EPS = 1e-6
LN_EPS = 1e-5
GN_EPS = 64e-5

LANES = 128
MXU_DIM = 256
VMEM_LIMIT = 56 * 1024 * 1024

PRE_TM = 1024
PRE_SUB = 256
SCAN_TM = 512
SCAN_L = 64
HEADS_PER_GROUP = MXU_DIM // HEAD_DIM
ROUTE_T = 512
MOE_BM = 512
EXPERT_SPLIT = 2
DISPATCH_T = 2048
COMBINE_T = 512
ROUTE_ROWS = 8
DMA_UNROLL = 8


def _dot(a, b):
    return jnp.dot(a, b, preferred_element_type=F32)


def _dot_nt(a, b):
    return lax.dot_general(a, b, (((1,), (1,)), ((), ())), preferred_element_type=F32)


def _rms(x, g):
    return x * lax.rsqrt(jnp.mean(x * x, axis=-1, keepdims=True) + RMS_EPS) * g


ROW_SUB = 8


def _store_row_tiles(ref, val):
    m = val.shape[0]
    for s in range(ROW_SUB):
        ref[pl.ds(s, m, stride=ROW_SUB), :] = val[:, s * LANES:(s + 1) * LANES]


def _load_row_tiles(ref):
    m = ref.shape[0] // ROW_SUB
    return jnp.concatenate([ref[pl.ds(s, m, stride=ROW_SUB), :] for s in range(ROW_SUB)], axis=1)


def _row_tile(ref, row):
    return ref.at[pl.ds(pl.multiple_of(row * ROW_SUB, ROW_SUB), ROW_SUB), :]


def _mod_kernel(c_ref, w_ref, b_ref, o_ref):
    c = c_ref[...]
    w = w_ref[...]
    c_hi = c.astype(BF16)
    c_lo = (c - c_hi.astype(F32)).astype(BF16)
    w_hi = w.astype(BF16)
    w_lo = (w - w_hi.astype(F32)).astype(BF16)
    o_ref[...] = _dot(c_hi, w_hi) + _dot(c_lo, w_hi) + _dot(c_hi, w_lo) + b_ref[...]


def _mod_call(c, ada_w, ada_b):
    bsz, d = c.shape
    n = ada_w.shape[1]
    tn = 1024
    return pl.pallas_call(
        _mod_kernel,
        grid=(n // tn,),
        in_specs=[pl.BlockSpec((bsz, d), lambda j: (0, 0)),
                  pl.BlockSpec((d, tn), lambda j: (0, j)),
                  pl.BlockSpec((1, tn), lambda j: (0, j))],
        out_specs=pl.BlockSpec((bsz, tn), lambda j: (0, j)),
        out_shape=jax.ShapeDtypeStruct((bsz, n), F32),
        compiler_params=pltpu.CompilerParams(dimension_semantics=("arbitrary",),
                                             vmem_limit_bytes=VMEM_LIMIT),
        name="adaln_mod",
    )(c, ada_w, ada_b.reshape(1, n))


def _token_shift(z, carry_ref):
    tm = z.shape[0]
    prev = pltpu.roll(z, 1, axis=0)
    row = lax.broadcasted_iota(I32, z.shape, 0)
    prev = jnp.where(row == 0, carry_ref[0:1, :], prev)
    carry_ref[0:1, :] = z[tm - 1:tm, :]
    return prev


def _pre_kernel(x_ref, mod_ref, g1_ref, wuv_ref, wrkv_ref, wl_ref, lnw_ref, lnb_ref, wsp_ref,
                bsx_ref, mu_rkv_ref, mu_l_ref, lora_w_ref, w0_ref, a0_ref, kk_ref, ka_ref, ones_ref,
                ya_ref, r_ref, k2_ref, v_ref, kkn_ref, al_ref, g_ref, lw_ref,
                carry_rkv, carry_l):
    tm = x_ref.shape[1]

    @pl.when(pl.program_id(1) == 0)
    def _():
        carry_rkv[...] = jnp.zeros_like(carry_rkv)
        carry_l[...] = jnp.zeros_like(carry_l)

    shift1 = mod_ref[0, 0:1, :]
    scale1 = mod_ref[0, 1:2, :]
    sub = min(PRE_SUB, tm)
    zs = []
    for i in range(tm // sub):
        x = x_ref[0, i * sub:(i + 1) * sub, :]
        hb = (_rms(x, g1_ref[...]) * (1.0 + scale1) + shift1).astype(BF16)
        zs.append((_dot(hb, wuv_ref[...]),
                   _dot(hb, wrkv_ref[...]),
                   _dot(hb, wl_ref[...])))

    lane = lax.broadcasted_iota(I32, (GMLP_CHUNK, LANES), 1)
    left = lane < HEAD_DIM
    trow = lax.broadcasted_iota(I32, (GMLP_CHUNK, 2 * GMLP_CHUNK), 0)
    tcol = lax.broadcasted_iota(I32, (GMLP_CHUNK, 2 * GMLP_CHUNK), 1) % GMLP_CHUNK
    causal = tcol <= trow
    zero = jnp.zeros((), BF16)
    wpairs = [jnp.where(causal, wsp_ref[p], 0.0).astype(BF16) for p in range(GMLP_WIDTH // LANES)]

    for i, (zuv, zr, zl) in enumerate(zs):
        base = i * sub
        u = jax.nn.gelu(zuv[:, :GMLP_WIDTH])
        v = jax.nn.gelu(zuv[:, GMLP_WIDTH:])
        m = jnp.mean(v, axis=-1, keepdims=True)
        var = jnp.mean(jnp.square(v - m), axis=-1, keepdims=True)
        vn = ((v - m) * lax.rsqrt(var + LN_EPS) * lnw_ref[...] + lnb_ref[...]).astype(BF16)
        for p, wpair in enumerate(wpairs):
            for c in range(sub // GMLP_CHUNK):
                rows = slice(c * GMLP_CHUNK, (c + 1) * GMLP_CHUNK)
                cols = slice(p * LANES, (p + 1) * LANES)
                vc = vn[rows, cols]
                rhs = jnp.concatenate([jnp.where(left, vc, zero), jnp.where(left, zero, vc)], axis=0)
                mixed = _dot(wpair, rhs) + bsx_ref[:, cols]
                ya_ref[0, base + c * GMLP_CHUNK:base + (c + 1) * GMLP_CHUNK, cols] = (
                    u[rows, cols] * mixed).astype(ya_ref.dtype)

        out = slice(base, base + sub)
        zr = zr + (_token_shift(zr, carry_rkv) - zr) * mu_rkv_ref[...]
        zl = zl + (_token_shift(zl, carry_l) - zl) * mu_l_ref[...]
        r = zr[:, :RWKV_WIDTH]
        k = zr[:, RWKV_WIDTH:2 * RWKV_WIDTH]
        vv = zr[:, 2 * RWKV_WIDTH:]
        llane = lax.broadcasted_iota(I32, zl.shape, 1)
        f = jnp.where(llane < DECAY_LORA, jnp.tanh(zl),
                      jnp.where(llane < DECAY_LORA + ICLR_LORA, zl, jax.nn.sigmoid(zl)))
        lo = _dot(f.astype(BF16), lora_w_ref[...])
        wpre = w0_ref[...] + lo[:, :RWKV_WIDTH]
        w_log = -(jnp.maximum(-wpre, 0.0) + jnp.log(1.0 + jnp.exp(-jnp.abs(wpre)))) - 0.5
        lw_ref[0, out, :] = -jnp.exp(w_log)
        alpha = jax.nn.sigmoid(a0_ref[...] + lo[:, RWKV_WIDTH:2 * RWKV_WIDTH])
        g_ref[0, out, :] = lo[:, 2 * RWKV_WIDTH:].astype(g_ref.dtype)
        kkr = k * kk_ref[...]
        ss = _dot((kkr * kkr).astype(BF16), ones_ref[...])
        kkn_ref[0, out, :] = (kkr * lax.rsqrt(jnp.maximum(ss, 1e-24))).astype(kkn_ref.dtype)
        k2_ref[0, out, :] = (k * (1.0 + (alpha - 1.0) * ka_ref[...])).astype(k2_ref.dtype)
        r_ref[0, out, :] = r.astype(r_ref.dtype)
        v_ref[0, out, :] = vv.astype(v_ref.dtype)
        al_ref[0, out, :] = alpha.astype(al_ref.dtype)


def _head_ones(width):
    hid = jnp.arange(width) // HEAD_DIM
    return (hid[:, None] == hid[None, :]).astype(BF16)


def _pre_call(x, mod3, p):
    bsz, seq, d = x.shape
    tm = min(PRE_TM, seq)
    w_in = p["w_in"]
    wuv = w_in[:, :2 * GMLP_WIDTH].astype(BF16)
    wrkv = w_in[:, 2 * GMLP_WIDTH:2 * GMLP_WIDTH + 3 * RWKV_WIDTH].astype(BF16)
    n_lora = DECAY_LORA + ICLR_LORA + GATE_LORA
    wl = jnp.pad(w_in[:, 2 * GMLP_WIDTH + 3 * RWKV_WIDTH:], ((0, 0), (0, LORA_PAD - n_lora))).astype(BF16)
    mu = p["rwkv_mu"]
    mu_rkv = mu[:3 * RWKV_WIDTH].reshape(1, -1)
    mu_l = jnp.pad(mu[3 * RWKV_WIDTH:], (0, LORA_PAD - n_lora)).reshape(1, -1)
    zpad = lambda a, lo_, hi_: jnp.pad(a, ((lo_, LORA_PAD - hi_), (0, 0)))
    lora_w = jnp.concatenate([
        zpad(p["rwkv_w2"], 0, DECAY_LORA),
        zpad(p["rwkv_a2"], DECAY_LORA, DECAY_LORA + ICLR_LORA),
        zpad(p["rwkv_g2"], DECAY_LORA + ICLR_LORA, n_lora)], axis=1).astype(BF16)
    ws = p["gmlp_ws"]
    wsp = jnp.concatenate([ws[0::2], ws[1::2]], axis=2)
    bsx = jnp.repeat(p["gmlp_bs"].T, HEAD_DIM, axis=1)
    row = lambda a: a.reshape(1, -1)
    full = lambda a: pl.BlockSpec(a.shape, lambda b, s: (0,) * a.ndim)
    args = [mod3, row(p["norm1_g"]), wuv, wrkv, wl, row(p["gmlp_ln_w"]), row(p["gmlp_ln_b"]), wsp, bsx,
            mu_rkv, mu_l, lora_w, row(p["rwkv_w0"]), row(p["rwkv_a0"]), row(p["rwkv_k_k"]),
            row(p["rwkv_k_a"]), _head_ones(RWKV_WIDTH)]
    in_specs = [pl.BlockSpec((1, tm, d), lambda b, s: (b, s, 0)),
                pl.BlockSpec((1,) + mod3.shape[1:], lambda b, s: (b, 0, 0))]
    in_specs += [full(a) for a in args[1:]]
    tok = lambda dt: jax.ShapeDtypeStruct((bsz, seq, RWKV_WIDTH), dt)
    out_spec = pl.BlockSpec((1, tm, RWKV_WIDTH), lambda b, s: (b, s, 0))
    outs = [tok(BF16)] * 7 + [tok(F32)]
    return pl.pallas_call(
        _pre_kernel,
        grid=(bsz, seq // tm),
        in_specs=in_specs,
        out_specs=[out_spec] * 8,
        out_shape=outs,
        scratch_shapes=[pltpu.VMEM((8, 3 * RWKV_WIDTH), F32), pltpu.VMEM((8, LORA_PAD), F32)],
        compiler_params=pltpu.CompilerParams(dimension_semantics=("arbitrary", "arbitrary"),
                                             vmem_limit_bytes=VMEM_LIMIT),
        name="pre_mix",
    )(x, *args)


def _blockdiag(xb, mask):
    t = jnp.concatenate([xb] * HEADS_PER_GROUP, axis=0)
    return jnp.where(mask, t, jnp.zeros((), xb.dtype))


def _head_transpose(x):
    xt = x.T
    return jnp.concatenate([xt[hh * HEAD_DIM:(hh + 1) * HEAD_DIM, :] for hh in range(HEADS_PER_GROUP)], axis=1)


def _scan_kernel(x_ref, mod_ref, ya_ref, r_ref, k2_ref, v_ref, kk_ref, al_ref, g_ref, lw_ref,
                 rk_ref, gnw_ref, gnb_ref, ones_ref, wout_ref, g2n_ref, wr_ref, br_ref,
                 x1_ref, h2_ref, route_ref, slab_ref, cnt_ref, state_ref):
    tm = x_ref.shape[1]
    L = SCAN_L
    n_chunks = tm // L
    n_groups = RWKV_WIDTH // MXU_DIM

    @pl.when(pl.program_id(1) == 0)
    def _():
        state_ref[...] = jnp.zeros_like(state_ref)

    @pl.when(jnp.logical_and(pl.program_id(0) == 0, pl.program_id(1) == 0))
    def _():
        cnt_ref[...] = jnp.zeros_like(cnt_ref)

    r = r_ref[0].astype(F32)
    k2 = k2_ref[0].astype(F32)
    v = v_ref[0].astype(F32)
    kk = kk_ref[0].astype(F32)
    b = kk * al_ref[0].astype(F32)
    lw = lw_ref[0]
    lw_hi = lw.astype(BF16)
    lw_lo = (lw - lw_hi.astype(F32)).astype(BF16)

    trow = lax.broadcasted_iota(I32, (L, L), 0)
    tcol = lax.broadcasted_iota(I32, (L, L), 1)
    tri = (tcol <= trow).astype(BF16)
    row = lax.broadcasted_iota(I32, (L, MXU_DIM), 0)
    lane = lax.broadcasted_iota(I32, (L, MXU_DIM), 1)
    lane_j = lane % HEAD_DIM
    strict = lane_j < row
    incl = lane_j <= row
    eye = lane_j == row
    brow = lax.broadcasted_iota(I32, (MXU_DIM, MXU_DIM), 0) // HEAD_DIM
    bcol = lax.broadcasted_iota(I32, (MXU_DIM, MXU_DIM), 1) // HEAD_DIM
    bmask = brow == bcol

    def bd(xf):
        return _blockdiag(xf.astype(BF16), bmask)

    probs = [(c, g) for c in range(n_chunks) for g in range(n_groups)]
    each = lambda fn, *lists: [fn(*a) for a in zip(*lists)]
    lgs = [_dot(tri, lw_hi[c * L:(c + 1) * L]) + _dot(tri, lw_lo[c * L:(c + 1) * L])
           for c in range(n_chunks)]

    def sl(arr, c, g):
        return arr[c * L:(c + 1) * L, g * MXU_DIM:(g + 1) * MXU_DIM]

    lgc = [lgs[c][:, g * MXU_DIM:(g + 1) * MXU_DIM] for c, g in probs]
    glast = [x[L - 1:L, :] for x in lgc]
    eg = [jnp.exp(x) for x in lgc]
    egx = [jnp.exp(x - sl(lw, c, g)) for x, (c, g) in zip(lgc, probs)]
    einv = [jnp.exp(-x) for x in lgc]
    ehat = each(lambda gl, x: jnp.exp(gl - x), glast, lgc)
    at = [-sl(kk, c, g) * e for e, (c, g) in zip(egx, probs)]
    rt = [sl(r, c, g) * e for e, (c, g) in zip(eg, probs)]
    bt = [sl(b, c, g) * e for e, (c, g) in zip(einv, probs)]
    kt = [sl(k2, c, g) * e for e, (c, g) in zip(einv, probs)]
    bh = [sl(b, c, g) * e for e, (c, g) in zip(ehat, probs)]
    kh = [sl(k2, c, g) * e for e, (c, g) in zip(ehat, probs)]
    vc = [sl(v, c, g) for c, g in probs]
    lhs = each(lambda a_, r_: jnp.concatenate([a_, r_], axis=0).astype(BF16), at, rt)
    pb = each(lambda l_, x: _dot_nt(l_, bd(x)), lhs, bt)
    pk = each(lambda l_, x: _dot_nt(l_, bd(x)), lhs, kt)
    a_ab = [jnp.where(strict, x[:L], 0.0) for x in pb]
    a_rb = [jnp.where(incl, x[L:], 0.0) for x in pb]
    a_ak = [jnp.where(strict, x[:L], 0.0) for x in pk]
    a_rk = [jnp.where(incl, x[L:], 0.0) for x in pk]
    ssum = [jnp.where(eye, 1.0, 0.0) + x for x in a_ab]
    xb = [x.astype(BF16) for x in a_ab]
    xp = [_dot(x, _blockdiag(x, bmask)) for x in xb]
    m = 2
    while m < L // 2:
        xb = [x.astype(BF16) for x in xp]
        res = each(lambda x, s_: _dot(jnp.concatenate([x, s_.astype(BF16)], axis=0), _blockdiag(x, bmask)),
                   xb, ssum)
        xp = [x[:L] for x in res]
        ssum = each(lambda s_, x: s_ + x[L:], ssum, res)
        m *= 2
    tmat = each(lambda s_, x: s_ + _dot(s_.astype(BF16), bd(x)), ssum, xp)
    bht = [_head_transpose(x).astype(BF16) for x in bh]
    kht = [_head_transpose(x) for x in kh]
    res = each(lambda a1, a2, k_, x: _dot(jnp.concatenate([a1, a2, k_], axis=0).astype(BF16), bd(x)),
               a_ak, a_rk, kht, vc)
    akv = [x[:L] for x in res]
    arkv = [x[L:2 * L] for x in res]
    khv = [x[2 * L:] for x in res]
    tb = [x.astype(BF16) for x in tmat]
    wm = each(lambda t_, x: _dot(t_, bd(x)), tb, at)
    u0 = each(lambda t_, x: _dot(t_, bd(x)), tb, akv)
    lhs_w = each(lambda a_, b_: jnp.concatenate([a_.astype(BF16), b_], axis=0), a_rb, bht)
    res_w = each(lambda l_, x: _dot(l_, bd(x)), lhs_w, wm)
    res_u = each(lambda l_, x: _dot(l_, bd(x)), lhs_w, u0)
    qp = each(lambda r_, x: r_ + x[:L], rt, res_w)
    mt = each(lambda x, gl: x[L:] + jnp.where(eye, jnp.exp(gl), 0.0), res_w, glast)
    y0 = each(lambda x, z: x[:L] + z, res_u, arkv)
    nt = each(lambda x, z: x[L:] + z, res_u, khv)
    lhs2 = each(lambda q_, m_: jnp.concatenate([q_, m_], axis=0).astype(BF16), qp, mt)

    st = [state_ref[g] for g in range(n_groups)]
    y_rows = []
    for c in range(n_chunks):
        y_heads = []
        for g in range(n_groups):
            i = c * n_groups + g
            res_s = _dot(lhs2[i], bd(st[g]))
            y_heads.append(res_s[:L] + y0[i])
            st[g] = res_s[L:] + nt[i]
        y_rows.append(jnp.concatenate(y_heads, axis=1))
    for g in range(n_groups):
        state_ref[g] = st[g]

    y = jnp.concatenate(y_rows, axis=0)
    ones = ones_ref[...]
    inv_n = 1.0 / HEAD_DIM
    mean = _dot(y.astype(BF16), ones) * inv_n
    dlt = y - mean
    var = _dot((dlt * dlt).astype(BF16), ones) * inv_n
    yn = dlt * lax.rsqrt(var + GN_EPS) * gnw_ref[...] + gnb_ref[...]
    bonus = _dot((r * k2 * rk_ref[...]).astype(BF16), ones) * v
    yb = (yn + bonus) * g_ref[0].astype(F32)
    cat = jnp.concatenate([ya_ref[0], yb.astype(BF16)], axis=1)
    y2 = _dot(cat, wout_ref[...])
    gate1 = mod_ref[0, 2:3, :]
    shift2 = mod_ref[0, 3:4, :]
    scale2 = mod_ref[0, 4:5, :]
    x1 = x_ref[0] + gate1 * y2
    x1_ref[0] = x1
    h2 = _rms(x1, g2n_ref[...]) * (1.0 + scale2) + shift2
    _store_row_tiles(h2_ref.at[0], h2)

    logits = _dot(h2.astype(BF16), wr_ref[...]) + br_ref[...]
    slab = _route_slab(logits)
    slab_ref[...] = slab
    slab_t = slab.T[:ROUTE_ROWS, :]
    route_ref[...] = slab_t
    erow = lax.broadcasted_iota(I32, (N_EXPERTS, tm), 0)
    hits = ((erow == slab_t[0:1, :].astype(I32)).astype(F32)
            + (erow == slab_t[1:2, :].astype(I32)).astype(F32))
    cnt_ref[...] += jnp.sum(hits, axis=1, keepdims=True)


def _scan_call(x, mod3, pre_outs, p):
    bsz, seq, d = x.shape
    tm = min(SCAN_TM, seq)
    n_tok = bsz * seq
    steps = seq // tm
    n_route = N_GROUPS + N_EXPERTS
    wr = jnp.pad(jnp.concatenate([p["router_group_w"], p["router_expert_w"]], axis=1),
                 ((0, 0), (0, LANES - n_route))).astype(BF16)
    br = jnp.pad(jnp.concatenate([p["router_group_b"], p["router_expert_b"]]), (0, LANES - n_route)).reshape(1, -1)
    row = lambda a: a.reshape(1, -1)
    full = lambda a: pl.BlockSpec(a.shape, lambda b, s: (0,) * a.ndim)
    params = [row(p["rwkv_r_k"]), row(p["rwkv_gn_w"]), row(p["rwkv_gn_b"]), _head_ones(RWKV_WIDTH),
              p["w_out"].astype(BF16), row(p["norm2_g"]), wr, br]
    tok_spec = pl.BlockSpec((1, tm, RWKV_WIDTH), lambda b, s: (b, s, 0))
    x_spec = pl.BlockSpec((1, tm, d), lambda b, s: (b, s, 0))
    in_specs = [x_spec, pl.BlockSpec((1,) + mod3.shape[1:], lambda b, s: (b, 0, 0))]
    in_specs += [tok_spec] * 8 + [full(a) for a in params]
    return pl.pallas_call(
        _scan_kernel,
        grid=(bsz, steps),
        in_specs=in_specs,
        out_specs=[x_spec, pl.BlockSpec((1, tm * ROW_SUB, LANES), lambda b, s: (b, s, 0)),
                   pl.BlockSpec((ROUTE_ROWS, tm), lambda b, s: (0, b * steps + s)),
                   pl.BlockSpec((tm, LANES), lambda b, s: (b * steps + s, 0)),
                   pl.BlockSpec((N_EXPERTS, LANES), lambda b, s: (0, 0))],
        out_shape=[jax.ShapeDtypeStruct((bsz, seq, d), F32),
                   jax.ShapeDtypeStruct((bsz, seq * ROW_SUB, LANES), F32),
                   jax.ShapeDtypeStruct((ROUTE_ROWS, n_tok), F32), jax.ShapeDtypeStruct((n_tok, LANES), F32),
                   jax.ShapeDtypeStruct((N_EXPERTS, LANES), F32)],
        scratch_shapes=[pltpu.VMEM((RWKV_WIDTH // MXU_DIM, HEAD_DIM, MXU_DIM), F32)],
        compiler_params=pltpu.CompilerParams(dimension_semantics=("arbitrary", "arbitrary"),
                                             vmem_limit_bytes=VMEM_LIMIT),
        name="scan_post",
    )(x, mod3, *pre_outs, *params)


def _route_slab(logits):
    ln = lax.broadcasted_iota(I32, logits.shape, 1)
    neg = jnp.float32(-jnp.inf)
    is_g = ln < N_GROUPS
    gl = jnp.where(is_g, logits, neg)
    gmax = jnp.max(gl, axis=-1, keepdims=True)
    gidx = jnp.min(jnp.where(gl == gmax, ln, LANES), axis=-1, keepdims=True)
    g_p = 1.0 / jnp.sum(jnp.where(is_g, jnp.exp(logits - gmax), 0.0), axis=-1, keepdims=True)
    egrp = (ln - N_GROUPS) // EXPERTS_PER_GROUP
    sel = jnp.where(ln >= N_GROUPS, egrp, -1) == gidx
    el = jnp.where(sel, logits, neg)
    l1 = jnp.max(el, axis=-1, keepdims=True)
    i1 = jnp.min(jnp.where(el == l1, ln, LANES), axis=-1, keepdims=True)
    el2 = jnp.where(ln == i1, neg, el)
    l2 = jnp.max(el2, axis=-1, keepdims=True)
    i2 = jnp.min(jnp.where(el2 == l2, ln, LANES), axis=-1, keepdims=True)
    t = jnp.exp(l2 - l1)
    p1 = 1.0 / (1.0 + t)
    w1 = g_p * p1
    w2 = g_p * (t * p1)
    e1 = (i1 - N_GROUPS).astype(F32)
    e2 = (i2 - N_GROUPS).astype(F32)
    return jnp.where(ln == 0, e1, jnp.where(ln == 1, e2, jnp.where(ln == 2, w1, jnp.where(ln == 3, w2, 0.0))))


def _onehots(route_ref):
    t = route_ref.shape[1]
    erow = lax.broadcasted_iota(I32, (N_EXPERTS, t), 0)
    oh1 = erow == route_ref[0:1, :].astype(I32)
    oh2 = erow == route_ref[1:2, :].astype(I32)
    return oh1, oh2


def _dest_kernel(route_ref, cnt_ref, dest_ref, meta_ref, base_ref, upper_ref):
    t = route_ref.shape[1]
    nbp = meta_ref.shape[1]

    @pl.when(pl.program_id(0) == 0)
    def _():
        sr = lax.broadcasted_iota(I32, (t, t), 0)
        sc = lax.broadcasted_iota(I32, (t, t), 1)
        upper_ref[...] = (sr < sc).astype(BF16)
        cnt = cnt_ref[...]
        padded = jnp.floor((cnt + (MOE_BM - 1)) * (1.0 / MOE_BM)) * MOE_BM
        er = lax.broadcasted_iota(I32, (N_EXPERTS, N_EXPERTS), 0)
        ec = lax.broadcasted_iota(I32, (N_EXPERTS, N_EXPERTS), 1)
        lower = (ec < er).astype(BF16)
        p_hi = padded.astype(BF16)
        p_lo = (padded - p_hi.astype(F32)).astype(BF16)
        pstart = _dot(lower, p_hi) + _dot(lower, p_lo)
        base_ref[...] = pstart
        pend = (pstart + padded)[:, 0:1]
        blk0 = (lax.broadcasted_iota(I32, (N_EXPERTS, nbp), 1) * MOE_BM).astype(F32)
        be = jnp.sum((pend <= blk0).astype(I32), axis=0, keepdims=True)
        meta_ref[0:1, :] = jnp.minimum(be, N_EXPERTS - 1)
        used = (pend[N_EXPERTS - 1:N_EXPERTS, :] * (1.0 / MOE_BM)).astype(I32)
        meta_ref[1:2, :] = jnp.broadcast_to(used, (1, nbp))
        on_diag = (lax.broadcasted_iota(I32, (N_EXPERTS, nbp), 0)
                   == lax.broadcasted_iota(I32, (N_EXPERTS, nbp), 1))
        pad_lo = (pstart + cnt)[:, 0:1]
        pad_len = (padded - cnt)[:, 0:1]
        meta_ref[2:3, :] = jnp.sum(jnp.where(on_diag, pad_lo, 0.0), axis=0, keepdims=True).astype(I32)
        meta_ref[3:4, :] = jnp.sum(jnp.where(on_diag, pad_len, 0.0), axis=0, keepdims=True).astype(I32)
        e_id = lax.broadcasted_iota(I32, (N_EXPERTS, nbp), 0)
        later = jnp.logical_and(e_id > jnp.minimum(be, N_EXPERTS - 1), padded[:, 0:1] > 0.0)
        nxt = jnp.min(jnp.where(later, e_id, N_EXPERTS), axis=0, keepdims=True)
        meta_ref[4:5, :] = jnp.where(nxt == N_EXPERTS, -1, nxt)
        meta_ref[5:, :] = jnp.zeros((meta_ref.shape[0] - 5, nbp), I32)

    oh1, oh2 = _onehots(route_ref)
    cnt = oh1.astype(F32) + oh2.astype(F32)
    pos = _dot(cnt.astype(BF16), upper_ref[...]) + base_ref[:, 0:1]
    dest_ref[0:1, :] = jnp.sum(jnp.where(oh1, pos, 0.0), axis=0, keepdims=True).astype(I32)
    dest_ref[1:2, :] = jnp.sum(jnp.where(oh2, pos, 0.0), axis=0, keepdims=True).astype(I32)
    base_ref[...] += jnp.sum(cnt, axis=1, keepdims=True)


def _n_blocks(n_tok):
    return -(-2 * n_tok // MOE_BM) + N_EXPERTS


def _route_call(route, counts):
    n_tok = route.shape[1]
    t = min(ROUTE_T, n_tok)
    nbp = -(-_n_blocks(n_tok) // LANES) * LANES
    cparams = pltpu.CompilerParams(dimension_semantics=("arbitrary",), vmem_limit_bytes=VMEM_LIMIT)
    route_spec = pl.BlockSpec((ROUTE_ROWS, t), lambda j: (0, j))
    cnt_spec = pl.BlockSpec((N_EXPERTS, LANES), lambda j: (0, 0))
    dest, meta = pl.pallas_call(
        _dest_kernel, grid=(n_tok // t,), in_specs=[route_spec, cnt_spec],
        out_specs=[pl.BlockSpec((2, t), lambda j: (0, j)), pl.BlockSpec((8, nbp), lambda j: (0, 0))],
        out_shape=[jax.ShapeDtypeStruct((2, n_tok), I32), jax.ShapeDtypeStruct((8, nbp), I32)],
        scratch_shapes=[pltpu.VMEM((N_EXPERTS, LANES), F32), pltpu.VMEM((t, t), BF16)],
        compiler_params=cparams, name="route_dest")(route, counts)
    return dest, meta


def _zero_fill_slots(meta_ref, xs_hbm, ztile, zsem, wait):
    nbp = meta_ref.shape[0] // 8
    n_blocks = xs_hbm.shape[0] // (MOE_BM * ROW_SUB)

    def go(rows, first_slot):
        cp = pltpu.make_async_copy(
            ztile.at[pl.ds(0, rows * ROW_SUB), :],
            xs_hbm.at[pl.ds(pl.multiple_of(first_slot * ROW_SUB, ROW_SUB), rows * ROW_SUB), :], zsem)
        if wait:
            cp.wait()
        else:
            cp.start()

    def expert_pad(e, carry):
        slot = meta_ref[2 * nbp + e]
        plen = meta_ref[3 * nbp + e]
        bit = MOE_BM // 2
        while bit >= 1:
            has = (plen & bit) != 0
            pl.when(has)(functools.partial(go, bit, slot))
            slot = slot + jnp.where(has, bit, 0)
            bit //= 2
        return carry

    def tail_block(j, carry):
        go(MOE_BM, j * MOE_BM)
        return carry

    lax.fori_loop(0, N_EXPERTS, expert_pad, 0)
    lax.fori_loop(meta_ref[nbp], n_blocks, tail_block, 0)


def _dispatch_kernel(dest_ref, meta_ref, h_ref, xs_hbm, sem, zsem, ztile):
    n_tok = dest_ref.shape[0] // 2
    t = h_ref.shape[0] // ROW_SUB
    base = pl.program_id(0) * t
    first = pl.program_id(0) == 0

    @pl.when(first)
    def _():
        ztile[...] = jnp.zeros_like(ztile)
        _zero_fill_slots(meta_ref, xs_hbm, ztile, zsem, wait=False)

    def start(i, carry):
        tok = base + i
        src = _row_tile(h_ref, i)
        pltpu.make_async_copy(src, _row_tile(xs_hbm, dest_ref[tok]), sem).start(priority=0)
        pltpu.make_async_copy(src, _row_tile(xs_hbm, dest_ref[n_tok + tok]), sem).start(priority=1)
        return carry

    lax.fori_loop(0, t, start, 0, unroll=DMA_UNROLL)
    for _ in range(2):
        pltpu.make_async_copy(h_ref, xs_hbm.at[pl.ds(0, t * ROW_SUB), :], sem).wait()

    @pl.when(first)
    def _():
        _zero_fill_slots(meta_ref, xs_hbm, ztile, zsem, wait=True)


def _dispatch_call(dest_flat, meta_flat, h_tok, n_slots):
    n_tok = h_tok.shape[0] // ROW_SUB
    t = min(DISPATCH_T, n_tok)
    return pl.pallas_call(
        _dispatch_kernel,
        grid_spec=pltpu.PrefetchScalarGridSpec(
            num_scalar_prefetch=2, grid=(n_tok // t,),
            in_specs=[pl.BlockSpec((t * ROW_SUB, LANES), lambda i, dr, mr: (i, 0))],
            out_specs=pl.BlockSpec(memory_space=pl.ANY),
            scratch_shapes=[pltpu.SemaphoreType.DMA, pltpu.SemaphoreType.DMA,
                            pltpu.VMEM((MOE_BM * ROW_SUB, LANES), h_tok.dtype)]),
        out_shape=jax.ShapeDtypeStruct((n_slots * ROW_SUB, LANES), h_tok.dtype),
        compiler_params=pltpu.CompilerParams(dimension_semantics=("arbitrary",),
                                             vmem_limit_bytes=VMEM_LIMIT),
        name="moe_dispatch",
    )(dest_flat, meta_flat, h_tok)


def _expert_kernel(meta_ref, xs_ref, wg_hbm, wu_hbm, wd_hbm, yb_ref, wgf, wuf, wdf, wgb, wub, wdb,
                   wsem, slot_ref):
    j = pl.program_id(0)
    nbp = meta_ref.shape[0] // 8
    used = j < meta_ref[nbp]
    new_expert = jnp.logical_or(j == 0, meta_ref[j] != meta_ref[jnp.maximum(j - 1, 0)])

    def fetch(e, slot):
        return [pltpu.make_async_copy(src.at[e], dst.at[slot], wsem.at[slot])
                for src, dst in ((wg_hbm, wgf), (wu_hbm, wuf), (wd_hbm, wdf))]

    @pl.when(j == 0)
    def _():
        slot_ref[0] = 0
        for cp in fetch(meta_ref[0], 0):
            cp.start()

    @pl.when(jnp.logical_and(used, new_expert))
    def _():
        slot = slot_ref[0]
        for cp in fetch(meta_ref[j], slot):
            cp.wait()
        nxt = meta_ref[4 * nbp + j]

        @pl.when(nxt >= 0)
        def _():
            for cp in fetch(nxt, 1 - slot):
                cp.start()

        wgb[...] = wgf[slot].astype(BF16)
        wub[...] = wuf[slot].astype(BF16)
        wdb[...] = wdf[slot].astype(BF16)
        slot_ref[0] = 1 - slot

    @pl.when(used)
    def _():
        sub = MOE_BM // EXPERT_SPLIT
        gu = []
        for i in range(EXPERT_SPLIT):
            x = _load_row_tiles(xs_ref.at[pl.ds(i * sub * ROW_SUB, sub * ROW_SUB), :]).astype(BF16)
            gu.append((_dot(x, wgb[...]), _dot(x, wub[...])))
        for i, (gate, up) in enumerate(gu):
            hid = (jax.nn.silu(gate) * up).astype(BF16)
            _store_row_tiles(yb_ref.at[pl.ds(i * sub * ROW_SUB, sub * ROW_SUB), :], _dot(hid, wdb[...]))

    @pl.when(jnp.logical_not(used))
    def _():
        yb_ref[...] = jnp.zeros_like(yb_ref)


def _expert_call(meta_flat, xs, wg, wu, wd):
    n_slots = xs.shape[0] // ROW_SUB
    d, hid = wg.shape[1:]
    blk = pl.BlockSpec((MOE_BM * ROW_SUB, LANES), lambda j, m: (j, 0))
    return pl.pallas_call(
        _expert_kernel,
        grid_spec=pltpu.PrefetchScalarGridSpec(
            num_scalar_prefetch=1, grid=(n_slots // MOE_BM,),
            in_specs=[blk] + [pl.BlockSpec(memory_space=pl.ANY)] * 3,
            out_specs=blk,
            scratch_shapes=[pltpu.VMEM((2, d, hid), F32), pltpu.VMEM((2, d, hid), F32),
                            pltpu.VMEM((2, hid, d), F32),
                            pltpu.VMEM((d, hid), BF16), pltpu.VMEM((d, hid), BF16),
                            pltpu.VMEM((hid, d), BF16),
                            pltpu.SemaphoreType.DMA((2,)), pltpu.SMEM((1,), I32)]),
        out_shape=jax.ShapeDtypeStruct((n_slots * ROW_SUB, LANES), F32),
        compiler_params=pltpu.CompilerParams(dimension_semantics=("arbitrary",),
                                             vmem_limit_bytes=VMEM_LIMIT),
        name="moe_experts",
    )(meta_flat, xs, wg, wu, wd)


def _combine_kernel(dest_ref, yb_hbm, slab_ref, x1_ref, mod_ref, gf_ref, out_ref, gbuf, sem):
    t = x1_ref.shape[0]
    n_tok = dest_ref.shape[0] // 2
    step = pl.program_id(0)
    slot = step % 2

    def issue(tile, buf):
        def body(r, carry):
            tok = tile * t + r
            pltpu.make_async_copy(_row_tile(yb_hbm, dest_ref[tok]), _row_tile(gbuf.at[buf, 0], r),
                                  sem.at[buf]).start(priority=0)
            pltpu.make_async_copy(_row_tile(yb_hbm, dest_ref[n_tok + tok]), _row_tile(gbuf.at[buf, 1], r),
                                  sem.at[buf]).start(priority=1)
            return carry
        lax.fori_loop(0, t, body, 0, unroll=DMA_UNROLL)

    @pl.when(step == 0)
    def _():
        issue(step, slot)

    @pl.when(step + 1 < pl.num_programs(0))
    def _():
        issue(step + 1, 1 - slot)

    for kk in range(2):
        pltpu.make_async_copy(yb_hbm.at[pl.ds(0, t * ROW_SUB), :], gbuf.at[slot, kk], sem.at[slot]).wait()
    slab = slab_ref[...]
    y = (slab[:, 2:3] * _load_row_tiles(gbuf.at[slot, 0])
         + slab[:, 3:4] * _load_row_tiles(gbuf.at[slot, 1]))
    gate2 = mod_ref[0, 5:6, :]
    x2 = x1_ref[...] + gate2 * y
    out_ref[...] = _rms(x2, gf_ref[...])


def _combine_call(dest_flat, yb, slab, x1_tok, mod3, gf, seq):
    n_tok, d = x1_tok.shape
    t = min(COMBINE_T, seq)
    return pl.pallas_call(
        _combine_kernel,
        grid_spec=pltpu.PrefetchScalarGridSpec(
            num_scalar_prefetch=1, grid=(n_tok // t,),
            in_specs=[pl.BlockSpec(memory_space=pl.ANY),
                      pl.BlockSpec((t, LANES), lambda i, dr: (i, 0)),
                      pl.BlockSpec((t, d), lambda i, dr: (i, 0)),
                      pl.BlockSpec((1,) + mod3.shape[1:], lambda i, dr: ((i * t) // seq, 0, 0)),
                      pl.BlockSpec((1, d), lambda i, dr: (0, 0))],
            out_specs=pl.BlockSpec((t, d), lambda i, dr: (i, 0)),
            scratch_shapes=[pltpu.VMEM((2, 2, t * ROW_SUB, LANES), F32), pltpu.SemaphoreType.DMA((2,))]),
        out_shape=jax.ShapeDtypeStruct((n_tok, d), F32),
        compiler_params=pltpu.CompilerParams(dimension_semantics=("arbitrary",),
                                             vmem_limit_bytes=VMEM_LIMIT),
        name="moe_combine",
    )(dest_flat, yb, slab, x1_tok, mod3, gf.reshape(1, d))


def kernel(x, c, ada_w, ada_b, norm1_g, w_in, gmlp_ln_w, gmlp_ln_b, gmlp_ws, gmlp_bs, rwkv_mu, rwkv_w0,
           rwkv_w2, rwkv_a0, rwkv_a2, rwkv_g2, rwkv_k_k, rwkv_k_a, rwkv_r_k, rwkv_gn_w, rwkv_gn_b, w_out,
           norm2_g, router_group_w, router_group_b, router_expert_w, router_expert_b, moe_w_gate,
           moe_w_up, moe_w_down, final_norm_g):
    bsz, seq, d = x.shape
    n_tok = bsz * seq
    assert ada_w.shape[0] == 1, "the final rmsnorm is fused into the (single) layer's combine kernel"
    layer = lambda a: a.reshape(a.shape[1:])
    p = dict(norm1_g=layer(norm1_g), w_in=layer(w_in), gmlp_ln_w=layer(gmlp_ln_w), gmlp_ln_b=layer(gmlp_ln_b),
             gmlp_ws=layer(gmlp_ws), gmlp_bs=layer(gmlp_bs), rwkv_mu=layer(rwkv_mu), rwkv_w0=layer(rwkv_w0),
             rwkv_w2=layer(rwkv_w2), rwkv_a0=layer(rwkv_a0), rwkv_a2=layer(rwkv_a2), rwkv_g2=layer(rwkv_g2),
             rwkv_k_k=layer(rwkv_k_k), rwkv_k_a=layer(rwkv_k_a), rwkv_r_k=layer(rwkv_r_k),
             rwkv_gn_w=layer(rwkv_gn_w), rwkv_gn_b=layer(rwkv_gn_b), w_out=layer(w_out), norm2_g=layer(norm2_g),
             router_group_w=layer(router_group_w), router_group_b=layer(router_group_b),
             router_expert_w=layer(router_expert_w), router_expert_b=layer(router_expert_b))
    mod3 = _mod_call(c, layer(ada_w), layer(ada_b)).reshape(bsz, 6, d)
    pre = _pre_call(x, mod3, p)
    x1, h2, route, slab, counts = _scan_call(x, mod3, pre, p)
    dest, meta = _route_call(route, counts)
    dest_flat = dest.reshape(-1)
    n_slots = _n_blocks(n_tok) * MOE_BM
    meta_flat = meta.reshape(-1)
    xs = _dispatch_call(dest_flat, meta_flat, h2.reshape(n_tok * ROW_SUB, LANES), n_slots)
    yb = _expert_call(meta_flat, xs, layer(moe_w_gate), layer(moe_w_up), layer(moe_w_down))
    out = _combine_call(dest_flat, yb, slab, x1.reshape(n_tok, d), mod3, final_norm_g, seq)
    return out.reshape(bsz, seq, d)
```

```python
import functools

import jax
import jax.numpy as jnp
from jax import lax
from jax.experimental import pallas as pl
from jax.experimental.pallas import tpu as pltpu

F32 = jnp.float32
BF16 = jnp.bfloat16
I32 = jnp.int32

HEAD_DIM = 64
GMLP_WIDTH = 512
RWKV_WIDTH = 512
GMLP_CHUNK = 128
DECAY_LORA = 32
ICLR_LORA = 32
GATE_LORA = 96
LORA_PAD = 256
N_GROUPS = 4
EXPERTS_PER_GROUP = 8
N_EXPERTS = 32
RMS_EPS = 1e-6
LN_EPS = 1e-5
GN_EPS = 64e-5

LANES = 128
MXU_DIM = 256
VMEM_LIMIT = 56 * 1024 * 1024

PRE_TM = 1024
PRE_SUB = 256
SCAN_TM = 512
SCAN_L = 64
HEADS_PER_GROUP = MXU_DIM // HEAD_DIM
ROUTE_T = 512
MOE_BM = 512
EXPERT_SPLIT = 2
DISPATCH_T = 2048
COMBINE_T = 512
ROUTE_ROWS = 8
DMA_UNROLL = 8


def _dot(a, b):
    return jnp.dot(a, b, preferred_element_type=F32)


def _dot_nt(a, b):
    return lax.dot_general(a, b, (((1,), (1,)), ((), ())), preferred_element_type=F32)


def _rms(x, g):
    return x * lax.rsqrt(jnp.mean(x * x, axis=-1, keepdims=True) + RMS_EPS) * g


ROW_SUB = 8


def _store_row_tiles(ref, val):
    m = val.shape[0]
    for s in range(ROW_SUB):
        ref[pl.ds(s, m, stride=ROW_SUB), :] = val[:, s * LANES:(s + 1) * LANES]


def _load_row_tiles(ref):
    m = ref.shape[0] // ROW_SUB
    return jnp.concatenate([ref[pl.ds(s, m, stride=ROW_SUB), :] for s in range(ROW_SUB)], axis=1)


def _row_tile(ref, row):
    return ref.at[pl.ds(pl.multiple_of(row * ROW_SUB, ROW_SUB), ROW_SUB), :]


def _mod_kernel(c_ref, w_ref, b_ref, o_ref):
    c = c_ref[...]
    w = w_ref[...]
    c_hi = c.astype(BF16)
    c_lo = (c - c_hi.astype(F32)).astype(BF16)
    w_hi = w.astype(BF16)
    w_lo = (w - w_hi.astype(F32)).astype(BF16)
    o_ref[...] = _dot(c_hi, w_hi) + _dot(c_lo, w_hi) + _dot(c_hi, w_lo) + b_ref[...]


def _mod_call(c, ada_w, ada_b):
    bsz, d = c.shape
    n = ada_w.shape[1]
    tn = 1024
    return pl.pallas_call(
        _mod_kernel,
        grid=(n // tn,),
        in_specs=[pl.BlockSpec((bsz, d), lambda j: (0, 0)),
                  pl.BlockSpec((d, tn), lambda j: (0, j)),
                  pl.BlockSpec((1, tn), lambda j: (0, j))],
        out_specs=pl.BlockSpec((bsz, tn), lambda j: (0, j)),
        out_shape=jax.ShapeDtypeStruct((bsz, n), F32),
        compiler_params=pltpu.CompilerParams(dimension_semantics=("arbitrary",),
                                             vmem_limit_bytes=VMEM_LIMIT),
        name="adaln_mod",
    )(c, ada_w, ada_b.reshape(1, n))


def _token_shift(z, carry_ref):
    tm = z.shape[0]
    prev = pltpu.roll(z, 1, axis=0)
    row = lax.broadcasted_iota(I32, z.shape, 0)
    prev = jnp.where(row == 0, carry_ref[0:1, :], prev)
    carry_ref[0:1, :] = z[tm - 1:tm, :]
    return prev


def _pre_kernel(x_ref, mod_ref, g1_ref, wuv_ref, wrkv_ref, wl_ref, lnw_ref, lnb_ref, wsp_ref,
                bsx_ref, mu_rkv_ref, mu_l_ref, lora_w_ref, w0_ref, a0_ref, kk_ref, ka_ref, ones_ref,
                ya_ref, r_ref, k2_ref, v_ref, kkn_ref, al_ref, g_ref, lw_ref,
                carry_rkv, carry_l):
    tm = x_ref.shape[1]

    @pl.when(pl.program_id(1) == 0)
    def _():
        carry_rkv[...] = jnp.zeros_like(carry_rkv)
        carry_l[...] = jnp.zeros_like(carry_l)

    shift1 = mod_ref[0, 0:1, :]
    scale1 = mod_ref[0, 1:2, :]
    sub = min(PRE_SUB, tm)
    zs = []
    for i in range(tm // sub):
        x = x_ref[0, i * sub:(i + 1) * sub, :]
        hb = (_rms(x, g1_ref[...]) * (1.0 + scale1) + shift1).astype(BF16)
        zs.append((_dot(hb, wuv_ref[...]),
                   _dot(hb, wrkv_ref[...]),
                   _dot(hb, wl_ref[...])))

    lane = lax.broadcasted_iota(I32, (GMLP_CHUNK, LANES), 1)
    left = lane < HEAD_DIM
    trow = lax.broadcasted_iota(I32, (GMLP_CHUNK, 2 * GMLP_CHUNK), 0)
    tcol = lax.broadcasted_iota(I32, (GMLP_CHUNK, 2 * GMLP_CHUNK), 1) % GMLP_CHUNK
    causal = tcol <= trow
    zero = jnp.zeros((), BF16)
    wpairs = [jnp.where(causal, wsp_ref[p], 0.0).astype(BF16) for p in range(GMLP_WIDTH // LANES)]

    for i, (zuv, zr, zl) in enumerate(zs):
        base = i * sub
        u = jax.nn.gelu(zuv[:, :GMLP_WIDTH])
        v = jax.nn.gelu(zuv[:, GMLP_WIDTH:])
        m = jnp.mean(v, axis=-1, keepdims=True)
        var = jnp.mean(jnp.square(v - m), axis=-1, keepdims=True)
        vn = ((v - m) * lax.rsqrt(var + LN_EPS) * lnw_ref[...] + lnb_ref[...]).astype(BF16)
        for p, wpair in enumerate(wpairs):
            for c in range(sub // GMLP_CHUNK):
                rows = slice(c * GMLP_CHUNK, (c + 1) * GMLP_CHUNK)
                cols = slice(p * LANES, (p + 1) * LANES)
                vc = vn[rows, cols]
                rhs = jnp.concatenate([jnp.where(left, vc, zero), jnp.where(left, zero, vc)], axis=0)
                mixed = _dot(wpair, rhs) + bsx_ref[:, cols]
                ya_ref[0, base + c * GMLP_CHUNK:base + (c + 1) * GMLP_CHUNK, cols] = (
                    u[rows, cols] * mixed).astype(ya_ref.dtype)

        out = slice(base, base + sub)
        zr = zr + (_token_shift(zr, carry_rkv) - zr) * mu_rkv_ref[...]
        zl = zl + (_token_shift(zl, carry_l) - zl) * mu_l_ref[...]
        r = zr[:, :RWKV_WIDTH]
        k = zr[:, RWKV_WIDTH:2 * RWKV_WIDTH]
        vv = zr[:, 2 * RWKV_WIDTH:]
        llane = lax.broadcasted_iota(I32, zl.shape, 1)
        f = jnp.where(llane < DECAY_LORA, jnp.tanh(zl),
                      jnp.where(llane < DECAY_LORA + ICLR_LORA, zl, jax.nn.sigmoid(zl)))
        lo = _dot(f.astype(BF16), lora_w_ref[...])
        wpre = w0_ref[...] + lo[:, :RWKV_WIDTH]
        w_log = -(jnp.maximum(-wpre, 0.0) + jnp.log(1.0 + jnp.exp(-jnp.abs(wpre)))) - 0.5
        lw_ref[0, out, :] = -jnp.exp(w_log)
        alpha = jax.nn.sigmoid(a0_ref[...] + lo[:, RWKV_WIDTH:2 * RWKV_WIDTH])
        g_ref[0, out, :] = lo[:, 2 * RWKV_WIDTH:].astype(g_ref.dtype)
        kkr = k * kk_ref[...]
        ss = _dot((kkr * kkr).astype(BF16), ones_ref[...])
        kkn_ref[0, out, :] = (kkr * lax.rsqrt(jnp.maximum(ss, 1e-24))).astype(kkn_ref.dtype)
        k2_ref[0, out, :] = (k * (1.0 + (alpha - 1.0) * ka_ref[...])).astype(k2_ref.dtype)
        r_ref[0, out, :] = r.astype(r_ref.dtype)
        v_ref[0, out, :] = vv.astype(v_ref.dtype)
        al_ref[0, out, :] = alpha.astype(al_ref.dtype)


def _head_ones(width):
    hid = jnp.arange(width) // HEAD_DIM
    return (hid[:, None] == hid[None, :]).astype(BF16)


def _pre_call(x, mod3, p):
    bsz, seq, d = x.shape
    tm = min(PRE_TM, seq)
    w_in = p["w_in"]
    wuv = w_in[:, :2 * GMLP_WIDTH].astype(BF16)
    wrkv = w_in[:, 2 * GMLP_WIDTH:2 * GMLP_WIDTH + 3 * RWKV_WIDTH].astype(BF16)
    n_lora = DECAY_LORA + ICLR_LORA + GATE_LORA
    wl = jnp.pad(w_in[:, 2 * GMLP_WIDTH + 3 * RWKV_WIDTH:], ((0, 0), (0, LORA_PAD - n_lora))).astype(BF16)
    mu = p["rwkv_mu"]
    mu_rkv = mu[:3 * RWKV_WIDTH].reshape(1, -1)
    mu_l = jnp.pad(mu[3 * RWKV_WIDTH:], (0, LORA_PAD - n_lora)).reshape(1, -1)
    zpad = lambda a, lo_, hi_: jnp.pad(a, ((lo_, LORA_PAD - hi_), (0, 0)))
    lora_w = jnp.concatenate([
        zpad(p["rwkv_w2"], 0, DECAY_LORA),
        zpad(p["rwkv_a2"], DECAY_LORA, DECAY_LORA + ICLR_LORA),
        zpad(p["rwkv_g2"], DECAY_LORA + ICLR_LORA, n_lora)], axis=1).astype(BF16)
    ws = p["gmlp_ws"]
    wsp = jnp.concatenate([ws[0::2], ws[1::2]], axis=2)
    bsx = jnp.repeat(p["gmlp_bs"].T, HEAD_DIM, axis=1)
    row = lambda a: a.reshape(1, -1)
    full = lambda a: pl.BlockSpec(a.shape, lambda b, s: (0,) * a.ndim)
    args = [mod3, row(p["norm1_g"]), wuv, wrkv, wl, row(p["gmlp_ln_w"]), row(p["gmlp_ln_b"]), wsp, bsx,
            mu_rkv, mu_l, lora_w, row(p["rwkv_w0"]), row(p["rwkv_a0"]), row(p["rwkv_k_k"]),
            row(p["rwkv_k_a"]), _head_ones(RWKV_WIDTH)]
    in_specs = [pl.BlockSpec((1, tm, d), lambda b, s: (b, s, 0)),
                pl.BlockSpec((1,) + mod3.shape[1:], lambda b, s: (b, 0, 0))]
    in_specs += [full(a) for a in args[1:]]
    tok = lambda dt: jax.ShapeDtypeStruct((bsz, seq, RWKV_WIDTH), dt)
    out_spec = pl.BlockSpec((1, tm, RWKV_WIDTH), lambda b, s: (b, s, 0))
    outs = [tok(BF16)] * 7 + [tok(F32)]
    return pl.pallas_call(
        _pre_kernel,
        grid=(bsz, seq // tm),
        in_specs=in_specs,
        out_specs=[out_spec] * 8,
        out_shape=outs,
        scratch_shapes=[pltpu.VMEM((8, 3 * RWKV_WIDTH), F32), pltpu.VMEM((8, LORA_PAD), F32)],
        compiler_params=pltpu.CompilerParams(dimension_semantics=("arbitrary", "arbitrary"),
                                             vmem_limit_bytes=VMEM_LIMIT),
        name="pre_mix",
    )(x, *args)


def _blockdiag(xb, mask):
    t = jnp.concatenate([xb] * HEADS_PER_GROUP, axis=0)
    return jnp.where(mask, t, jnp.zeros((), xb.dtype))


def _head_transpose(x):
    xt = x.T
    return jnp.concatenate([xt[hh * HEAD_DIM:(hh + 1) * HEAD_DIM, :] for hh in range(HEADS_PER_GROUP)], axis=1)


def _scan_kernel(x_ref, mod_ref, ya_ref, r_ref, k2_ref, v_ref, kk_ref, al_ref, g_ref, lw_ref,
                 rk_ref, gnw_ref, gnb_ref, ones_ref, wout_ref, g2n_ref, wr_ref, br_ref,
                 x1_ref, h2_ref, route_ref, slab_ref, cnt_ref,
                 state_ref, lhs_scr, y0_scr, nt_scr, bonus_scr, *, steps):
    tm = x_ref.shape[1]
    L = SCAN_L
    n_chunks = tm // L
    n_groups = RWKV_WIDTH // MXU_DIM
    n_p = n_chunks * n_groups
    t = pl.program_id(0)
    cur = (t % 2) * n_p
    prev = n_p - cur

    @pl.when(t == 0)
    def _():
        lhs_scr[...] = jnp.zeros_like(lhs_scr)
        y0_scr[...] = jnp.zeros_like(y0_scr)
        nt_scr[...] = jnp.zeros_like(nt_scr)
        bonus_scr[...] = jnp.zeros_like(bonus_scr)
        cnt_ref[...] = jnp.zeros_like(cnt_ref)

    @pl.when(jnp.maximum(t - 1, 0) % steps == 0)
    def _():
        state_ref[...] = jnp.zeros_like(state_ref)

    r = r_ref[0].astype(F32)
    k2 = k2_ref[0].astype(F32)
    v = v_ref[0].astype(F32)
    kk = kk_ref[0].astype(F32)
    b = kk * al_ref[0].astype(F32)
    lw = lw_ref[0]
    lw_hi = lw.astype(BF16)
    lw_lo = (lw - lw_hi.astype(F32)).astype(BF16)

    trow = lax.broadcasted_iota(I32, (L, L), 0)
    tcol = lax.broadcasted_iota(I32, (L, L), 1)
    tri = (tcol <= trow).astype(BF16)
    row = lax.broadcasted_iota(I32, (L, MXU_DIM), 0)
    lane = lax.broadcasted_iota(I32, (L, MXU_DIM), 1)
    lane_j = lane % HEAD_DIM
    strict = lane_j < row
    incl = lane_j <= row
    eye = lane_j == row
    brow = lax.broadcasted_iota(I32, (MXU_DIM, MXU_DIM), 0) // HEAD_DIM
    bcol = lax.broadcasted_iota(I32, (MXU_DIM, MXU_DIM), 1) // HEAD_DIM
    bmask = brow == bcol

    def bd(xf):
        return _blockdiag(xf.astype(BF16), bmask)

    st = [state_ref[g] for g in range(n_groups)]
    y_rows = []

    def chain_step():
        c = len(y_rows)
        if c >= n_chunks:
            return
        y_heads = []
        for g in range(n_groups):
            i = prev + c * n_groups + g
            res_s = _dot(lhs_scr[i], bd(st[g]))
            y_heads.append(res_s[:L] + y0_scr[i])
            st[g] = res_s[L:] + nt_scr[i]
        y_rows.append(jnp.concatenate(y_heads, axis=1))

    probs = [(c, g) for c in range(n_chunks) for g in range(n_groups)]
    each = lambda fn, *lists: [fn(*a) for a in zip(*lists)]
    lgs = [_dot(tri, lw_hi[c * L:(c + 1) * L]) + _dot(tri, lw_lo[c * L:(c + 1) * L])
           for c in range(n_chunks)]

    def sl(arr, c, g):
        return arr[c * L:(c + 1) * L, g * MXU_DIM:(g + 1) * MXU_DIM]

    lgc = [lgs[c][:, g * MXU_DIM:(g + 1) * MXU_DIM] for c, g in probs]
    glast = [x[L - 1:L, :] for x in lgc]
    eg = [jnp.exp(x) for x in lgc]
    egx = [jnp.exp(x - sl(lw, c, g)) for x, (c, g) in zip(lgc, probs)]
    einv = [jnp.exp(-x) for x in lgc]
    ehat = each(lambda gl, x: jnp.exp(gl - x), glast, lgc)
    at = [-sl(kk, c, g) * e for e, (c, g) in zip(egx, probs)]
    rt = [sl(r, c, g) * e for e, (c, g) in zip(eg, probs)]
    bt = [sl(b, c, g) * e for e, (c, g) in zip(einv, probs)]
    kt = [sl(k2, c, g) * e for e, (c, g) in zip(einv, probs)]
    bh = [sl(b, c, g) * e for e, (c, g) in zip(ehat, probs)]
    kh = [sl(k2, c, g) * e for e, (c, g) in zip(ehat, probs)]
    vc = [sl(v, c, g) for c, g in probs]
    lhs = each(lambda a_, r_: jnp.concatenate([a_, r_], axis=0).astype(BF16), at, rt)
    pb = each(lambda l_, x: _dot_nt(l_, bd(x)), lhs, bt)
    chain_step()
    pk = each(lambda l_, x: _dot_nt(l_, bd(x)), lhs, kt)
    chain_step()
    a_ab = [jnp.where(strict, x[:L], 0.0) for x in pb]
    a_rb = [jnp.where(incl, x[L:], 0.0) for x in pb]
    a_ak = [jnp.where(strict, x[:L], 0.0) for x in pk]
    a_rk = [jnp.where(incl, x[L:], 0.0) for x in pk]
    ssum = [jnp.where(eye, 1.0, 0.0) + x for x in a_ab]
    xb = [x.astype(BF16) for x in a_ab]
    xp = [_dot(x, _blockdiag(x, bmask)) for x in xb]
    chain_step()
    m = 2
    while m < L // 2:
        xb = [x.astype(BF16) for x in xp]
        res = each(lambda x, s_: _dot(jnp.concatenate([x, s_.astype(BF16)], axis=0), _blockdiag(x, bmask)),
                   xb, ssum)
        chain_step()
        xp = [x[:L] for x in res]
        ssum = each(lambda s_, x: s_ + x[L:], ssum, res)
        m *= 2
    tmat = each(lambda s_, x: s_ + _dot(s_.astype(BF16), bd(x)), ssum, xp)
    while len(y_rows) < n_chunks:
        chain_step()
    for g in range(n_groups):
        state_ref[g] = st[g]

    y = jnp.concatenate(y_rows, axis=0)
    ones = ones_ref[...]
    inv_n = 1.0 / HEAD_DIM
    mean = _dot(y.astype(BF16), ones) * inv_n
    dlt = y - mean
    var = _dot((dlt * dlt).astype(BF16), ones) * inv_n
    bht = [_head_transpose(x).astype(BF16) for x in bh]
    kht = [_head_transpose(x) for x in kh]
    res = each(lambda a1, a2, k_, x: _dot(jnp.concatenate([a1, a2, k_], axis=0).astype(BF16), bd(x)),
               a_ak, a_rk, kht, vc)
    akv = [x[:L] for x in res]
    arkv = [x[L:2 * L] for x in res]
    khv = [x[2 * L:] for x in res]
    tb = [x.astype(BF16) for x in tmat]
    wm = each(lambda t_, x: _dot(t_, bd(x)), tb, at)
    u0 = each(lambda t_, x: _dot(t_, bd(x)), tb, akv)

    yn = dlt * lax.rsqrt(var + GN_EPS) * gnw_ref[...] + gnb_ref[...]
    yb = (yn + bonus_scr[(t + 1) % 2]) * g_ref[0].astype(F32)
    cat = jnp.concatenate([ya_ref[0], yb.astype(BF16)], axis=1)
    y2 = _dot(cat, wout_ref[...])
    gate1 = mod_ref[0, 2:3, :]
    shift2 = mod_ref[0, 3:4, :]
    scale2 = mod_ref[0, 4:5, :]
    x1 = x_ref[0] + gate1 * y2
    x1_ref[0] = x1
    h2 = _rms(x1, g2n_ref[...]) * (1.0 + scale2) + shift2
    _store_row_tiles(h2_ref.at[0], h2)

    lhs_w = each(lambda a_, b_: jnp.concatenate([a_.astype(BF16), b_], axis=0), a_rb, bht)
    res_w = each(lambda l_, x: _dot(l_, bd(x)), lhs_w, wm)

    logits = _dot(h2.astype(BF16), wr_ref[...]) + br_ref[...]
    slab = _route_slab(logits)
    slab_ref[...] = slab
    slab_t = slab.T[:ROUTE_ROWS, :]
    route_ref[...] = slab_t
    erow = lax.broadcasted_iota(I32, (N_EXPERTS, tm), 0)
    hits = ((erow == slab_t[0:1, :].astype(I32)).astype(F32)
            + (erow == slab_t[1:2, :].astype(I32)).astype(F32))
    cnt_ref[...] += jnp.where(t > 0, jnp.sum(hits, axis=1, keepdims=True), 0.0)

    res_u = each(lambda l_, x: _dot(l_, bd(x)), lhs_w, u0)
    bonus_scr[t % 2] = _dot((r * k2 * rk_ref[...]).astype(BF16), ones) * v
    for i in range(n_p):
        mt = res_w[i][L:] + jnp.where(eye, jnp.exp(glast[i]), 0.0)
        lhs_scr[cur + i] = jnp.concatenate([rt[i] + res_w[i][:L], mt], axis=0).astype(BF16)
        y0_scr[cur + i] = res_u[i][:L] + arkv[i]
        nt_scr[cur + i] = res_u[i][L:] + khv[i]


def _scan_call(x, mod3, pre_outs, p):
    bsz, seq, d = x.shape
    tm = min(SCAN_TM, seq)
    n_tok = bsz * seq
    steps = seq // tm
    n_route = N_GROUPS + N_EXPERTS
    wr = jnp.pad(jnp.concatenate([p["router_group_w"], p["router_expert_w"]], axis=1),
                 ((0, 0), (0, LANES - n_route))).astype(BF16)
    br = jnp.pad(jnp.concatenate([p["router_group_b"], p["router_expert_b"]]), (0, LANES - n_route)).reshape(1, -1)
    row = lambda a: a.reshape(1, -1)
    full = lambda a: pl.BlockSpec(a.shape, lambda t: (0,) * a.ndim)
    params = [row(p["rwkv_r_k"]), row(p["rwkv_gn_w"]), row(p["rwkv_gn_b"]), _head_ones(RWKV_WIDTH),
              p["w_out"].astype(BF16), row(p["norm2_g"]), wr, br]
    n_tiles = bsz * steps
    n_p = (tm // SCAN_L) * (RWKV_WIDTH // MXU_DIM)
    post = lambda t: jnp.maximum(t - 1, 0)
    prep = lambda t: jnp.minimum(t, n_tiles - 1)
    tok_post = pl.BlockSpec((1, tm, RWKV_WIDTH), lambda t: (post(t) // steps, post(t) % steps, 0))
    tok_prep = pl.BlockSpec((1, tm, RWKV_WIDTH), lambda t: (prep(t) // steps, prep(t) % steps, 0))
    x_spec = pl.BlockSpec((1, tm, d), lambda t: (post(t) // steps, post(t) % steps, 0))
    ya, r, k2, v, kkn, al, g, lw = pre_outs
    in_specs = [x_spec, pl.BlockSpec((1,) + mod3.shape[1:], lambda t: (post(t) // steps, 0, 0)),
                tok_post, tok_prep, tok_prep, tok_prep, tok_prep, tok_prep, tok_post, tok_prep]
    in_specs += [full(a) for a in params]
    return pl.pallas_call(
        functools.partial(_scan_kernel, steps=steps),
        grid=(n_tiles + 1,),
        in_specs=in_specs,
        out_specs=[x_spec,
                   pl.BlockSpec((1, tm * ROW_SUB, LANES), lambda t: (post(t) // steps, post(t) % steps, 0)),
                   pl.BlockSpec((ROUTE_ROWS, tm), lambda t: (0, post(t))),
                   pl.BlockSpec((tm, LANES), lambda t: (post(t), 0)),
                   pl.BlockSpec((N_EXPERTS, LANES), lambda t: (0, 0))],
        out_shape=[jax.ShapeDtypeStruct((bsz, seq, d), F32),
                   jax.ShapeDtypeStruct((bsz, seq * ROW_SUB, LANES), F32),
                   jax.ShapeDtypeStruct((ROUTE_ROWS, n_tok), F32), jax.ShapeDtypeStruct((n_tok, LANES), F32),
                   jax.ShapeDtypeStruct((N_EXPERTS, LANES), F32)],
        scratch_shapes=[pltpu.VMEM((RWKV_WIDTH // MXU_DIM, HEAD_DIM, MXU_DIM), F32),
                        pltpu.VMEM((2 * n_p, 2 * SCAN_L, MXU_DIM), BF16),
                        pltpu.VMEM((2 * n_p, SCAN_L, MXU_DIM), F32),
                        pltpu.VMEM((2 * n_p, SCAN_L, MXU_DIM), F32),
                        pltpu.VMEM((2, tm, RWKV_WIDTH), F32)],
        compiler_params=pltpu.CompilerParams(dimension_semantics=("arbitrary",),
                                             vmem_limit_bytes=VMEM_LIMIT),
        name="scan_post",
    )(x, mod3, ya, r, k2, v, kkn, al, g, lw, *params)


def _route_slab(logits):
    ln = lax.broadcasted_iota(I32, logits.shape, 1)
    neg = jnp.float32(-jnp.inf)
    is_g = ln < N_GROUPS
    gl = jnp.where(is_g, logits, neg)
    gmax = jnp.max(gl, axis=-1, keepdims=True)
    gidx = jnp.min(jnp.where(gl == gmax, ln, LANES), axis=-1, keepdims=True)
    g_p = 1.0 / jnp.sum(jnp.where(is_g, jnp.exp(logits - gmax), 0.0), axis=-1, keepdims=True)
    egrp = (ln - N_GROUPS) // EXPERTS_PER_GROUP
    sel = jnp.where(ln >= N_GROUPS, egrp, -1) == gidx
    el = jnp.where(sel, logits, neg)
    l1 = jnp.max(el, axis=-1, keepdims=True)
    i1 = jnp.min(jnp.where(el == l1, ln, LANES), axis=-1, keepdims=True)
    el2 = jnp.where(ln == i1, neg, el)
    l2 = jnp.max(el2, axis=-1, keepdims=True)
    i2 = jnp.min(jnp.where(el2 == l2, ln, LANES), axis=-1, keepdims=True)
    t = jnp.exp(l2 - l1)
    p1 = 1.0 / (1.0 + t)
    w1 = g_p * p1
    w2 = g_p * (t * p1)
    e1 = (i1 - N_GROUPS).astype(F32)
    e2 = (i2 - N_GROUPS).astype(F32)
    return jnp.where(ln == 0, e1, jnp.where(ln == 1, e2, jnp.where(ln == 2, w1, jnp.where(ln == 3, w2, 0.0))))


def _onehots(route_ref):
    t = route_ref.shape[1]
    erow = lax.broadcasted_iota(I32, (N_EXPERTS, t), 0)
    oh1 = erow == route_ref[0:1, :].astype(I32)
    oh2 = erow == route_ref[1:2, :].astype(I32)
    return oh1, oh2


def _dest_kernel(route_ref, cnt_ref, dest_ref, meta_ref, base_ref, upper_ref):
    t = route_ref.shape[1]
    nbp = meta_ref.shape[1]

    @pl.when(pl.program_id(0) == 0)
    def _():
        sr = lax.broadcasted_iota(I32, (t, t), 0)
        sc = lax.broadcasted_iota(I32, (t, t), 1)
        upper_ref[...] = (sr < sc).astype(BF16)
        cnt = cnt_ref[...]
        padded = jnp.floor((cnt + (MOE_BM - 1)) * (1.0 / MOE_BM)) * MOE_BM
        er = lax.broadcasted_iota(I32, (N_EXPERTS, N_EXPERTS), 0)
        ec = lax.broadcasted_iota(I32, (N_EXPERTS, N_EXPERTS), 1)
        lower = (ec < er).astype(BF16)
        p_hi = padded.astype(BF16)
        p_lo = (padded - p_hi.astype(F32)).astype(BF16)
        pstart = _dot(lower, p_hi) + _dot(lower, p_lo)
        base_ref[...] = pstart
        pend = (pstart + padded)[:, 0:1]
        blk0 = (lax.broadcasted_iota(I32, (N_EXPERTS, nbp), 1) * MOE_BM).astype(F32)
        be = jnp.sum((pend <= blk0).astype(I32), axis=0, keepdims=True)
        meta_ref[0:1, :] = jnp.minimum(be, N_EXPERTS - 1)
        used = (pend[N_EXPERTS - 1:N_EXPERTS, :] * (1.0 / MOE_BM)).astype(I32)
        meta_ref[1:2, :] = jnp.broadcast_to(used, (1, nbp))
        on_diag = (lax.broadcasted_iota(I32, (N_EXPERTS, nbp), 0)
                   == lax.broadcasted_iota(I32, (N_EXPERTS, nbp), 1))
        pad_lo = (pstart + cnt)[:, 0:1]
        pad_len = (padded - cnt)[:, 0:1]
        meta_ref[2:3, :] = jnp.sum(jnp.where(on_diag, pad_lo, 0.0), axis=0, keepdims=True).astype(I32)
        meta_ref[3:4, :] = jnp.sum(jnp.where(on_diag, pad_len, 0.0), axis=0, keepdims=True).astype(I32)
        e_id = lax.broadcasted_iota(I32, (N_EXPERTS, nbp), 0)
        later = jnp.logical_and(e_id > jnp.minimum(be, N_EXPERTS - 1), padded[:, 0:1] > 0.0)
        nxt = jnp.min(jnp.where(later, e_id, N_EXPERTS), axis=0, keepdims=True)
        meta_ref[4:5, :] = jnp.where(nxt == N_EXPERTS, -1, nxt)
        meta_ref[5:, :] = jnp.zeros((meta_ref.shape[0] - 5, nbp), I32)

    oh1, oh2 = _onehots(route_ref)
    cnt = oh1.astype(F32) + oh2.astype(F32)
    pos = _dot(cnt.astype(BF16), upper_ref[...]) + base_ref[:, 0:1]
    dest_ref[0:1, :] = jnp.sum(jnp.where(oh1, pos, 0.0), axis=0, keepdims=True).astype(I32)
    dest_ref[1:2, :] = jnp.sum(jnp.where(oh2, pos, 0.0), axis=0, keepdims=True).astype(I32)
    base_ref[...] += jnp.sum(cnt, axis=1, keepdims=True)


def _n_blocks(n_tok):
    return -(-2 * n_tok // MOE_BM) + N_EXPERTS


def _route_call(route, counts):
    n_tok = route.shape[1]
    t = min(ROUTE_T, n_tok)
    nbp = -(-_n_blocks(n_tok) // LANES) * LANES
    cparams = pltpu.CompilerParams(dimension_semantics=("arbitrary",), vmem_limit_bytes=VMEM_LIMIT)
    route_spec = pl.BlockSpec((ROUTE_ROWS, t), lambda j: (0, j))
    cnt_spec = pl.BlockSpec((N_EXPERTS, LANES), lambda j: (0, 0))
    dest, meta = pl.pallas_call(
        _dest_kernel, grid=(n_tok // t,), in_specs=[route_spec, cnt_spec],
        out_specs=[pl.BlockSpec((2, t), lambda j: (0, j)), pl.BlockSpec((8, nbp), lambda j: (0, 0))],
        out_shape=[jax.ShapeDtypeStruct((2, n_tok), I32), jax.ShapeDtypeStruct((8, nbp), I32)],
        scratch_shapes=[pltpu.VMEM((N_EXPERTS, LANES), F32), pltpu.VMEM((t, t), BF16)],
        compiler_params=cparams, name="route_dest")(route, counts)
    return dest, meta


def _zero_fill_slots(meta_ref, xs_hbm, ztile, zsem, wait):
    nbp = meta_ref.shape[0] // 8
    n_blocks = xs_hbm.shape[0] // (MOE_BM * ROW_SUB)

    def go(rows, first_slot):
        cp = pltpu.make_async_copy(
            ztile.at[pl.ds(0, rows * ROW_SUB), :],
            xs_hbm.at[pl.ds(pl.multiple_of(first_slot * ROW_SUB, ROW_SUB), rows * ROW_SUB), :], zsem)
        if wait:
            cp.wait()
        else:
            cp.start()

    def expert_pad(e, carry):
        slot = meta_ref[2 * nbp + e]
        plen = meta_ref[3 * nbp + e]
        bit = MOE_BM // 2
        while bit >= 1:
            has = (plen & bit) != 0
            pl.when(has)(functools.partial(go, bit, slot))
            slot = slot + jnp.where(has, bit, 0)
            bit //= 2
        return carry

    def tail_block(j, carry):
        go(MOE_BM, j * MOE_BM)
        return carry

    lax.fori_loop(0, N_EXPERTS, expert_pad, 0)
    lax.fori_loop(meta_ref[nbp], n_blocks, tail_block, 0)


def _dispatch_kernel(dest_ref, meta_ref, h_ref, xs_hbm, sem, zsem, ztile):
    n_tok = dest_ref.shape[0] // 2
    t = h_ref.shape[0] // ROW_SUB
    base = pl.program_id(0) * t
    first = pl.program_id(0) == 0

    @pl.when(first)
    def _():
        ztile[...] = jnp.zeros_like(ztile)
        _zero_fill_slots(meta_ref, xs_hbm, ztile, zsem, wait=False)

    def start(i, carry):
        tok = base + i
        src = _row_tile(h_ref, i)
        pltpu.make_async_copy(src, _row_tile(xs_hbm, dest_ref[tok]), sem).start(priority=0)
        pltpu.make_async_copy(src, _row_tile(xs_hbm, dest_ref[n_tok + tok]), sem).start(priority=1)
        return carry

    lax.fori_loop(0, t, start, 0, unroll=DMA_UNROLL)
    for _ in range(2):
        pltpu.make_async_copy(h_ref, xs_hbm.at[pl.ds(0, t * ROW_SUB), :], sem).wait()

    @pl.when(first)
    def _():
        _zero_fill_slots(meta_ref, xs_hbm, ztile, zsem, wait=True)


def _dispatch_call(dest_flat, meta_flat, h_tok, n_slots):
    n_tok = h_tok.shape[0] // ROW_SUB
    t = min(DISPATCH_T, n_tok)
    return pl.pallas_call(
        _dispatch_kernel,
        grid_spec=pltpu.PrefetchScalarGridSpec(
            num_scalar_prefetch=2, grid=(n_tok // t,),
            in_specs=[pl.BlockSpec((t * ROW_SUB, LANES), lambda i, dr, mr: (i, 0))],
            out_specs=pl.BlockSpec(memory_space=pl.ANY),
            scratch_shapes=[pltpu.SemaphoreType.DMA, pltpu.SemaphoreType.DMA,
                            pltpu.VMEM((MOE_BM * ROW_SUB, LANES), h_tok.dtype)]),
        out_shape=jax.ShapeDtypeStruct((n_slots * ROW_SUB, LANES), h_tok.dtype),
        compiler_params=pltpu.CompilerParams(dimension_semantics=("arbitrary",),
                                             vmem_limit_bytes=VMEM_LIMIT),
        name="moe_dispatch",
    )(dest_flat, meta_flat, h_tok)


def _expert_kernel(meta_ref, xs_ref, wg_hbm, wu_hbm, wd_hbm, yb_ref, wgf, wuf, wdf, wgb, wub, wdb,
                   wsem, slot_ref):
    j = pl.program_id(0)
    nbp = meta_ref.shape[0] // 8
    used = j < meta_ref[nbp]
    new_expert = jnp.logical_or(j == 0, meta_ref[j] != meta_ref[jnp.maximum(j - 1, 0)])

    def fetch(e, slot):
        return [pltpu.make_async_copy(src.at[e], dst.at[slot], wsem.at[slot])
                for src, dst in ((wg_hbm, wgf), (wu_hbm, wuf), (wd_hbm, wdf))]

    @pl.when(j == 0)
    def _():
        slot_ref[0] = 0
        for cp in fetch(meta_ref[0], 0):
            cp.start()

    @pl.when(jnp.logical_and(used, new_expert))
    def _():
        slot = slot_ref[0]
        for cp in fetch(meta_ref[j], slot):
            cp.wait()
        nxt = meta_ref[4 * nbp + j]

        @pl.when(nxt >= 0)
        def _():
            for cp in fetch(nxt, 1 - slot):
                cp.start()

        wgb[...] = wgf[slot].astype(BF16)
        wub[...] = wuf[slot].astype(BF16)
        wdb[...] = wdf[slot].astype(BF16)
        slot_ref[0] = 1 - slot

    @pl.when(used)
    def _():
        sub = MOE_BM // EXPERT_SPLIT
        gu = []
        for i in range(EXPERT_SPLIT):
            x = _load_row_tiles(xs_ref.at[pl.ds(i * sub * ROW_SUB, sub * ROW_SUB), :]).astype(BF16)
            gu.append((_dot(x, wgb[...]), _dot(x, wub[...])))
        for i, (gate, up) in enumerate(gu):
            hid = (jax.nn.silu(gate) * up).astype(BF16)
            _store_row_tiles(yb_ref.at[pl.ds(i * sub * ROW_SUB, sub * ROW_SUB), :], _dot(hid, wdb[...]))

    @pl.when(jnp.logical_not(used))
    def _():
        yb_ref[...] = jnp.zeros_like(yb_ref)


def _expert_call(meta_flat, xs, wg, wu, wd):
    n_slots = xs.shape[0] // ROW_SUB
    d, hid = wg.shape[1:]
    blk = pl.BlockSpec((MOE_BM * ROW_SUB, LANES), lambda j, m: (j, 0))
    return pl.pallas_call(
        _expert_kernel,
        grid_spec=pltpu.PrefetchScalarGridSpec(
            num_scalar_prefetch=1, grid=(n_slots // MOE_BM,),
            in_specs=[blk] + [pl.BlockSpec(memory_space=pl.ANY)] * 3,
            out_specs=blk,
            scratch_shapes=[pltpu.VMEM((2, d, hid), F32), pltpu.VMEM((2, d, hid), F32),
                            pltpu.VMEM((2, hid, d), F32),
                            pltpu.VMEM((d, hid), BF16), pltpu.VMEM((d, hid), BF16),
                            pltpu.VMEM((hid, d), BF16),
                            pltpu.SemaphoreType.DMA((2,)), pltpu.SMEM((1,), I32)]),
        out_shape=jax.ShapeDtypeStruct((n_slots * ROW_SUB, LANES), F32),
        compiler_params=pltpu.CompilerParams(dimension_semantics=("arbitrary",),
                                             vmem_limit_bytes=VMEM_LIMIT),
        name="moe_experts",
    )(meta_flat, xs, wg, wu, wd)


def _combine_kernel(dest_ref, yb_hbm, slab_ref, x1_ref, mod_ref, gf_ref, out_ref, gbuf, sem):
    t = x1_ref.shape[0]
    n_tok = dest_ref.shape[0] // 2
    step = pl.program_id(0)
    slot = step % 2

    def issue(tile, buf):
        def body(r, carry):
            tok = tile * t + r
            pltpu.make_async_copy(_row_tile(yb_hbm, dest_ref[tok]), _row_tile(gbuf.at[buf, 0], r),
                                  sem.at[buf]).start(priority=0)
            pltpu.make_async_copy(_row_tile(yb_hbm, dest_ref[n_tok + tok]), _row_tile(gbuf.at[buf, 1], r),
                                  sem.at[buf]).start(priority=1)
            return carry
        lax.fori_loop(0, t, body, 0, unroll=DMA_UNROLL)

    @pl.when(step == 0)
    def _():
        issue(step, slot)

    @pl.when(step + 1 < pl.num_programs(0))
    def _():
        issue(step + 1, 1 - slot)

    for kk in range(2):
        pltpu.make_async_copy(yb_hbm.at[pl.ds(0, t * ROW_SUB), :], gbuf.at[slot, kk], sem.at[slot]).wait()
    slab = slab_ref[...]
    y = (slab[:, 2:3] * _load_row_tiles(gbuf.at[slot, 0])
         + slab[:, 3:4] * _load_row_tiles(gbuf.at[slot, 1]))
    gate2 = mod_ref[0, 5:6, :]
    x2 = x1_ref[...] + gate2 * y
    out_ref[...] = _rms(x2, gf_ref[...])


def _combine_call(dest_flat, yb, slab, x1_tok, mod3, gf, seq):
    n_tok, d = x1_tok.shape
    t = min(COMBINE_T, seq)
    return pl.pallas_call(
        _combine_kernel,
        grid_spec=pltpu.PrefetchScalarGridSpec(
            num_scalar_prefetch=1, grid=(n_tok // t,),
            in_specs=[pl.BlockSpec(memory_space=pl.ANY),
                      pl.BlockSpec((t, LANES), lambda i, dr: (i, 0)),
                      pl.BlockSpec((t, d), lambda i, dr: (i, 0)),
                      pl.BlockSpec((1,) + mod3.shape[1:], lambda i, dr: ((i * t) // seq, 0, 0)),
                      pl.BlockSpec((1, d), lambda i, dr: (0, 0))],
            out_specs=pl.BlockSpec((t, d), lambda i, dr: (i, 0)),
            scratch_shapes=[pltpu.VMEM((2, 2, t * ROW_SUB, LANES), F32), pltpu.SemaphoreType.DMA((2,))]),
        out_shape=jax.ShapeDtypeStruct((n_tok, d), F32),
        compiler_params=pltpu.CompilerParams(dimension_semantics=("arbitrary",),
                                             vmem_limit_bytes=VMEM_LIMIT),
        name="moe_combine",
    )(dest_flat, yb, slab, x1_tok, mod3, gf.reshape(1, d))


def kernel(x, c, ada_w, ada_b, norm1_g, w_in, gmlp_ln_w, gmlp_ln_b, gmlp_ws, gmlp_bs, rwkv_mu, rwkv_w0,
           rwkv_w2, rwkv_a0, rwkv_a2, rwkv_g2, rwkv_k_k, rwkv_k_a, rwkv_r_k, rwkv_gn_w, rwkv_gn_b, w_out,
           norm2_g, router_group_w, router_group_b, router_expert_w, router_expert_b, moe_w_gate,
           moe_w_up, moe_w_down, final_norm_g):
    bsz, seq, d = x.shape
    n_tok = bsz * seq
    assert ada_w.shape[0] == 1, "the final rmsnorm is fused into the (single) layer's combine kernel"
    layer = lambda a: a.reshape(a.shape[1:])
    p = dict(norm1_g=layer(norm1_g), w_in=layer(w_in), gmlp_ln_w=layer(gmlp_ln_w), gmlp_ln_b=layer(gmlp_ln_b),
             gmlp_ws=layer(gmlp_ws), gmlp_bs=layer(gmlp_bs), rwkv_mu=layer(rwkv_mu), rwkv_w0=layer(rwkv_w0),
             rwkv_w2=layer(rwkv_w2), rwkv_a0=layer(rwkv_a0), rwkv_a2=layer(rwkv_a2), rwkv_g2=layer(rwkv_g2),
             rwkv_k_k=layer(rwkv_k_k), rwkv_k_a=layer(rwkv_k_a), rwkv_r_k=layer(rwkv_r_k),
             rwkv_gn_w=layer(rwkv_gn_w), rwkv_gn_b=layer(rwkv_gn_b), w_out=layer(w_out), norm2_g=layer(norm2_g),
             router_group_w=layer(router_group_w), router_group_b=layer(router_group_b),
             router_expert_w=layer(router_expert_w), router_expert_b=layer(router_expert_b))
    mod3 = _mod_call(c, layer(ada_w), layer(ada_b)).reshape(bsz, 6, d)
    pre = _pre_call(x, mod3, p)
    x1, h2, route, slab, counts = _scan_call(x, mod3, pre, p)
    dest, meta = _route_call(route, counts)
    dest_flat = dest.reshape(-1)
    n_slots = _n_blocks(n_tok) * MOE_BM
    meta_flat = meta.reshape(-1)
    xs = _dispatch_call(dest_flat, meta_flat, h2.reshape(n_tok * ROW_SUB, LANES), n_slots)
    yb = _expert_call(meta_flat, xs, layer(moe_w_gate), layer(moe_w_up), layer(moe_w_down))
    out = _combine_call(dest_flat, yb, slab, x1.reshape(n_tok, d), mod3, final_norm_g, seq)
    return out.reshape(bsz, seq, d)
```

```python
import functools

import jax
import jax.numpy as jnp
from jax import lax
from jax.experimental import pallas as pl
from jax.experimental.pallas import tpu as pltpu

F32 = jnp.float32
BF16 = jnp.bfloat16
I32 = jnp.int32

HEAD_DIM = 64
GMLP_WIDTH = 512
RWKV_WIDTH = 512
GMLP_CHUNK = 128
DECAY_LORA = 32
ICLR_LORA = 32
GATE_LORA = 96
LORA_PAD = 256
N_GROUPS = 4
EXPERTS_PER_GROUP = 8
N_EXPERTS = 32
RMS_EPS = 1e-6
LN_EPS = 1e-5
GN_EPS = 64e-5

LANES = 128
MXU_DIM = 256
VMEM_LIMIT = 56 * 1024 * 1024

PRE_TM = 1024
PRE_SUB = 256
SCAN_TM = 512
SCAN_L = 64
HEADS_PER_GROUP = MXU_DIM // HEAD_DIM
ROUTE_T = 512
MOE_BM = 512
EXPERT_SPLIT = 2
DISPATCH_T = 4096
COMBINE_T = 256
ROUTE_ROWS = 8
DMA_UNROLL = 8


def _dot(a, b):
    return jnp.dot(a, b, preferred_element_type=F32)


def _dot_nt(a, b):
    return lax.dot_general(a, b, (((1,), (1,)), ((), ())), preferred_element_type=F32)


def _rms(x, g):
    return x * lax.rsqrt(jnp.mean(x * x, axis=-1, keepdims=True) + RMS_EPS) * g


ROW_SUB = 8


def _store_row_tiles(ref, val):
    m = val.shape[0]
    for s in range(ROW_SUB):
        ref[pl.ds(s, m, stride=ROW_SUB), :] = val[:, s * LANES:(s + 1) * LANES]


def _load_row_tiles(ref):
    m = ref.shape[0] // ROW_SUB
    return jnp.concatenate([ref[pl.ds(s, m, stride=ROW_SUB), :] for s in range(ROW_SUB)], axis=1)


def _row_tile(ref, row):
    return ref.at[pl.ds(pl.multiple_of(row * ROW_SUB, ROW_SUB), ROW_SUB), :]


def _mod_kernel(c_ref, w_ref, b_ref, o_ref):
    c = c_ref[...]
    w = w_ref[...]
    c_hi = c.astype(BF16)
    c_lo = (c - c_hi.astype(F32)).astype(BF16)
    w_hi = w.astype(BF16)
    w_lo = (w - w_hi.astype(F32)).astype(BF16)
    o_ref[...] = _dot(c_hi, w_hi) + _dot(c_lo, w_hi) + _dot(c_hi, w_lo) + b_ref[...]


def _mod_call(c, ada_w, ada_b):
    bsz, d = c.shape
    n = ada_w.shape[1]
    tn = 1024
    return pl.pallas_call(
        _mod_kernel,
        grid=(n // tn,),
        in_specs=[pl.BlockSpec((bsz, d), lambda j: (0, 0)),
                  pl.BlockSpec((d, tn), lambda j: (0, j)),
                  pl.BlockSpec((1, tn), lambda j: (0, j))],
        out_specs=pl.BlockSpec((bsz, tn), lambda j: (0, j)),
        out_shape=jax.ShapeDtypeStruct((bsz, n), F32),
        compiler_params=pltpu.CompilerParams(dimension_semantics=("arbitrary",),
                                             vmem_limit_bytes=VMEM_LIMIT),
        name="adaln_mod",
    )(c, ada_w, ada_b.reshape(1, n))


def _token_shift(z, carry_ref):
    tm = z.shape[0]
    prev = pltpu.roll(z, 1, axis=0)
    row = lax.broadcasted_iota(I32, z.shape, 0)
    prev = jnp.where(row == 0, carry_ref[0:1, :], prev)
    carry_ref[0:1, :] = z[tm - 1:tm, :]
    return prev


def _pre_kernel(x_ref, mod_ref, g1_ref, wuv_ref, wrkv_ref, wl_ref, lnw_ref, lnb_ref, wsp_ref,
                bsx_ref, mu_rkv_ref, mu_l_ref, lora_w_ref, w0_ref, a0_ref, kk_ref, ka_ref, ones_ref,
                ya_ref, r_ref, k2_ref, v_ref, kkn_ref, al_ref, g_ref, lw_ref,
                carry_rkv, carry_l):
    tm = x_ref.shape[1]

    @pl.when(pl.program_id(1) == 0)
    def _():
        carry_rkv[...] = jnp.zeros_like(carry_rkv)
        carry_l[...] = jnp.zeros_like(carry_l)

    shift1 = mod_ref[0, 0:1, :]
    scale1 = mod_ref[0, 1:2, :]
    sub = min(PRE_SUB, tm)
    zs = []
    for i in range(tm // sub):
        x = x_ref[0, i * sub:(i + 1) * sub, :]
        hb = (_rms(x, g1_ref[...]) * (1.0 + scale1) + shift1).astype(BF16)
        zs.append((_dot(hb, wuv_ref[...]),
                   _dot(hb, wrkv_ref[...]),
                   _dot(hb, wl_ref[...])))

    lane = lax.broadcasted_iota(I32, (GMLP_CHUNK, LANES), 1)
    left = lane < HEAD_DIM
    trow = lax.broadcasted_iota(I32, (GMLP_CHUNK, 2 * GMLP_CHUNK), 0)
    tcol = lax.broadcasted_iota(I32, (GMLP_CHUNK, 2 * GMLP_CHUNK), 1) % GMLP_CHUNK
    causal = tcol <= trow
    zero = jnp.zeros((), BF16)
    wpairs = [jnp.where(causal, wsp_ref[p], 0.0).astype(BF16) for p in range(GMLP_WIDTH // LANES)]

    for i, (zuv, zr, zl) in enumerate(zs):
        base = i * sub
        u = jax.nn.gelu(zuv[:, :GMLP_WIDTH])
        v = jax.nn.gelu(zuv[:, GMLP_WIDTH:])
        m = jnp.mean(v, axis=-1, keepdims=True)
        var = jnp.mean(jnp.square(v - m), axis=-1, keepdims=True)
        vn = ((v - m) * lax.rsqrt(var + LN_EPS) * lnw_ref[...] + lnb_ref[...]).astype(BF16)
        for p, wpair in enumerate(wpairs):
            for c in range(sub // GMLP_CHUNK):
                rows = slice(c * GMLP_CHUNK, (c + 1) * GMLP_CHUNK)
                cols = slice(p * LANES, (p + 1) * LANES)
                vc = vn[rows, cols]
                rhs = jnp.concatenate([jnp.where(left, vc, zero), jnp.where(left, zero, vc)], axis=0)
                mixed = _dot(wpair, rhs) + bsx_ref[:, cols]
                ya_ref[0, base + c * GMLP_CHUNK:base + (c + 1) * GMLP_CHUNK, cols] = (
                    u[rows, cols] * mixed).astype(ya_ref.dtype)

        out = slice(base, base + sub)
        zr = zr + (_token_shift(zr, carry_rkv) - zr) * mu_rkv_ref[...]
        zl = zl + (_token_shift(zl, carry_l) - zl) * mu_l_ref[...]
        r = zr[:, :RWKV_WIDTH]
        k = zr[:, RWKV_WIDTH:2 * RWKV_WIDTH]
        vv = zr[:, 2 * RWKV_WIDTH:]
        llane = lax.broadcasted_iota(I32, zl.shape, 1)
        f = jnp.where(llane < DECAY_LORA, jnp.tanh(zl),
                      jnp.where(llane < DECAY_LORA + ICLR_LORA, zl, jax.nn.sigmoid(zl)))
        lo = _dot(f.astype(BF16), lora_w_ref[...])
        wpre = w0_ref[...] + lo[:, :RWKV_WIDTH]
        w_log = -(jnp.maximum(-wpre, 0.0) + jnp.log(1.0 + jnp.exp(-jnp.abs(wpre)))) - 0.5
        lw_ref[0, out, :] = -jnp.exp(w_log)
        alpha = jax.nn.sigmoid(a0_ref[...] + lo[:, RWKV_WIDTH:2 * RWKV_WIDTH])
        g_ref[0, out, :] = lo[:, 2 * RWKV_WIDTH:].astype(g_ref.dtype)
        kkr = k * kk_ref[...]
        ss = _dot((kkr * kkr).astype(BF16), ones_ref[...])
        kkn_ref[0, out, :] = (kkr * lax.rsqrt(jnp.maximum(ss, 1e-24))).astype(kkn_ref.dtype)
        k2_ref[0, out, :] = (k * (1.0 + (alpha - 1.0) * ka_ref[...])).astype(k2_ref.dtype)
        r_ref[0, out, :] = r.astype(r_ref.dtype)
        v_ref[0, out, :] = vv.astype(v_ref.dtype)
        al_ref[0, out, :] = alpha.astype(al_ref.dtype)


def _head_ones(width):
    hid = jnp.arange(width) // HEAD_DIM
    return (hid[:, None] == hid[None, :]).astype(BF16)


def _pre_call(x, mod3, p):
    bsz, seq, d = x.shape
    tm = min(PRE_TM, seq)
    w_in = p["w_in"]
    wuv = w_in[:, :2 * GMLP_WIDTH].astype(BF16)
    wrkv = w_in[:, 2 * GMLP_WIDTH:2 * GMLP_WIDTH + 3 * RWKV_WIDTH].astype(BF16)
    n_lora = DECAY_LORA + ICLR_LORA + GATE_LORA
    wl = jnp.pad(w_in[:, 2 * GMLP_WIDTH + 3 * RWKV_WIDTH:], ((0, 0), (0, LORA_PAD - n_lora))).astype(BF16)
    mu = p["rwkv_mu"]
    mu_rkv = mu[:3 * RWKV_WIDTH].reshape(1, -1)
    mu_l = jnp.pad(mu[3 * RWKV_WIDTH:], (0, LORA_PAD - n_lora)).reshape(1, -1)
    zpad = lambda a, lo_, hi_: jnp.pad(a, ((lo_, LORA_PAD - hi_), (0, 0)))
    lora_w = jnp.concatenate([
        zpad(p["rwkv_w2"], 0, DECAY_LORA),
        zpad(p["rwkv_a2"], DECAY_LORA, DECAY_LORA + ICLR_LORA),
        zpad(p["rwkv_g2"], DECAY_LORA + ICLR_LORA, n_lora)], axis=1).astype(BF16)
    ws = p["gmlp_ws"]
    wsp = jnp.concatenate([ws[0::2], ws[1::2]], axis=2)
    bsx = jnp.repeat(p["gmlp_bs"].T, HEAD_DIM, axis=1)
    row = lambda a: a.reshape(1, -1)
    full = lambda a: pl.BlockSpec(a.shape, lambda b, s: (0,) * a.ndim)
    args = [mod3, row(p["norm1_g"]), wuv, wrkv, wl, row(p["gmlp_ln_w"]), row(p["gmlp_ln_b"]), wsp, bsx,
            mu_rkv, mu_l, lora_w, row(p["rwkv_w0"]), row(p["rwkv_a0"]), row(p["rwkv_k_k"]),
            row(p["rwkv_k_a"]), _head_ones(RWKV_WIDTH)]
    in_specs = [pl.BlockSpec((1, tm, d), lambda b, s: (b, s, 0)),
                pl.BlockSpec((1,) + mod3.shape[1:], lambda b, s: (b, 0, 0))]
    in_specs += [full(a) for a in args[1:]]
    tok = lambda dt: jax.ShapeDtypeStruct((bsz, seq, RWKV_WIDTH), dt)
    out_spec = pl.BlockSpec((1, tm, RWKV_WIDTH), lambda b, s: (b, s, 0))
    outs = [tok(BF16)] * 7 + [tok(F32)]
    return pl.pallas_call(
        _pre_kernel,
        grid=(bsz, seq // tm),
        in_specs=in_specs,
        out_specs=[out_spec] * 8,
        out_shape=outs,
        scratch_shapes=[pltpu.VMEM((8, 3 * RWKV_WIDTH), F32), pltpu.VMEM((8, LORA_PAD), F32)],
        compiler_params=pltpu.CompilerParams(dimension_semantics=("arbitrary", "arbitrary"),
                                             vmem_limit_bytes=VMEM_LIMIT),
        name="pre_mix",
    )(x, *args)


def _blockdiag(xb, mask):
    t = jnp.concatenate([xb] * HEADS_PER_GROUP, axis=0)
    return jnp.where(mask, t, jnp.zeros((), xb.dtype))


def _head_transpose(x):
    xt = x.T
    return jnp.concatenate([xt[hh * HEAD_DIM:(hh + 1) * HEAD_DIM, :] for hh in range(HEADS_PER_GROUP)], axis=1)


def _scan_kernel(x_ref, mod_ref, ya_ref, r_ref, k2_ref, v_ref, kk_ref, al_ref, g_ref, lw_ref,
                 rk_ref, gnw_ref, gnb_ref, ones_ref, wout_ref, g2n_ref, wr_ref, br_ref,
                 x1_ref, h2_ref, route_ref, slab_ref, cnt_ref,
                 state_ref, lhs_scr, y0_scr, nt_scr, bonus_scr, *, steps):
    tm = x_ref.shape[1]
    L = SCAN_L
    n_chunks = tm // L
    n_groups = RWKV_WIDTH // MXU_DIM
    n_p = n_chunks * n_groups
    t = pl.program_id(0)
    cur = (t % 2) * n_p
    prev = n_p - cur

    @pl.when(t == 0)
    def _():
        lhs_scr[...] = jnp.zeros_like(lhs_scr)
        y0_scr[...] = jnp.zeros_like(y0_scr)
        nt_scr[...] = jnp.zeros_like(nt_scr)
        bonus_scr[...] = jnp.zeros_like(bonus_scr)
        cnt_ref[...] = jnp.zeros_like(cnt_ref)

    @pl.when(jnp.maximum(t - 1, 0) % steps == 0)
    def _():
        state_ref[...] = jnp.zeros_like(state_ref)

    r = r_ref[0].astype(F32)
    k2 = k2_ref[0].astype(F32)
    v = v_ref[0].astype(F32)
    kk = kk_ref[0].astype(F32)
    b = kk * al_ref[0].astype(F32)
    lw = lw_ref[0]
    lw_hi = lw.astype(BF16)
    lw_lo = (lw - lw_hi.astype(F32)).astype(BF16)

    trow = lax.broadcasted_iota(I32, (L, L), 0)
    tcol = lax.broadcasted_iota(I32, (L, L), 1)
    tri = (tcol <= trow).astype(BF16)
    row = lax.broadcasted_iota(I32, (L, MXU_DIM), 0)
    lane = lax.broadcasted_iota(I32, (L, MXU_DIM), 1)
    lane_j = lane % HEAD_DIM
    strict = lane_j < row
    incl = lane_j <= row
    eye = lane_j == row
    brow = lax.broadcasted_iota(I32, (MXU_DIM, MXU_DIM), 0) // HEAD_DIM
    bcol = lax.broadcasted_iota(I32, (MXU_DIM, MXU_DIM), 1) // HEAD_DIM
    bmask = brow == bcol

    def bd(xf):
        return _blockdiag(xf.astype(BF16), bmask)

    st = [state_ref[g] for g in range(n_groups)]
    y_rows = []

    def chain_step():
        c = len(y_rows)
        if c >= n_chunks:
            return
        y_heads = []
        for g in range(n_groups):
            i = prev + c * n_groups + g
            res_s = _dot(lhs_scr[i], bd(st[g]))
            y_heads.append(res_s[:L] + y0_scr[i])
            st[g] = res_s[L:] + nt_scr[i]
        y_rows.append(jnp.concatenate(y_heads, axis=1))

    probs = [(c, g) for c in range(n_chunks) for g in range(n_groups)]
    each = lambda fn, *lists: [fn(*a) for a in zip(*lists)]
    lgs = [_dot(tri, lw_hi[c * L:(c + 1) * L]) + _dot(tri, lw_lo[c * L:(c + 1) * L])
           for c in range(n_chunks)]

    def sl(arr, c, g):
        return arr[c * L:(c + 1) * L, g * MXU_DIM:(g + 1) * MXU_DIM]

    lgc = [lgs[c][:, g * MXU_DIM:(g + 1) * MXU_DIM] for c, g in probs]
    glast = [x[L - 1:L, :] for x in lgc]
    eg = [jnp.exp(x) for x in lgc]
    egx = [jnp.exp(x - sl(lw, c, g)) for x, (c, g) in zip(lgc, probs)]
    einv = [jnp.exp(-x) for x in lgc]
    ehat = each(lambda gl, x: jnp.exp(gl - x), glast, lgc)
    at = [-sl(kk, c, g) * e for e, (c, g) in zip(egx, probs)]
    rt = [sl(r, c, g) * e for e, (c, g) in zip(eg, probs)]
    bt = [sl(b, c, g) * e for e, (c, g) in zip(einv, probs)]
    kt = [sl(k2, c, g) * e for e, (c, g) in zip(einv, probs)]
    bh = [sl(b, c, g) * e for e, (c, g) in zip(ehat, probs)]
    kh = [sl(k2, c, g) * e for e, (c, g) in zip(ehat, probs)]
    vc = [sl(v, c, g) for c, g in probs]
    lhs = each(lambda a_, r_: jnp.concatenate([a_, r_], axis=0).astype(BF16), at, rt)
    pb = each(lambda l_, x: _dot_nt(l_, bd(x)), lhs, bt)
    chain_step()
    pk = each(lambda l_, x: _dot_nt(l_, bd(x)), lhs, kt)
    chain_step()
    a_ab = [jnp.where(strict, x[:L], 0.0) for x in pb]
    a_rb = [jnp.where(incl, x[L:], 0.0) for x in pb]
    a_ak = [jnp.where(strict, x[:L], 0.0) for x in pk]
    a_rk = [jnp.where(incl, x[L:], 0.0) for x in pk]
    ssum = [jnp.where(eye, 1.0, 0.0) + x for x in a_ab]
    xb = [x.astype(BF16) for x in a_ab]
    xp = [_dot(x, _blockdiag(x, bmask)) for x in xb]
    chain_step()
    m = 2
    while m < L // 2:
        xb = [x.astype(BF16) for x in xp]
        res = each(lambda x, s_: _dot(jnp.concatenate([x, s_.astype(BF16)], axis=0), _blockdiag(x, bmask)),
                   xb, ssum)
        chain_step()
        xp = [x[:L] for x in res]
        ssum = each(lambda s_, x: s_ + x[L:], ssum, res)
        m *= 2
    tmat = each(lambda s_, x: s_ + _dot(s_.astype(BF16), bd(x)), ssum, xp)
    while len(y_rows) < n_chunks:
        chain_step()
    for g in range(n_groups):
        state_ref[g] = st[g]

    y = jnp.concatenate(y_rows, axis=0)
    ones = ones_ref[...]
    inv_n = 1.0 / HEAD_DIM
    mean = _dot(y.astype(BF16), ones) * inv_n
    dlt = y - mean
    var = _dot((dlt * dlt).astype(BF16), ones) * inv_n
    bht = [_head_transpose(x).astype(BF16) for x in bh]
    kht = [_head_transpose(x) for x in kh]
    res = each(lambda a1, a2, k_, x: _dot(jnp.concatenate([a1, a2, k_], axis=0).astype(BF16), bd(x)),
               a_ak, a_rk, kht, vc)
    akv = [x[:L] for x in res]
    arkv = [x[L:2 * L] for x in res]
    khv = [x[2 * L:] for x in res]
    tb = [x.astype(BF16) for x in tmat]
    wm = each(lambda t_, x: _dot(t_, bd(x)), tb, at)
    u0 = each(lambda t_, x: _dot(t_, bd(x)), tb, akv)

    yn = dlt * lax.rsqrt(var + GN_EPS) * gnw_ref[...] + gnb_ref[...]
    yb = (yn + bonus_scr[(t + 1) % 2]) * g_ref[0].astype(F32)
    cat = jnp.concatenate([ya_ref[0], yb.astype(BF16)], axis=1)
    y2 = _dot(cat, wout_ref[...])
    gate1 = mod_ref[0, 2:3, :]
    shift2 = mod_ref[0, 3:4, :]
    scale2 = mod_ref[0, 4:5, :]
    x1 = x_ref[0] + gate1 * y2
    x1_ref[0] = x1
    h2 = _rms(x1, g2n_ref[...]) * (1.0 + scale2) + shift2
    _store_row_tiles(h2_ref.at[0], h2)

    lhs_w = each(lambda a_, b_: jnp.concatenate([a_.astype(BF16), b_], axis=0), a_rb, bht)
    res_w = each(lambda l_, x: _dot(l_, bd(x)), lhs_w, wm)

    logits = _dot(h2.astype(BF16), wr_ref[...]) + br_ref[...]
    slab = _route_slab(logits)
    slab_ref[...] = slab
    slab_t = slab.T[:ROUTE_ROWS, :]
    route_ref[...] = slab_t
    erow = lax.broadcasted_iota(I32, (N_EXPERTS, tm), 0)
    hits = ((erow == slab_t[0:1, :].astype(I32)).astype(F32)
            + (erow == slab_t[1:2, :].astype(I32)).astype(F32))
    cnt_ref[...] += jnp.where(t > 0, jnp.sum(hits, axis=1, keepdims=True), 0.0)

    res_u = each(lambda l_, x: _dot(l_, bd(x)), lhs_w, u0)
    bonus_scr[t % 2] = _dot((r * k2 * rk_ref[...]).astype(BF16), ones) * v
    for i in range(n_p):
        mt = res_w[i][L:] + jnp.where(eye, jnp.exp(glast[i]), 0.0)
        lhs_scr[cur + i] = jnp.concatenate([rt[i] + res_w[i][:L], mt], axis=0).astype(BF16)
        y0_scr[cur + i] = res_u[i][:L] + arkv[i]
        nt_scr[cur + i] = res_u[i][L:] + khv[i]


def _scan_call(x, mod3, pre_outs, p):
    bsz, seq, d = x.shape
    tm = min(SCAN_TM, seq)
    n_tok = bsz * seq
    steps = seq // tm
    n_route = N_GROUPS + N_EXPERTS
    wr = jnp.pad(jnp.concatenate([p["router_group_w"], p["router_expert_w"]], axis=1),
                 ((0, 0), (0, LANES - n_route))).astype(BF16)
    br = jnp.pad(jnp.concatenate([p["router_group_b"], p["router_expert_b"]]), (0, LANES - n_route)).reshape(1, -1)
    row = lambda a: a.reshape(1, -1)
    full = lambda a: pl.BlockSpec(a.shape, lambda t: (0,) * a.ndim)
    params = [row(p["rwkv_r_k"]), row(p["rwkv_gn_w"]), row(p["rwkv_gn_b"]), _head_ones(RWKV_WIDTH),
              p["w_out"].astype(BF16), row(p["norm2_g"]), wr, br]
    n_tiles = bsz * steps
    n_p = (tm // SCAN_L) * (RWKV_WIDTH // MXU_DIM)
    post = lambda t: jnp.maximum(t - 1, 0)
    prep = lambda t: jnp.minimum(t, n_tiles - 1)
    tok_post = pl.BlockSpec((1, tm, RWKV_WIDTH), lambda t: (post(t) // steps, post(t) % steps, 0))
    tok_prep = pl.BlockSpec((1, tm, RWKV_WIDTH), lambda t: (prep(t) // steps, prep(t) % steps, 0))
    x_spec = pl.BlockSpec((1, tm, d), lambda t: (post(t) // steps, post(t) % steps, 0))
    ya, r, k2, v, kkn, al, g, lw = pre_outs
    in_specs = [x_spec, pl.BlockSpec((1,) + mod3.shape[1:], lambda t: (post(t) // steps, 0, 0)),
                tok_post, tok_prep, tok_prep, tok_prep, tok_prep, tok_prep, tok_post, tok_prep]
    in_specs += [full(a) for a in params]
    return pl.pallas_call(
        functools.partial(_scan_kernel, steps=steps),
        grid=(n_tiles + 1,),
        in_specs=in_specs,
        out_specs=[x_spec,
                   pl.BlockSpec((1, tm * ROW_SUB, LANES), lambda t: (post(t) // steps, post(t) % steps, 0)),
                   pl.BlockSpec((ROUTE_ROWS, tm), lambda t: (0, post(t))),
                   pl.BlockSpec((tm, LANES), lambda t: (post(t), 0)),
                   pl.BlockSpec((N_EXPERTS, LANES), lambda t: (0, 0))],
        out_shape=[jax.ShapeDtypeStruct((bsz, seq, d), F32),
                   jax.ShapeDtypeStruct((bsz, seq * ROW_SUB, LANES), F32),
                   jax.ShapeDtypeStruct((ROUTE_ROWS, n_tok), F32), jax.ShapeDtypeStruct((n_tok, LANES), F32),
                   jax.ShapeDtypeStruct((N_EXPERTS, LANES), F32)],
        scratch_shapes=[pltpu.VMEM((RWKV_WIDTH // MXU_DIM, HEAD_DIM, MXU_DIM), F32),
                        pltpu.VMEM((2 * n_p, 2 * SCAN_L, MXU_DIM), BF16),
                        pltpu.VMEM((2 * n_p, SCAN_L, MXU_DIM), F32),
                        pltpu.VMEM((2 * n_p, SCAN_L, MXU_DIM), F32),
                        pltpu.VMEM((2, tm, RWKV_WIDTH), F32)],
        compiler_params=pltpu.CompilerParams(dimension_semantics=("arbitrary",),
                                             vmem_limit_bytes=VMEM_LIMIT),
        name="scan_post",
    )(x, mod3, ya, r, k2, v, kkn, al, g, lw, *params)


def _route_slab(logits):
    ln = lax.broadcasted_iota(I32, logits.shape, 1)
    neg = jnp.float32(-jnp.inf)
    is_g = ln < N_GROUPS
    gl = jnp.where(is_g, logits, neg)
    gmax = jnp.max(gl, axis=-1, keepdims=True)
    gidx = jnp.min(jnp.where(gl == gmax, ln, LANES), axis=-1, keepdims=True)
    g_p = 1.0 / jnp.sum(jnp.where(is_g, jnp.exp(logits - gmax), 0.0), axis=-1, keepdims=True)
    egrp = (ln - N_GROUPS) // EXPERTS_PER_GROUP
    sel = jnp.where(ln >= N_GROUPS, egrp, -1) == gidx
    el = jnp.where(sel, logits, neg)
    l1 = jnp.max(el, axis=-1, keepdims=True)
    i1 = jnp.min(jnp.where(el == l1, ln, LANES), axis=-1, keepdims=True)
    el2 = jnp.where(ln == i1, neg, el)
    l2 = jnp.max(el2, axis=-1, keepdims=True)
    i2 = jnp.min(jnp.where(el2 == l2, ln, LANES), axis=-1, keepdims=True)
    t = jnp.exp(l2 - l1)
    p1 = 1.0 / (1.0 + t)
    w1 = g_p * p1
    w2 = g_p * (t * p1)
    e1 = (i1 - N_GROUPS).astype(F32)
    e2 = (i2 - N_GROUPS).astype(F32)
    return jnp.where(ln == 0, e1, jnp.where(ln == 1, e2, jnp.where(ln == 2, w1, jnp.where(ln == 3, w2, 0.0))))


def _onehots(route_ref):
    t = route_ref.shape[1]
    erow = lax.broadcasted_iota(I32, (N_EXPERTS, t), 0)
    oh1 = erow == route_ref[0:1, :].astype(I32)
    oh2 = erow == route_ref[1:2, :].astype(I32)
    return oh1, oh2


def _dest_kernel(route_ref, cnt_ref, dest_ref, meta_ref, base_ref, upper_ref):
    t = route_ref.shape[1]
    nbp = meta_ref.shape[1]

    @pl.when(pl.program_id(0) == 0)
    def _():
        sr = lax.broadcasted_iota(I32, (t, t), 0)
        sc = lax.broadcasted_iota(I32, (t, t), 1)
        upper_ref[...] = (sr < sc).astype(BF16)
        cnt = cnt_ref[...]
        padded = jnp.floor((cnt + (MOE_BM - 1)) * (1.0 / MOE_BM)) * MOE_BM
        er = lax.broadcasted_iota(I32, (N_EXPERTS, N_EXPERTS), 0)
        ec = lax.broadcasted_iota(I32, (N_EXPERTS, N_EXPERTS), 1)
        lower = (ec < er).astype(BF16)
        p_hi = padded.astype(BF16)
        p_lo = (padded - p_hi.astype(F32)).astype(BF16)
        pstart = _dot(lower, p_hi) + _dot(lower, p_lo)
        base_ref[...] = pstart
        pend = (pstart + padded)[:, 0:1]
        blk0 = (lax.broadcasted_iota(I32, (N_EXPERTS, nbp), 1) * MOE_BM).astype(F32)
        be = jnp.sum((pend <= blk0).astype(I32), axis=0, keepdims=True)
        meta_ref[0:1, :] = jnp.minimum(be, N_EXPERTS - 1)
        used = (pend[N_EXPERTS - 1:N_EXPERTS, :] * (1.0 / MOE_BM)).astype(I32)
        meta_ref[1:2, :] = jnp.broadcast_to(used, (1, nbp))
        on_diag = (lax.broadcasted_iota(I32, (N_EXPERTS, nbp), 0)
                   == lax.broadcasted_iota(I32, (N_EXPERTS, nbp), 1))
        pad_lo = (pstart + cnt)[:, 0:1]
        pad_len = (padded - cnt)[:, 0:1]
        meta_ref[2:3, :] = jnp.sum(jnp.where(on_diag, pad_lo, 0.0), axis=0, keepdims=True).astype(I32)
        meta_ref[3:4, :] = jnp.sum(jnp.where(on_diag, pad_len, 0.0), axis=0, keepdims=True).astype(I32)
        e_id = lax.broadcasted_iota(I32, (N_EXPERTS, nbp), 0)
        later = jnp.logical_and(e_id > jnp.minimum(be, N_EXPERTS - 1), padded[:, 0:1] > 0.0)
        nxt = jnp.min(jnp.where(later, e_id, N_EXPERTS), axis=0, keepdims=True)
        meta_ref[4:5, :] = jnp.where(nxt == N_EXPERTS, -1, nxt)
        meta_ref[5:, :] = jnp.zeros((meta_ref.shape[0] - 5, nbp), I32)

    oh1, oh2 = _onehots(route_ref)
    cnt = oh1.astype(F32) + oh2.astype(F32)
    pos = _dot(cnt.astype(BF16), upper_ref[...]) + base_ref[:, 0:1]
    dest_ref[0:1, :] = jnp.sum(jnp.where(oh1, pos, 0.0), axis=0, keepdims=True).astype(I32)
    dest_ref[1:2, :] = jnp.sum(jnp.where(oh2, pos, 0.0), axis=0, keepdims=True).astype(I32)
    base_ref[...] += jnp.sum(cnt, axis=1, keepdims=True)


def _n_blocks(n_tok):
    return -(-2 * n_tok // MOE_BM) + N_EXPERTS


def _route_call(route, counts):
    n_tok = route.shape[1]
    t = min(ROUTE_T, n_tok)
    nbp = -(-_n_blocks(n_tok) // LANES) * LANES
    cparams = pltpu.CompilerParams(dimension_semantics=("arbitrary",), vmem_limit_bytes=VMEM_LIMIT)
    route_spec = pl.BlockSpec((ROUTE_ROWS, t), lambda j: (0, j))
    cnt_spec = pl.BlockSpec((N_EXPERTS, LANES), lambda j: (0, 0))
    dest, meta = pl.pallas_call(
        _dest_kernel, grid=(n_tok // t,), in_specs=[route_spec, cnt_spec],
        out_specs=[pl.BlockSpec((2, t), lambda j: (0, j)), pl.BlockSpec((8, nbp), lambda j: (0, 0))],
        out_shape=[jax.ShapeDtypeStruct((2, n_tok), I32), jax.ShapeDtypeStruct((8, nbp), I32)],
        scratch_shapes=[pltpu.VMEM((N_EXPERTS, LANES), F32), pltpu.VMEM((t, t), BF16)],
        compiler_params=cparams, name="route_dest")(route, counts)
    return dest, meta


def _zero_fill_slots(meta_ref, xs_hbm, ztile, zsem, wait):
    nbp = meta_ref.shape[0] // 8
    n_blocks = xs_hbm.shape[0] // (MOE_BM * ROW_SUB)

    def go(rows, first_slot):
        cp = pltpu.make_async_copy(
            ztile.at[pl.ds(0, rows * ROW_SUB), :],
            xs_hbm.at[pl.ds(pl.multiple_of(first_slot * ROW_SUB, ROW_SUB), rows * ROW_SUB), :], zsem)
        if wait:
            cp.wait()
        else:
            cp.start()

    def expert_pad(e, carry):
        slot = meta_ref[2 * nbp + e]
        plen = meta_ref[3 * nbp + e]
        bit = MOE_BM // 2
        while bit >= 1:
            has = (plen & bit) != 0
            pl.when(has)(functools.partial(go, bit, slot))
            slot = slot + jnp.where(has, bit, 0)
            bit //= 2
        return carry

    def tail_block(j, carry):
        go(MOE_BM, j * MOE_BM)
        return carry

    lax.fori_loop(0, N_EXPERTS, expert_pad, 0)
    lax.fori_loop(meta_ref[nbp], n_blocks, tail_block, 0)


def _dispatch_kernel(dest_ref, meta_ref, h_ref, xs_hbm, sem, zsem, ztile):
    n_tok = dest_ref.shape[0] // 2
    t = h_ref.shape[0] // ROW_SUB
    base = pl.program_id(0) * t
    first = pl.program_id(0) == 0

    @pl.when(first)
    def _():
        ztile[...] = jnp.zeros_like(ztile)
        _zero_fill_slots(meta_ref, xs_hbm, ztile, zsem, wait=False)

    def start(i, carry):
        tok = base + i
        src = _row_tile(h_ref, i)
        pltpu.make_async_copy(src, _row_tile(xs_hbm, dest_ref[tok]), sem).start(priority=0)
        pltpu.make_async_copy(src, _row_tile(xs_hbm, dest_ref[n_tok + tok]), sem).start(priority=1)
        return carry

    lax.fori_loop(0, t, start, 0, unroll=DMA_UNROLL)
    for _ in range(2):
        pltpu.make_async_copy(h_ref, xs_hbm.at[pl.ds(0, t * ROW_SUB), :], sem).wait()

    @pl.when(first)
    def _():
        _zero_fill_slots(meta_ref, xs_hbm, ztile, zsem, wait=True)


def _dispatch_call(dest_flat, meta_flat, h_tok, n_slots):
    n_tok = h_tok.shape[0] // ROW_SUB
    t = min(DISPATCH_T, n_tok)
    return pl.pallas_call(
        _dispatch_kernel,
        grid_spec=pltpu.PrefetchScalarGridSpec(
            num_scalar_prefetch=2, grid=(n_tok // t,),
            in_specs=[pl.BlockSpec((t * ROW_SUB, LANES), lambda i, dr, mr: (i, 0))],
            out_specs=pl.BlockSpec(memory_space=pl.ANY),
            scratch_shapes=[pltpu.SemaphoreType.DMA, pltpu.SemaphoreType.DMA,
                            pltpu.VMEM((MOE_BM * ROW_SUB, LANES), h_tok.dtype)]),
        out_shape=jax.ShapeDtypeStruct((n_slots * ROW_SUB, LANES), h_tok.dtype),
        compiler_params=pltpu.CompilerParams(dimension_semantics=("arbitrary",),
                                             vmem_limit_bytes=VMEM_LIMIT),
        name="moe_dispatch",
    )(dest_flat, meta_flat, h_tok)


def _expert_kernel(meta_ref, xs_ref, wg_hbm, wu_hbm, wd_hbm, yb_ref, wgf, wuf, wdf, wgb, wub, wdb,
                   wsem, slot_ref):
    j = pl.program_id(0)
    nbp = meta_ref.shape[0] // 8
    used = j < meta_ref[nbp]
    new_expert = jnp.logical_or(j == 0, meta_ref[j] != meta_ref[jnp.maximum(j - 1, 0)])

    def fetch(e, slot):
        return [pltpu.make_async_copy(src.at[e], dst.at[slot], wsem.at[slot])
                for src, dst in ((wg_hbm, wgf), (wu_hbm, wuf), (wd_hbm, wdf))]

    @pl.when(j == 0)
    def _():
        slot_ref[0] = 0
        for cp in fetch(meta_ref[0], 0):
            cp.start()

    @pl.when(jnp.logical_and(used, new_expert))
    def _():
        slot = slot_ref[0]
        for cp in fetch(meta_ref[j], slot):
            cp.wait()
        nxt = meta_ref[4 * nbp + j]

        @pl.when(nxt >= 0)
        def _():
            for cp in fetch(nxt, 1 - slot):
                cp.start()

        wgb[...] = wgf[slot].astype(BF16)
        wub[...] = wuf[slot].astype(BF16)
        wdb[...] = wdf[slot].astype(BF16)
        slot_ref[0] = 1 - slot

    @pl.when(used)
    def _():
        sub = MOE_BM // EXPERT_SPLIT
        gu = []
        for i in range(EXPERT_SPLIT):
            x = _load_row_tiles(xs_ref.at[pl.ds(i * sub * ROW_SUB, sub * ROW_SUB), :]).astype(BF16)
            gu.append((_dot(x, wgb[...]), _dot(x, wub[...])))
        for i, (gate, up) in enumerate(gu):
            hid = (jax.nn.silu(gate) * up).astype(BF16)
            _store_row_tiles(yb_ref.at[pl.ds(i * sub * ROW_SUB, sub * ROW_SUB), :], _dot(hid, wdb[...]))

    @pl.when(jnp.logical_not(used))
    def _():
        yb_ref[...] = jnp.zeros_like(yb_ref)


def _expert_call(meta_flat, xs, wg, wu, wd):
    n_slots = xs.shape[0] // ROW_SUB
    d, hid = wg.shape[1:]
    blk = pl.BlockSpec((MOE_BM * ROW_SUB, LANES), lambda j, m: (j, 0))
    return pl.pallas_call(
        _expert_kernel,
        grid_spec=pltpu.PrefetchScalarGridSpec(
            num_scalar_prefetch=1, grid=(n_slots // MOE_BM,),
            in_specs=[blk] + [pl.BlockSpec(memory_space=pl.ANY)] * 3,
            out_specs=blk,
            scratch_shapes=[pltpu.VMEM((2, d, hid), F32), pltpu.VMEM((2, d, hid), F32),
                            pltpu.VMEM((2, hid, d), F32),
                            pltpu.VMEM((d, hid), BF16), pltpu.VMEM((d, hid), BF16),
                            pltpu.VMEM((hid, d), BF16),
                            pltpu.SemaphoreType.DMA((2,)), pltpu.SMEM((1,), I32)]),
        out_shape=jax.ShapeDtypeStruct((n_slots * ROW_SUB, LANES), F32),
        compiler_params=pltpu.CompilerParams(dimension_semantics=("arbitrary",),
                                             vmem_limit_bytes=VMEM_LIMIT),
        name="moe_experts",
    )(meta_flat, xs, wg, wu, wd)


def _combine_kernel(dest_ref, yb_hbm, slab_ref, x1_ref, mod_ref, gf_ref, out_ref, gbuf, sem):
    t = x1_ref.shape[0]
    n_tok = dest_ref.shape[0] // 2
    step = pl.program_id(0)
    slot = step % 2

    def issue(tile, buf):
        def body(r, carry):
            tok = tile * t + r
            pltpu.make_async_copy(_row_tile(yb_hbm, dest_ref[tok]), _row_tile(gbuf.at[buf, 0], r),
                                  sem.at[buf]).start(priority=0)
            pltpu.make_async_copy(_row_tile(yb_hbm, dest_ref[n_tok + tok]), _row_tile(gbuf.at[buf, 1], r),
                                  sem.at[buf]).start(priority=1)
            return carry
        lax.fori_loop(0, t, body, 0, unroll=DMA_UNROLL)

    @pl.when(step == 0)
    def _():
        issue(step, slot)

    @pl.when(step + 1 < pl.num_programs(0))
    def _():
        issue(step + 1, 1 - slot)

    for kk in range(2):
        pltpu.make_async_copy(yb_hbm.at[pl.ds(0, t * ROW_SUB), :], gbuf.at[slot, kk], sem.at[slot]).wait()
    slab = slab_ref[...]
    y = (slab[:, 2:3] * _load_row_tiles(gbuf.at[slot, 0])
         + slab[:, 3:4] * _load_row_tiles(gbuf.at[slot, 1]))
    gate2 = mod_ref[0, 5:6, :]
    x2 = x1_ref[...] + gate2 * y
    out_ref[...] = _rms(x2, gf_ref[...])


def _combine_call(dest_flat, yb, slab, x1_tok, mod3, gf, seq):
    n_tok, d = x1_tok.shape
    t = min(COMBINE_T, seq)
    return pl.pallas_call(
        _combine_kernel,
        grid_spec=pltpu.PrefetchScalarGridSpec(
            num_scalar_prefetch=1, grid=(n_tok // t,),
            in_specs=[pl.BlockSpec(memory_space=pl.ANY),
                      pl.BlockSpec((t, LANES), lambda i, dr: (i, 0)),
                      pl.BlockSpec((t, d), lambda i, dr: (i, 0)),
                      pl.BlockSpec((1,) + mod3.shape[1:], lambda i, dr: ((i * t) // seq, 0, 0)),
                      pl.BlockSpec((1, d), lambda i, dr: (0, 0))],
            out_specs=pl.BlockSpec((t, d), lambda i, dr: (i, 0)),
            scratch_shapes=[pltpu.VMEM((2, 2, t * ROW_SUB, LANES), F32), pltpu.SemaphoreType.DMA((2,))]),
        out_shape=jax.ShapeDtypeStruct((n_tok, d), F32),
        compiler_params=pltpu.CompilerParams(dimension_semantics=("arbitrary",),
                                             vmem_limit_bytes=VMEM_LIMIT),
        name="moe_combine",
    )(dest_flat, yb, slab, x1_tok, mod3, gf.reshape(1, d))


def kernel(x, c, ada_w, ada_b, norm1_g, w_in, gmlp_ln_w, gmlp_ln_b, gmlp_ws, gmlp_bs, rwkv_mu, rwkv_w0,
           rwkv_w2, rwkv_a0, rwkv_a2, rwkv_g2, rwkv_k_k, rwkv_k_a, rwkv_r_k, rwkv_gn_w, rwkv_gn_b, w_out,
           norm2_g, router_group_w, router_group_b, router_expert_w, router_expert_b, moe_w_gate,
           moe_w_up, moe_w_down, final_norm_g):
    bsz, seq, d = x.shape
    n_tok = bsz * seq
    assert ada_w.shape[0] == 1, "the final rmsnorm is fused into the (single) layer's combine kernel"
    layer = lambda a: a.reshape(a.shape[1:])
    p = dict(norm1_g=layer(norm1_g), w_in=layer(w_in), gmlp_ln_w=layer(gmlp_ln_w), gmlp_ln_b=layer(gmlp_ln_b),
             gmlp_ws=layer(gmlp_ws), gmlp_bs=layer(gmlp_bs), rwkv_mu=layer(rwkv_mu), rwkv_w0=layer(rwkv_w0),
             rwkv_w2=layer(rwkv_w2), rwkv_a0=layer(rwkv_a0), rwkv_a2=layer(rwkv_a2), rwkv_g2=layer(rwkv_g2),
             rwkv_k_k=layer(rwkv_k_k), rwkv_k_a=layer(rwkv_k_a), rwkv_r_k=layer(rwkv_r_k),
             rwkv_gn_w=layer(rwkv_gn_w), rwkv_gn_b=layer(rwkv_gn_b), w_out=layer(w_out), norm2_g=layer(norm2_g),
             router_group_w=layer(router_group_w), router_group_b=layer(router_group_b),
             router_expert_w=layer(router_expert_w), router_expert_b=layer(router_expert_b))
    mod3 = _mod_call(c, layer(ada_w), layer(ada_b)).reshape(bsz, 6, d)
    pre = _pre_call(x, mod3, p)
    x1, h2, route, slab, counts = _scan_call(x, mod3, pre, p)
    dest, meta = _route_call(route, counts)
    dest_flat = dest.reshape(-1)
    n_slots = _n_blocks(n_tok) * MOE_BM
    meta_flat = meta.reshape(-1)
    xs = _dispatch_call(dest_flat, meta_flat, h2.reshape(n_tok * ROW_SUB, LANES), n_slots)
    yb = _expert_call(meta_flat, xs, layer(moe_w_gate), layer(moe_w_up), layer(moe_w_down))
    out = _combine_call(dest_flat, yb, slab, x1.reshape(n_tok, d), mod3, final_norm_g, seq)
    return out.reshape(bsz, seq, d)
```
